```python
import jax, jax.numpy as jnp
from jax import lax
import numpy as np

D_MODEL = 2048
BATCH = 16
SEQ = 256
DEPTH = 4
DEC_BATCH = 4
DEC_SEQ = 1024
PAST_LEN = 256

GRID_W = 64
EPS = 1e-6
ROPE_BASE = 10000.0
Q_BLOCK = 128
D_RNN = D_MODEL // 2
LRU_BLOCKS = 8
LRU_BS = D_RNN // LRU_BLOCKS
CONV_W = 4
LRU_C = 8.0
MLA_HEADS = 8
MLA_NOPE = 128
MLA_ROPE = 64
MLA_V = 128
MLA_Q_RANK = D_MODEL // 4
MLA_KV_RANK = D_MODEL // 8
RET_HEADS = 8
RET_DK = 128
RET_DV = 128
RET_CHUNK = 128
D_FF = 4 * D_MODEL
N_BRANCH = 3

COL_SIZES = (D_RNN, D_RNN, MLA_Q_RANK, MLA_KV_RANK, MLA_ROPE,
             RET_HEADS * RET_DK, RET_HEADS * RET_DK, RET_HEADS * RET_DV, RET_HEADS * RET_DV,
             N_BRANCH * D_MODEL)
N_IN = sum(COL_SIZES)
SPLITS = tuple(sum(COL_SIZES[:i + 1]) for i in range(len(COL_SIZES) - 1))

kernel_name = "hybrid_lru_mla_retention_dit_step"


def rms_norm(x, g):
    xf = x.astype(jnp.float32)
    y = xf * lax.rsqrt(jnp.mean(xf * xf, axis=-1, keepdims=True) + EPS)
    return (y * g.astype(jnp.float32)).astype(x.dtype)


def head_norm(x):
    xf = x.astype(jnp.float32)
    mu = jnp.mean(xf, axis=-1, keepdims=True)
    var = jnp.mean(jnp.square(xf - mu), axis=-1, keepdims=True)
    return (xf - mu) * lax.rsqrt(var + EPS)


def modulate(h, shift, scale):
    return h * (1.0 + scale) + shift


def ada_params(cond, w_mod, b_mod):
    return jnp.split(jax.nn.silu(cond) @ w_mod + b_mod, 6, axis=-1)


def axial_rope(x):
    T = x.shape[1]
    half = x.shape[-1] // 2
    rows = T // GRID_W
    row = jnp.repeat(jnp.arange(rows), GRID_W).astype(jnp.float32)
    col = jnp.tile(jnp.arange(GRID_W), rows).astype(jnp.float32)
    inv = ROPE_BASE ** (-jnp.arange(0, half, 2, dtype=jnp.float32) / half)
    bshape = (1, T) + (1,) * (x.ndim - 3) + (half // 2,)

    def rot(xp, pos):
        ang = (pos[:, None] * inv[None, :]).reshape(bshape)
        cos, sin = jnp.cos(ang), jnp.sin(ang)
        x1, x2 = jnp.split(xp.astype(jnp.float32), 2, axis=-1)
        return jnp.concatenate([x1 * cos - x2 * sin, x2 * cos + x1 * sin], axis=-1)

    return jnp.concatenate([rot(x[..., :half], row), rot(x[..., half:], col)], axis=-1).astype(x.dtype)


def rglru_direction(xb, conv_w, conv_b, wa, ba, wx, bx, lam, h0):
    B, T, _ = xb.shape
    xc = lax.conv_general_dilated(xb, conv_w[:, None, :].astype(xb.dtype), window_strides=(1,),
                                  padding=[(CONV_W - 1, 0)], dimension_numbers=("NWC", "WIO", "NWC"),
                                  feature_group_count=D_RNN) + conv_b
    xr = xc.reshape(B, T, LRU_BLOCKS, LRU_BS)
    r = jax.nn.sigmoid((jnp.einsum("btnc,ncd->btnd", xr, wa).reshape(B, T, D_RNN) + ba).astype(jnp.float32))
    i = jax.nn.sigmoid((jnp.einsum("btnc,ncd->btnd", xr, wx).reshape(B, T, D_RNN) + bx).astype(jnp.float32))
    log_a = -LRU_C * r * jax.nn.softplus(-lam.astype(jnp.float32))
    a = jnp.exp(log_a)
    b = jnp.sqrt(-jnp.expm1(2.0 * log_a)) * (i * xc.astype(jnp.float32))

    def step(h, ab):
        h = ab[0] * h + ab[1]
        return h, h

    h_last, hs = lax.scan(step, h0.astype(jnp.float32), (a.swapaxes(0, 1), b.swapaxes(0, 1)))
    return hs.swapaxes(0, 1).astype(xb.dtype), h_last


def rglru_branch(x_rnn, x_gate, p, h0_f, h0_b):
    dirs = []
    for d, (xd, h0) in enumerate(((x_rnn, h0_f), (x_rnn[:, ::-1], h0_b))):
        dirs.append(rglru_direction(xd, p["lru_conv_w"][d], p["lru_conv_b"][d], p["lru_wa"][d], p["lru_ba"][d],
                                    p["lru_wx"][d], p["lru_bx"][d], p["lru_lam"][d], h0))
    (y_f, h_f), (y_b, h_b) = dirs
    y = (y_f + y_b[:, ::-1]) * jax.nn.gelu(x_gate)
    return y @ p["w_br_lru"], jnp.stack([h_f, h_b], axis=1)


def retention_direction(q, k, v, log_g, s0):
    B, T, H, _ = q.shape
    n_chunks = T // RET_CHUNK

    def chunks(t):
        return t.astype(jnp.float32).reshape(B, n_chunks, RET_CHUNK, H, t.shape[-1]).swapaxes(0, 1)

    pos = jnp.arange(RET_CHUNK, dtype=jnp.float32)
    diff = pos[:, None] - pos[None, :]
    inner_decay = jnp.where(diff >= 0, jnp.exp(log_g[:, None, None] * jnp.maximum(diff, 0.0)), 0.0)
    q_decay = jnp.exp(log_g[None, :] * (pos[:, None] + 1.0))
    k_decay = jnp.exp(log_g[None, :] * (RET_CHUNK - 1.0 - pos[:, None]))
    chunk_decay = jnp.exp(log_g * RET_CHUNK)

    def step(s, qkv):
        qc, kc, vc = qkv
        scores = jnp.einsum("bqhd,bkhd->bhqk", qc, kc) * inner_decay
        o = (jnp.einsum("bhqk,bkhe->bqhe", scores, vc)
             + jnp.einsum("bqhd,bhde->bqhe", qc * q_decay[None, :, :, None], s))
        s = (chunk_decay[None, :, None, None] * s
             + jnp.einsum("bkhd,bkhe->bhde", kc * k_decay[None, :, :, None], vc))
        return s, o

    s_last, o = lax.scan(step, s0.astype(jnp.float32), (chunks(q), chunks(k), chunks(v)))
    return o.swapaxes(0, 1).reshape(B, T, H, v.shape[-1]), s_last


def retention_branch(rq, rk, rv, rg, decay_logit, w_br, s0_f, s0_b, rotate):
    B, T, _ = rq.shape
    q = rq.reshape(B, T, RET_HEADS, RET_DK)
    k = rk.reshape(B, T, RET_HEADS, RET_DK) * (RET_DK ** -0.5)
    v = rv.reshape(B, T, RET_HEADS, RET_DV)
    if rotate:
        q, k = axial_rope(q), axial_rope(k)
    log_g = jax.nn.log_sigmoid(decay_logit.astype(jnp.float32))
    o_f, s_f = retention_direction(q, k, v, log_g[0], s0_f)
    o_b, s_b = retention_direction(q[:, ::-1], k[:, ::-1], v[:, ::-1], log_g[1], s0_b)
    o = head_norm(o_f + o_b[:, ::-1]).reshape(B, T, RET_HEADS * RET_DV).astype(rg.dtype) * jax.nn.silu(rg)
    return o @ w_br, jnp.stack([s_f, s_b], axis=1)


def mla_queries(cq_raw, g_q, w_uq, rotate):
    B, T, _ = cq_raw.shape
    q = (rms_norm(cq_raw, g_q) @ w_uq).reshape(B, T, MLA_HEADS, MLA_NOPE + MLA_ROPE)
    q_nope, q_rope = q[..., :MLA_NOPE], q[..., MLA_NOPE:]
    if rotate:
        q_rope = axial_rope(q_rope)
    return q_nope, q_rope


def mla_attend(q_nope, q_rope, ckv, k_rope, w_ukv, w_br):
    B, T = q_nope.shape[:2]
    S = ckv.shape[1]
    kv = (ckv @ w_ukv).reshape(B, S, MLA_HEADS, MLA_NOPE + MLA_V)
    k_nope, v = kv[..., :MLA_NOPE], kv[..., MLA_NOPE:]
    scale = (MLA_NOPE + MLA_ROPE) ** -0.5
    n_blocks = T // Q_BLOCK

    def blocks(t):
        return t.reshape((B, n_blocks, Q_BLOCK) + t.shape[2:]).swapaxes(0, 1)

    def attend(qs):
        qn, qr = qs
        s = jnp.einsum("bqhd,bkhd->bhqk", qn, k_nope) + jnp.einsum("bqhd,bkd->bhqk", qr, k_rope)
        pr = jax.nn.softmax(s.astype(jnp.float32) * scale, axis=-1).astype(v.dtype)
        return jnp.einsum("bhqk,bkhd->bqhd", pr, v)

    o = lax.map(attend, (blocks(q_nope), blocks(q_rope)))
    o = o.swapaxes(0, 1).reshape(B, T, MLA_HEADS * MLA_V)
    return o @ w_br


def merge_branches(gate_logits, u_lru, u_mla, u_ret, w_out):
    g = jax.nn.sigmoid(gate_logits.astype(jnp.float32)).astype(u_lru.dtype)
    g_lru, g_mla, g_ret = jnp.split(g, N_BRANCH, axis=-1)
    return (g_lru * u_lru + g_mla * u_mla + g_ret * u_ret) @ w_out


def mixer_context(h, p):
    x_rnn, x_gate, cq, ckv_raw, kr, rq, rk, rv, rg, gates = jnp.split(h @ p["w_in"], SPLITS, axis=-1)
    B = h.shape[0]
    h0 = jnp.zeros((B, D_RNN), jnp.float32)
    u_lru, st_lru = rglru_branch(x_rnn, x_gate, p, h0, h0)
    ckv = rms_norm(ckv_raw, p["mla_gkv"])
    qn, qr = mla_queries(cq, p["mla_gq"], p["mla_wuq"], rotate=False)
    u_mla = mla_attend(qn, qr, ckv, kr, p["mla_wukv"], p["w_br_mla"])
    s0 = jnp.zeros((B, RET_HEADS, RET_DK, RET_DV), jnp.float32)
    u_ret, st_ret = retention_branch(rq, rk, rv, rg, p["ret_decay"], p["w_br_ret"], s0, s0, rotate=False)
    return merge_branches(gates, u_lru, u_mla, u_ret, p["w_out"]), ckv, kr, st_lru, st_ret


def mixer_latent(h, p, ckv_ctx, kr_ctx, st_lru, st_ret):
    x_rnn, x_gate, cq, ckv_raw, kr, rq, rk, rv, rg, gates = jnp.split(h @ p["w_in"], SPLITS, axis=-1)
    u_lru, _ = rglru_branch(x_rnn, x_gate, p, st_lru[:, 0], st_lru[:, 1])
    ckv = jnp.concatenate([ckv_ctx.astype(h.dtype), rms_norm(ckv_raw, p["mla_gkv"])], axis=1)
    k_rope = jnp.concatenate([kr_ctx.astype(h.dtype), axial_rope(kr)], axis=1)
    qn, qr = mla_queries(cq, p["mla_gq"], p["mla_wuq"], rotate=True)
    u_mla = mla_attend(qn, qr, ckv, k_rope, p["mla_wukv"], p["w_br_mla"])
    u_ret, _ = retention_branch(rq, rk, rv, rg, p["ret_decay"], p["w_br_ret"], st_ret[:, 0], st_ret[:, 1],
                                rotate=True)
    return merge_branches(gates, u_lru, u_mla, u_ret, p["w_out"])


def ffn_sublayer(x, shift, scale, gate, g_pre, g_post, w1, w2):
    h = modulate(rms_norm(x, g_pre), shift, scale)
    y = jnp.square(jax.nn.relu(h @ w1)) @ w2
    return x + gate * rms_norm(y, g_post)


def setup_inputs(seed: int = 0) -> dict:
    key = jax.random.key(seed)
    ks = iter(jax.random.split(key, 40))
    D = D_MODEL

    def nrm(shape, scale):
        return jax.random.normal(next(ks), shape, jnp.float32) * scale

    x_prompt = nrm((BATCH, SEQ, D), 1.0)
    x_sample = nrm((DEC_BATCH, DEC_SEQ, D), 1.0)
    c = nrm((DEC_BATCH, D), 1.0)
    cache_mla_ckv = nrm((DEC_BATCH, DEPTH, PAST_LEN, MLA_KV_RANK), 1.0)
    cache_mla_krope = nrm((DEC_BATCH, DEPTH, PAST_LEN, MLA_ROPE), 1.0)
    state_lru = nrm((DEC_BATCH, DEPTH, 2, D_RNN), 0.5)
    state_ret = nrm((DEC_BATCH, DEPTH, 2, RET_HEADS, RET_DK, RET_DV), RET_DK ** -0.5)
    c_ctx = nrm((D,), 1.0)
    w_mod = nrm((DEPTH, D, 6 * D), 0.5 * D ** -0.5)
    b_mod = nrm((DEPTH, 6 * D), 0.01)
    g_norm = 1.0 + nrm((DEPTH, 4, D), 0.01)
    w_in = nrm((DEPTH, D, N_IN), D ** -0.5)
    lru_conv_w = nrm((DEPTH, 2, CONV_W, D_RNN), CONV_W ** -0.5)
    lru_conv_b = nrm((DEPTH, 2, D_RNN), 0.01)
    lru_wa = nrm((DEPTH, 2, LRU_BLOCKS, LRU_BS, LRU_BS), LRU_BS ** -0.5)
    lru_ba = nrm((DEPTH, 2, D_RNN), 0.01)
    lru_wx = nrm((DEPTH, 2, LRU_BLOCKS, LRU_BS, LRU_BS), LRU_BS ** -0.5)
    lru_bx = nrm((DEPTH, 2, D_RNN), 0.01)
    a_base = jax.random.uniform(next(ks), (DEPTH, 2, D_RNN), jnp.float32, 0.9, 0.999) ** (1.0 / LRU_C)
    lru_lam = jnp.log(a_base) - jnp.log1p(-a_base)
    mla_gq = 1.0 + nrm((DEPTH, MLA_Q_RANK), 0.01)
    mla_gkv = 1.0 + nrm((DEPTH, MLA_KV_RANK), 0.01)
    mla_wuq = nrm((DEPTH, MLA_Q_RANK, MLA_HEADS * (MLA_NOPE + MLA_ROPE)), MLA_Q_RANK ** -0.5)
    mla_wukv = nrm((DEPTH, MLA_KV_RANK, MLA_HEADS * (MLA_NOPE + MLA_V)), MLA_KV_RANK ** -0.5)
    gamma = 1.0 - 2.0 ** (-5.0 - jnp.arange(RET_HEADS, dtype=jnp.float32))
    ret_decay = jnp.log(gamma) - jnp.log1p(-gamma) + nrm((DEPTH, 2, RET_HEADS), 0.01)
    w_br_lru = nrm((DEPTH, D_RNN, D), D_RNN ** -0.5)
    w_br_mla = nrm((DEPTH, MLA_HEADS * MLA_V, D), (MLA_HEADS * MLA_V) ** -0.5)
    w_br_ret = nrm((DEPTH, RET_HEADS * RET_DV, D), (RET_HEADS * RET_DV) ** -0.5)
    w_out = nrm((DEPTH, D, D), D ** -0.5)
    w_ff1 = nrm((DEPTH, D, D_FF), D ** -0.5)
    w_ff2 = nrm((DEPTH, D_FF, D), D_FF ** -0.5)
    return {"x_prompt": x_prompt, "x_sample": x_sample, "c": c,
            "cache_mla_ckv": cache_mla_ckv, "cache_mla_krope": cache_mla_krope,
            "state_lru": state_lru, "state_ret": state_ret,
            "c_ctx": c_ctx, "w_mod": w_mod, "b_mod": b_mod, "g_norm": g_norm, "w_in": w_in,
            "lru_conv_w": lru_conv_w, "lru_conv_b": lru_conv_b, "lru_wa": lru_wa, "lru_ba": lru_ba,
            "lru_wx": lru_wx, "lru_bx": lru_bx, "lru_lam": lru_lam,
            "mla_gq": mla_gq, "mla_gkv": mla_gkv, "mla_wuq": mla_wuq, "mla_wukv": mla_wukv,
            "ret_decay": ret_decay, "w_br_lru": w_br_lru, "w_br_mla": w_br_mla, "w_br_ret": w_br_ret,
            "w_out": w_out, "w_ff1": w_ff1, "w_ff2": w_ff2}


def reference(x_prompt, x_sample, c, cache_mla_ckv, cache_mla_krope, state_lru, state_ret,
              c_ctx, w_mod, b_mod, g_norm, w_in, lru_conv_w, lru_conv_b, lru_wa, lru_ba, lru_wx, lru_bx,
              lru_lam, mla_gq, mla_gkv, mla_wuq, mla_wukv, ret_decay, w_br_lru, w_br_mla, w_br_ret,
              w_out, w_ff1, w_ff2):
    xp, xs = x_prompt, x_sample
    cond_ctx = c_ctx[None, None, :]
    cond_lat = c[:, None, :]
    ckv_out, kr_out, lru_out, ret_out = [], [], [], []
    for l in range(DEPTH):
        p = {"w_in": w_in[l], "lru_conv_w": lru_conv_w[l], "lru_conv_b": lru_conv_b[l],
             "lru_wa": lru_wa[l], "lru_ba": lru_ba[l], "lru_wx": lru_wx[l], "lru_bx": lru_bx[l],
             "lru_lam": lru_lam[l], "mla_gq": mla_gq[l], "mla_gkv": mla_gkv[l], "mla_wuq": mla_wuq[l],
             "mla_wukv": mla_wukv[l], "ret_decay": ret_decay[l], "w_br_lru": w_br_lru[l],
             "w_br_mla": w_br_mla[l], "w_br_ret": w_br_ret[l], "w_out": w_out[l]}

        sh1, sc1, ga1, sh2, sc2, ga2 = ada_params(cond_ctx, w_mod[l], b_mod[l])
        h = modulate(rms_norm(xp, g_norm[l, 0]), sh1, sc1)
        u, ckv, kr, st_l, st_r = mixer_context(h, p)
        xp = xp + ga1 * rms_norm(u, g_norm[l, 1])
        xp = ffn_sublayer(xp, sh2, sc2, ga2, g_norm[l, 2], g_norm[l, 3], w_ff1[l], w_ff2[l])
        ckv_out.append(ckv)
        kr_out.append(kr)
        lru_out.append(st_l.astype(x_prompt.dtype))
        ret_out.append(st_r.astype(x_prompt.dtype))

        sh1, sc1, ga1, sh2, sc2, ga2 = ada_params(cond_lat, w_mod[l], b_mod[l])
        h = modulate(rms_norm(xs, g_norm[l, 0]), sh1, sc1)
        u = mixer_latent(h, p, cache_mla_ckv[:, l], cache_mla_krope[:, l], state_lru[:, l], state_ret[:, l])
        xs = xs + ga1 * rms_norm(u, g_norm[l, 1])
        xs = ffn_sublayer(xs, sh2, sc2, ga2, g_norm[l, 2], g_norm[l, 3], w_ff1[l], w_ff2[l])

    y_prompt, y_sample = xp, xs
    new_mla_ckv = jnp.stack(ckv_out, axis=1)
    new_mla_krope = jnp.stack(kr_out, axis=1)
    new_state_lru = jnp.stack(lru_out, axis=1)
    new_state_ret = jnp.stack(ret_out, axis=1)
    return (y_prompt, y_sample, new_mla_ckv, new_mla_krope, new_state_lru, new_state_ret)
```

```python
import functools

import jax
import jax.numpy as jnp
from jax import lax
from jax.experimental import pallas as pl
from jax.experimental.pallas import tpu as pltpu

F32 = jnp.float32
BF16 = jnp.bfloat16

D_MODEL = 2048
BATCH = 16
SEQ = 256
DEPTH = 4
DEC_BATCH = 4
DEC_SEQ = 1024
PAST_LEN = 256
GRID_W = 64
EPS = 1e-6
ROPE_BASE = 10000.0
D_RNN = D_MODEL // 2
LRU_BLOCKS = 8
LRU_BS = D_RNN // LRU_BLOCKS
CONV_W = 4
LRU_C = 8.0
MLA_HEADS = 8
MLA_NOPE = 128
MLA_ROPE = 64
MLA_V = 128
MLA_Q_RANK = D_MODEL // 4
MLA_KV_RANK = D_MODEL // 8
RET_HEADS = 8
RET_DK = 128
RET_DV = 128
D_FF = 4 * D_MODEL

N_CTX = BATCH * SEQ
N_LAT = DEC_BATCH * DEC_SEQ
N_TOK = N_CTX + N_LAT
COL_TAIL = 2 * D_RNN + MLA_Q_RANK + MLA_KV_RANK + MLA_ROPE
MLA_QH = 256

LANES = 128
SUBLANES = 8
VMEM_LIMIT = 56 * 1024 * 1024
ROW_TILE = 256


def _params(*sem):
    return pltpu.CompilerParams(dimension_semantics=sem, vmem_limit_bytes=VMEM_LIMIT)


def _sigmoid(x):
    return 1.0 / (1.0 + jnp.exp(-x))


def _softplus(x):
    return jnp.maximum(x, 0.0) + jnp.log(1.0 + jnp.exp(-jnp.abs(x)))


def _rms(x, g):
    return x * lax.rsqrt(jnp.mean(x * x, axis=-1, keepdims=True) + EPS) * g


def _dot(a, b):
    return jnp.dot(a, b, preferred_element_type=F32)


def _dot_nt(a, b):
    return lax.dot_general(a, b, (((1,), (1,)), ((), ())), preferred_element_type=F32)


def _dot_tn(a, b):
    return lax.dot_general(a, b, (((0,), (0,)), ((), ())), preferred_element_type=F32)


def _rope(x, cos, sa, sb, quarter):
    w = x.shape[-1]
    return x * cos + pltpu.roll(x, quarter, 1) * sa + pltpu.roll(x, w - quarter, 1) * sb


def _mod_row(row0):
    return jnp.where(row0 < N_CTX, 0, 1 + (row0 - N_CTX) // DEC_SEQ)


def _ada_kernel(c_ref, w_ref, b_ref, o_ref):
    c = c_ref[...]
    s = c * _sigmoid(c)
    o_ref[...] = _dot(s.astype(BF16), w_ref[...].astype(BF16)) + b_ref[...]


def _ada(cond8, w_mod, b_mod):
    tn = 1024
    n = w_mod.shape[-1]
    return pl.pallas_call(
        _ada_kernel,
        grid=(DEPTH, n // tn),
        in_specs=[pl.BlockSpec((8, D_MODEL), lambda l, j: (0, 0)),
                  pl.BlockSpec((None, D_MODEL, tn), lambda l, j: (l, 0, j)),
                  pl.BlockSpec((None, 1, tn), lambda l, j: (l, 0, j))],
        out_specs=pl.BlockSpec((None, 8, tn), lambda l, j: (l, 0, j)),
        out_shape=jax.ShapeDtypeStruct((DEPTH, 8, n), F32),
        compiler_params=_params("arbitrary", "arbitrary"),
        name="ada",
    )(cond8, w_mod, b_mod.reshape(DEPTH, 1, n))


def _resnorm_kernel(*refs, has_res, has_next):
    refs = list(refs)
    x_ref = refs.pop(0)
    x = x_ref[...]
    if has_res:
        u_ref, gpost_ref, gate_ref = refs[:3]
        refs = refs[3:]
        x = x + gate_ref[...] * _rms(u_ref[...], gpost_ref[...])
    if has_next:
        gpre_ref, shift_ref, scale_ref = refs[:3]
        refs = refs[3:]
    if has_res:
        xo_ref = refs.pop(0)
        xo_ref[...] = x
    if has_next:
        ho_ref = refs.pop(0)
        h = _rms(x, gpre_ref[...]) * (1.0 + scale_ref[...]) + shift_ref[...]
        ho_ref[...] = h.astype(BF16)


def _resnorm(x, mod, g_norm, *, res=None, nxt=None):
    tr = ROW_TILE
    row = pl.BlockSpec((tr, D_MODEL), lambda i: (i, 0))

    def gspec(k):
        return pl.BlockSpec((None, 1, D_MODEL), lambda i: (k, 0, 0))

    def mspec(l, chunk):
        return pl.BlockSpec((None, 1, D_MODEL), lambda i: (l * 8 + _mod_row(i * tr), 0, chunk))

    args, specs, outs, ospecs = [x], [row], [], []
    if res is not None:
        u, gk, l, chunk = res
        args += [u, g_norm, mod]
        specs += [row, gspec(gk), mspec(l, chunk)]
        outs.append(jax.ShapeDtypeStruct((N_TOK, D_MODEL), F32))
        ospecs.append(row)
    if nxt is not None:
        gk, l, chunk = nxt
        args += [g_norm, mod, mod]
        specs += [gspec(gk), mspec(l, chunk), mspec(l, chunk + 1)]
        outs.append(jax.ShapeDtypeStruct((N_TOK, D_MODEL), BF16))
        ospecs.append(row)
    res_out = pl.pallas_call(
        functools.partial(_resnorm_kernel, has_res=res is not None, has_next=nxt is not None),
        grid=(N_TOK // tr,),
        in_specs=specs, out_specs=ospecs, out_shape=outs,
        compiler_params=_params("arbitrary"),
        name="resnorm",
    )(*args)
    return res_out


def _mm_kernel(x_ref, w_ref, o_ref, *scratch, nk, act):
    def finish(acc):
        if act == "relu2":
            acc = jnp.square(jnp.maximum(acc, 0.0))
        elif act == "sigmoid":
            acc = _sigmoid(acc)
        o_ref[...] = acc.astype(o_ref.dtype)

    part = _dot(x_ref[...].astype(BF16), w_ref[...].astype(BF16))
    if nk == 1:
        finish(part)
        return
    acc_ref, = scratch
    k = pl.program_id(2)

    @pl.when(k == 0)
    def _():
        acc_ref[...] = part

    @pl.when(k > 0)
    def _():
        acc_ref[...] += part

    @pl.when(k == nk - 1)
    def _():
        finish(acc_ref[...])


def _mm(x, w, l, *, col0=0, ncols=None, out_dtype=F32, act=None, tm=1024, tn=1024, tk=512,
        m=None, row_map=None):
    kdim = w.shape[1]
    ncols = w.shape[2] - col0 if ncols is None else ncols
    m = x.shape[0] if m is None else m
    tm, tn, tk = min(tm, m), min(tn, ncols), min(tk, kdim)
    assert m % tm == 0 and ncols % tn == 0 and kdim % tk == 0 and col0 % tn == 0
    nk = kdim // tk
    jb = col0 // tn
    row_map = (lambda i: i) if row_map is None else row_map
    return pl.pallas_call(
        functools.partial(_mm_kernel, nk=nk, act=act),
        grid=(m // tm, ncols // tn, nk),
        in_specs=[pl.BlockSpec((tm, tk), lambda i, j, k: (row_map(i), k)),
                  pl.BlockSpec((None, tk, tn), lambda i, j, k: (l, k, jb + j))],
        out_specs=pl.BlockSpec((tm, tn), lambda i, j, k: (i, j)),
        out_shape=jax.ShapeDtypeStruct((m, ncols), out_dtype),
        scratch_shapes=[] if nk == 1 else [pltpu.VMEM((tm, tn), F32)],
        compiler_params=_params("arbitrary", "arbitrary", "arbitrary"),
        name="mm",
    )(x, w)


LRU_CT = 512


def _lru_kernel(*refs, T, has_h0):
    if has_h0:
        (xr_ref, xg_ref, cw_ref, cb_ref, wa_ref, ba_ref, wx_ref, bx_ref, lam_ref, h0_ref,
         y_ref, st_ref, xpad_ref, a_ref, b_ref, h_ref) = refs
    else:
        (xr_ref, xg_ref, cw_ref, cb_ref, wa_ref, ba_ref, wx_ref, bx_ref, lam_ref,
         y_ref, st_ref, xpad_ref, a_ref, b_ref, h_ref) = refs
    ct = xr_ref.shape[1]
    groups = T // SUBLANES
    pad = SUBLANES
    xpad_ref[0:pad, :] = jnp.zeros((pad, ct), F32)
    xpad_ref[pad + T:2 * pad + T, :] = jnp.zeros((pad, ct), F32)
    xpad_ref[pad:pad + T, :] = xr_ref[...]
    row8 = lax.broadcasted_iota(jnp.int32, (SUBLANES, ct), 0)

    for d in range(2):
        for n in range(ct // LRU_BS):
            cs = slice(n * LRU_BS, (n + 1) * LRU_BS)
            xc = jnp.zeros((T, LRU_BS), F32) + cb_ref[d, :, cs]
            for j in range(CONV_W):
                off = (j - (CONV_W - 1)) if d == 0 else ((CONV_W - 1) - j)
                xc = xc + cw_ref[d, j:j + 1, cs] * xpad_ref[pad + off:pad + off + T, cs]
            xcb = xc.astype(BF16)
            r = _sigmoid(_dot(xcb, wa_ref[d, n].astype(BF16)) + ba_ref[d, :, cs])
            i = _sigmoid(_dot(xcb, wx_ref[d, n].astype(BF16)) + bx_ref[d, :, cs])
            log_a = (-LRU_C) * r * _softplus(-lam_ref[d, :, cs])
            a = jnp.exp(log_a)
            a_ref[:, cs] = a
            b_ref[:, cs] = jnp.sqrt(1.0 - a * a) * (i * xc)

        h0 = h0_ref[d] if has_h0 else jnp.zeros((1, ct), F32)

        def body(it, carry, d=d):
            g = it if d == 0 else groups - 1 - it
            r0 = pl.multiple_of(g * SUBLANES, SUBLANES)
            av = a_ref[pl.ds(r0, SUBLANES), :]
            bv = b_ref[pl.ds(r0, SUBLANES), :]
            for s in (1, 2, 4):
                if d == 0:
                    ok = row8 >= s
                    shift = s
                else:
                    ok = row8 < SUBLANES - s
                    shift = SUBLANES - s
                a_sh = jnp.where(ok, pltpu.roll(av, shift, 0), 1.0)
                b_sh = jnp.where(ok, pltpu.roll(bv, shift, 0), 0.0)
                bv = av * b_sh + bv
                av = av * a_sh
            h = av * carry + bv
            if d == 0:
                h_ref[pl.ds(r0, SUBLANES), :] = h
                return h[SUBLANES - 1:SUBLANES, :]
            h_ref[pl.ds(r0, SUBLANES), :] = h_ref[pl.ds(r0, SUBLANES), :] + h
            return h[0:1, :]

        last = lax.fori_loop(0, groups, body, h0)
        st_ref[d] = last

    xg = xg_ref[...]
    gelu = 0.5 * xg * (1.0 + jnp.tanh(0.7978845608028654 * (xg + 0.044715 * (xg * xg * xg))))
    y_ref[...] = (h_ref[...] * gelu).astype(BF16)


def _lru(proj_a, lw, l, *, T, nb, blk0, h0=None):
    ct = LRU_CT
    ncb = D_RNN // ct
    bpc = ct // LRU_BS
    in_specs = [
        pl.BlockSpec((T, ct), lambda b, c: (blk0 + b, c)),
        pl.BlockSpec((T, ct), lambda b, c: (blk0 + b, ncb + c)),
        pl.BlockSpec((None, 2, CONV_W, ct), lambda b, c: (l, 0, 0, c)),
        pl.BlockSpec((None, 2, 1, ct), lambda b, c: (l, 0, 0, c)),
        pl.BlockSpec((None, 2, bpc, LRU_BS, LRU_BS), lambda b, c: (l, 0, c, 0, 0)),
        pl.BlockSpec((None, 2, 1, ct), lambda b, c: (l, 0, 0, c)),
        pl.BlockSpec((None, 2, bpc, LRU_BS, LRU_BS), lambda b, c: (l, 0, c, 0, 0)),
        pl.BlockSpec((None, 2, 1, ct), lambda b, c: (l, 0, 0, c)),
        pl.BlockSpec((None, 2, 1, ct), lambda b, c: (l, 0, 0, c)),
    ]
    args = [proj_a, proj_a, lw["conv_w"], lw["conv_b"], lw["wa"], lw["ba"], lw["wx"], lw["bx"], lw["lam"]]
    if h0 is not None:
        in_specs.append(pl.BlockSpec((None, None, 2, 1, ct), lambda b, c: (b, l, 0, 0, c)))
        args.append(h0)
    return pl.pallas_call(
        functools.partial(_lru_kernel, T=T, has_h0=h0 is not None),
        grid=(nb, ncb),
        in_specs=in_specs,
        out_specs=[pl.BlockSpec((T, ct), lambda b, c: (b, c)),
                   pl.BlockSpec((None, 2, 1, ct), lambda b, c: (b, 0, 0, c))],
        out_shape=[jax.ShapeDtypeStruct((nb * T, D_RNN), BF16),
                   jax.ShapeDtypeStruct((nb, 2, 1, D_RNN), F32)],
        scratch_shapes=[pltpu.VMEM((T + 2 * SUBLANES, ct), F32), pltpu.VMEM((T, ct), F32),
                        pltpu.VMEM((T, ct), F32), pltpu.VMEM((T, ct), F32)],
        compiler_params=_params("arbitrary", "arbitrary"),
        name="lru",
    )(*args)


def _mla_prep_kernel(p_ref, gq_ref, gkv_ref, cos_ref, sa_ref, sb_ref, cq_ref, ckv_ref, ckvb_ref, kr_ref):
    i = pl.program_id(0)
    cq_ref[...] = _rms(p_ref[:, 0:MLA_Q_RANK], gq_ref[...]).astype(BF16)
    ckv = _rms(p_ref[:, MLA_Q_RANK:MLA_Q_RANK + MLA_KV_RANK], gkv_ref[...])
    ckv_ref[...] = ckv
    ckvb_ref[...] = ckv.astype(BF16)
    k0 = MLA_Q_RANK + MLA_KV_RANK
    kr = p_ref[:, k0:k0 + LANES]

    @pl.when(i < N_CTX // ROW_TILE)
    def _():
        kr_ref[...] = kr

    @pl.when(i >= N_CTX // ROW_TILE)
    def _():
        kr_ref[...] = _rope(kr, cos_ref[...], sa_ref[...], sb_ref[...], MLA_ROPE // 4)


def _lat_tile(i):
    return (jnp.maximum(i - N_CTX // ROW_TILE, 0)) % (DEC_SEQ // ROW_TILE)


def _mla_prep(proj_a, gq, gkv, l, tab):
    tr = ROW_TILE
    tspec = pl.BlockSpec((tr, LANES), lambda i: (_lat_tile(i), 0))
    return pl.pallas_call(
        _mla_prep_kernel,
        grid=(N_TOK // tr,),
        in_specs=[pl.BlockSpec((tr, 1024), lambda i: (i, 2)),
                  pl.BlockSpec((None, 1, MLA_Q_RANK), lambda i: (l, 0, 0)),
                  pl.BlockSpec((None, 1, MLA_KV_RANK), lambda i: (l, 0, 0)),
                  tspec, tspec, tspec],
        out_specs=[pl.BlockSpec((tr, MLA_Q_RANK), lambda i: (i, 0)),
                   pl.BlockSpec((tr, MLA_KV_RANK), lambda i: (i, 0)),
                   pl.BlockSpec((tr, MLA_KV_RANK), lambda i: (i, 0)),
                   pl.BlockSpec((tr, LANES), lambda i: (i, 0))],
        out_shape=[jax.ShapeDtypeStruct((N_TOK, MLA_Q_RANK), BF16),
                   jax.ShapeDtypeStruct((N_TOK, MLA_KV_RANK), F32),
                   jax.ShapeDtypeStruct((N_TOK, MLA_KV_RANK), BF16),
                   jax.ShapeDtypeStruct((N_TOK, LANES), F32)],
        compiler_params=_params("arbitrary"),
        name="mla_prep",
    )(proj_a, gq, gkv, *tab)


def _attn_kernel(*refs, lat):
    if lat:
        q_ref, kvn_ref, krn_ref, kvc_ref, krc_ref, cos_ref, sa_ref, sb_ref, o_ref = refs
    else:
        q_ref, kvn_ref, krn_ref, o_ref = refs
    scale = (MLA_NOPE + MLA_ROPE) ** -0.5
    krn = krn_ref[:, 0:MLA_ROPE].astype(BF16)
    if lat:
        krc = krc_ref[...].astype(BF16)
        kvc = kvc_ref[...].astype(BF16)
    for h in range(MLA_HEADS):
        c0 = h * MLA_QH
        qn = q_ref[:, c0:c0 + MLA_NOPE].astype(BF16)
        qr = q_ref[:, c0 + MLA_NOPE:c0 + MLA_QH]
        if lat:
            qr = _rope(qr, cos_ref[...], sa_ref[...], sb_ref[...], MLA_ROPE // 4)
        qr = qr[:, 0:MLA_ROPE].astype(BF16)
        kn = kvn_ref[:, c0:c0 + MLA_NOPE]
        v = kvn_ref[:, c0 + MLA_NOPE:c0 + MLA_QH]
        s2 = _dot_nt(qn, kn) + _dot_nt(qr, krn)
        m = jnp.max(s2, axis=-1, keepdims=True)
        if lat:
            s1 = _dot_nt(qn, kvc[:, c0:c0 + MLA_NOPE]) + _dot_nt(qr, krc)
            m = jnp.maximum(m, jnp.max(s1, axis=-1, keepdims=True))
            p1 = jnp.exp((s1 - m) * scale)
        p2 = jnp.exp((s2 - m) * scale)
        den = jnp.sum(p2, axis=-1, keepdims=True)
        o = _dot(p2.astype(BF16), v)
        if lat:
            den = den + jnp.sum(p1, axis=-1, keepdims=True)
            o = o + _dot(p1.astype(BF16), kvc[:, c0 + MLA_NOPE:c0 + MLA_QH])
        o_ref[:, h * MLA_V:(h + 1) * MLA_V] = (o / den).astype(BF16)


def _attn_ctx(q, kv, kr):
    t = SEQ
    return pl.pallas_call(
        functools.partial(_attn_kernel, lat=False),
        grid=(BATCH,),
        in_specs=[pl.BlockSpec((t, MLA_HEADS * MLA_QH), lambda b: (b, 0)),
                  pl.BlockSpec((t, MLA_HEADS * MLA_QH), lambda b: (b, 0)),
                  pl.BlockSpec((t, LANES), lambda b: (b, 0))],
        out_specs=pl.BlockSpec((t, MLA_HEADS * MLA_V), lambda b: (b, 0)),
        out_shape=jax.ShapeDtypeStruct((N_CTX, MLA_HEADS * MLA_V), BF16),
        compiler_params=_params("arbitrary"),
        name="attn_ctx",
    )(q, kv, kr)


def _attn_lat(q, kv, kr, kv_cache, cache_krope, l, tab):
    tq = ROW_TILE
    nq = DEC_SEQ // tq
    blk0 = N_CTX // DEC_SEQ
    qblk0 = N_CTX // tq
    tspec = pl.BlockSpec((tq, LANES), lambda b, i: (i, 0))
    return pl.pallas_call(
        functools.partial(_attn_kernel, lat=True),
        grid=(DEC_BATCH, nq),
        in_specs=[pl.BlockSpec((tq, MLA_HEADS * MLA_QH), lambda b, i: (qblk0 + b * nq + i, 0)),
                  pl.BlockSpec((DEC_SEQ, MLA_HEADS * MLA_QH), lambda b, i: (blk0 + b, 0)),
                  pl.BlockSpec((DEC_SEQ, LANES), lambda b, i: (blk0 + b, 0)),
                  pl.BlockSpec((PAST_LEN, MLA_HEADS * MLA_QH), lambda b, i: (b, 0)),
                  pl.BlockSpec((None, None, PAST_LEN, MLA_ROPE), lambda b, i: (b, l, 0, 0)),
                  tspec, tspec, tspec],
        out_specs=pl.BlockSpec((tq, MLA_HEADS * MLA_V), lambda b, i: (b * nq + i, 0)),
        out_shape=jax.ShapeDtypeStruct((N_LAT, MLA_HEADS * MLA_V), BF16),
        compiler_params=_params("arbitrary", "arbitrary"),
        name="attn_lat",
    )(q, kv, kr, kv_cache, cache_krope, *tab)


def _ret_prep_kernel(p_ref, cos_ref, sa_ref, sb_ref, o_ref):
    i = pl.program_id(0)
    w = RET_HEADS * RET_DK
    q = p_ref[:, 0:w]
    k = p_ref[:, w:2 * w] * (RET_DK ** -0.5)

    @pl.when(i < N_CTX // ROW_TILE)
    def _():
        o_ref[:, 0:w] = q.astype(BF16)
        o_ref[:, w:2 * w] = k.astype(BF16)

    @pl.when(i >= N_CTX // ROW_TILE)
    def _():
        cos, sa, sb = cos_ref[...], sa_ref[...], sb_ref[...]
        o_ref[:, 0:w] = _rope(q, cos, sa, sb, RET_DK // 4).astype(BF16)
        o_ref[:, w:2 * w] = _rope(k, cos, sa, sb, RET_DK // 4).astype(BF16)


def _ret_prep(proj_qk, tab):
    tr = ROW_TILE
    w = RET_HEADS * RET_DK
    tspec = pl.BlockSpec((tr, w), lambda i: (_lat_tile(i), 0))
    return pl.pallas_call(
        _ret_prep_kernel,
        grid=(N_TOK // tr,),
        in_specs=[pl.BlockSpec((tr, 2 * w), lambda i: (i, 0)), tspec, tspec, tspec],
        out_specs=pl.BlockSpec((tr, 2 * w), lambda i: (i, 0)),
        out_shape=jax.ShapeDtypeStruct((N_TOK, 2 * w), BF16),
        compiler_params=_params("arbitrary"),
        name="ret_prep",
    )(proj_qk, *tab)


def _log_sigmoid(x):
    return -_softplus(-x)


def _ret_kernel(*refs, lat, T):
    if lat:
        q_ref, k_ref, v_ref, rg_ref, dec_ref, s0_ref, o_ref = refs
    else:
        q_ref, k_ref, v_ref, rg_ref, dec_ref, o_ref, st_ref = refs
    tq = q_ref.shape[0]
    t0 = pl.program_id(1) * tq if lat else 0
    lg = _log_sigmoid(dec_ref[...])
    rows = (t0 + lax.broadcasted_iota(jnp.int32, (tq, T), 0)).astype(F32)
    cols = lax.broadcasted_iota(jnp.int32, (tq, T), 1).astype(F32)
    diff = rows - cols
    fwd = diff >= 0.0
    bwd = diff <= 0.0
    dpos = jnp.maximum(diff, 0.0)
    dneg = jnp.maximum(-diff, 0.0)
    tcol = (t0 + lax.broadcasted_iota(jnp.int32, (tq, 1), 0)).astype(F32)
    for h in range(RET_HEADS):
        cs = slice(h * RET_DK, (h + 1) * RET_DK)
        lgf = lg[0:1, h:h + 1]
        lgb = lg[1:2, h:h + 1]
        q = q_ref[:, cs]
        k = k_ref[:, cs]
        v = v_ref[:, cs]
        decay = (jnp.where(fwd, jnp.exp(lgf * dpos), 0.0) + jnp.where(bwd, jnp.exp(lgb * dneg), 0.0))
        sc = _dot_nt(q, k) * decay
        o = _dot(sc.astype(BF16), v)
        if lat:
            qf = q.astype(F32)
            o = o + _dot((qf * jnp.exp(lgf * (tcol + 1.0))).astype(BF16), s0_ref[0, h].astype(BF16))
            o = o + _dot((qf * jnp.exp(lgb * (T - tcol))).astype(BF16), s0_ref[1, h].astype(BF16))
        else:
            kf = k.astype(F32)
            st_ref[0, h] = _dot_tn((kf * jnp.exp(lgf * (T - 1.0 - tcol))).astype(BF16), v)
            st_ref[1, h] = _dot_tn((kf * jnp.exp(lgb * tcol)).astype(BF16), v)
        mu = jnp.mean(o, axis=-1, keepdims=True)
        oc = o - mu
        var = jnp.mean(oc * oc, axis=-1, keepdims=True)
        rg = rg_ref[:, cs]
        o_ref[:, cs] = (oc * lax.rsqrt(var + EPS) * (rg * _sigmoid(rg))).astype(BF16)


def _ret_ctx(qk, v, rg, decay, l):
    t = SEQ
    w = RET_HEADS * RET_DK
    return pl.pallas_call(
        functools.partial(_ret_kernel, lat=False, T=t),
        grid=(BATCH,),
        in_specs=[pl.BlockSpec((t, w), lambda b: (b, 0)),
                  pl.BlockSpec((t, w), lambda b: (b, 1)),
                  pl.BlockSpec((t, w), lambda b: (b, 0)),
                  pl.BlockSpec((t, w), lambda b: (b, 0)),
                  pl.BlockSpec((None, 2, RET_HEADS), lambda b: (l, 0, 0))],
        out_specs=[pl.BlockSpec((t, w), lambda b: (b, 0)),
                   pl.BlockSpec((None, 2, RET_HEADS, RET_DK, RET_DV), lambda b: (b, 0, 0, 0, 0))],
        out_shape=[jax.ShapeDtypeStruct((N_CTX, w), BF16),
                   jax.ShapeDtypeStruct((BATCH, 2, RET_HEADS, RET_DK, RET_DV), F32)],
        compiler_params=_params("arbitrary"),
        name="ret_ctx",
    )(qk, qk, v, rg, decay)


def _ret_lat(qk, v, rg, decay, state_ret, l):
    tq = ROW_TILE
    t = DEC_SEQ
    nq = t // tq
    w = RET_HEADS * RET_DK
    blk0 = N_CTX // t
    qblk0 = N_CTX // tq
    return pl.pallas_call(
        functools.partial(_ret_kernel, lat=True, T=t),
        grid=(DEC_BATCH, nq),
        in_specs=[pl.BlockSpec((tq, w), lambda b, i: (qblk0 + b * nq + i, 0)),
                  pl.BlockSpec((t, w), lambda b, i: (blk0 + b, 1)),
                  pl.BlockSpec((t, w), lambda b, i: (blk0 + b, 0)),
                  pl.BlockSpec((tq, w), lambda b, i: (qblk0 + b * nq + i, 0)),
                  pl.BlockSpec((None, 2, RET_HEADS), lambda b, i: (l, 0, 0)),
                  pl.BlockSpec((None, None, 2, RET_HEADS, RET_DK, RET_DV), lambda b, i: (b, l, 0, 0, 0, 0))],
        out_specs=pl.BlockSpec((tq, w), lambda b, i: (b * nq + i, 0)),
        out_shape=jax.ShapeDtypeStruct((N_LAT, w), BF16),
        compiler_params=_params("arbitrary", "arbitrary"),
        name="ret_lat",
    )(qk, qk, v, rg, decay, state_ret)


def _merge_kernel(g0_ref, g1_ref, g2_ref, u0_ref, u1_ref, u2_ref, o_ref):
    m = (_sigmoid(g0_ref[...]) * u0_ref[...] + _sigmoid(g1_ref[...]) * u1_ref[...]
         + _sigmoid(g2_ref[...]) * u2_ref[...])
    o_ref[...] = m.astype(BF16)


def _merge(gates, u_lru, u_mla, u_ret):
    tr = ROW_TILE
    row = pl.BlockSpec((tr, D_MODEL), lambda i: (i, 0))
    return pl.pallas_call(
        _merge_kernel,
        grid=(N_TOK // tr,),
        in_specs=[pl.BlockSpec((tr, D_MODEL), lambda i: (i, 0)),
                  pl.BlockSpec((tr, D_MODEL), lambda i: (i, 1)),
                  pl.BlockSpec((tr, D_MODEL), lambda i: (i, 2)),
                  row, row, row],
        out_specs=row,
        out_shape=jax.ShapeDtypeStruct((N_TOK, D_MODEL), BF16),
        compiler_params=_params("arbitrary"),
        name="merge",
    )(gates, gates, gates, u_lru, u_mla, u_ret)


def _rope_tables(dim, reps):
    half = dim // 2
    quarter = half // 2
    t = jnp.arange(DEC_SEQ)
    row = (t // GRID_W).astype(F32)
    col = (t % GRID_W).astype(F32)
    inv = ROPE_BASE ** (-jnp.arange(0, half, 2, dtype=F32) / half)
    zeros = jnp.zeros((DEC_SEQ, quarter), F32)
    cos_parts, sa_parts, sb_parts = [], [], []
    for pos in (row, col):
        ang = pos[:, None] * inv[None, :]
        c, s = jnp.cos(ang), jnp.sin(ang)
        cos_parts += [c, c]
        sa_parts += [zeros, s]
        sb_parts += [-s, zeros]
    padw = LANES - dim

    def build(parts):
        tab = jnp.concatenate(parts + ([jnp.zeros((DEC_SEQ, padw), F32)] if padw else []), axis=-1)
        return jnp.tile(tab, (1, reps))

    return build(cos_parts), build(sa_parts), build(sb_parts)


def kernel(x_prompt, x_sample, c, cache_mla_ckv, cache_mla_krope, state_lru, state_ret, c_ctx, w_mod, b_mod,
           g_norm, w_in, lru_conv_w, lru_conv_b, lru_wa, lru_ba, lru_wx, lru_bx, lru_lam, mla_gq, mla_gkv,
           mla_wuq, mla_wukv, ret_decay, w_br_lru, w_br_mla, w_br_ret, w_out, w_ff1, w_ff2):
    x = jnp.concatenate([x_prompt.reshape(N_CTX, D_MODEL), x_sample.reshape(N_LAT, D_MODEL)], axis=0)
    cond8 = jnp.concatenate([c_ctx[None, :], c, jnp.zeros((8 - 1 - DEC_BATCH, D_MODEL), F32)], axis=0)
    mod = _ada(cond8, w_mod, b_mod).reshape(DEPTH * 8, 1, 6 * D_MODEL)
    g_all = g_norm.reshape(DEPTH * 4, 1, D_MODEL)

    w_in_tail = w_in[:, :, COL_TAIL:]
    wuq = jnp.pad(mla_wuq.reshape(DEPTH, MLA_Q_RANK, MLA_HEADS, MLA_NOPE + MLA_ROPE),
                  ((0, 0), (0, 0), (0, 0), (0, MLA_QH - MLA_NOPE - MLA_ROPE)))
    wuq = wuq.reshape(DEPTH, MLA_Q_RANK, MLA_HEADS * MLA_QH)
    lw = {"conv_w": lru_conv_w, "conv_b": lru_conv_b.reshape(DEPTH, 2, 1, D_RNN), "wa": lru_wa,
          "ba": lru_ba.reshape(DEPTH, 2, 1, D_RNN), "wx": lru_wx, "bx": lru_bx.reshape(DEPTH, 2, 1, D_RNN),
          "lam": lru_lam.reshape(DEPTH, 2, 1, D_RNN)}
    gq = mla_gq.reshape(DEPTH, 1, MLA_Q_RANK)
    gkv = mla_gkv.reshape(DEPTH, 1, MLA_KV_RANK)
    h0_lat = state_lru.reshape(DEC_BATCH, DEPTH, 2, 1, D_RNN)
    cache_ckv_rows = cache_mla_ckv.reshape(DEC_BATCH * DEPTH * PAST_LEN, MLA_KV_RANK)
    tab_mla = _rope_tables(MLA_ROPE, 1)
    tab_ret = _rope_tables(RET_DK, RET_HEADS)
    wq = RET_HEADS * RET_DK

    ckv_out, kr_out, lru_out, ret_out = [], [], [], []
    h, = _resnorm(x, mod, g_all, nxt=(0, 0, 0))
    for l in range(DEPTH):
        proj_a = _mm(h, w_in, l, col0=0, ncols=3072)
        proj_qk = _mm(h, w_in_tail, l, col0=0, ncols=2 * wq)
        proj_v = _mm(h, w_in_tail, l, col0=2 * wq, ncols=wq, out_dtype=BF16)
        proj_rg = _mm(h, w_in_tail, l, col0=3 * wq, ncols=wq)
        gates = _mm(h, w_in_tail, l, col0=4 * wq, ncols=3 * D_MODEL)

        y_ctx, st_lru = _lru(proj_a, lw, l, T=SEQ, nb=BATCH, blk0=0)
        y_lat, _ = _lru(proj_a, lw, l, T=DEC_SEQ, nb=DEC_BATCH, blk0=N_CTX // DEC_SEQ, h0=h0_lat)
        u_lru = _mm(jnp.concatenate([y_ctx, y_lat], axis=0), w_br_lru, l)

        cqn, ckv, ckvb, kr = _mla_prep(proj_a, gq, gkv, l, tab_mla)
        q = _mm(cqn, wuq, l)
        kv = _mm(ckvb, mla_wukv, l, out_dtype=BF16)
        kv_cache = _mm(cache_ckv_rows, mla_wukv, l, out_dtype=BF16, tm=PAST_LEN, m=DEC_BATCH * PAST_LEN,
                       row_map=lambda i, l=l: i * DEPTH + l)
        o_ctx = _attn_ctx(q, kv, kr)
        o_lat = _attn_lat(q, kv, kr, kv_cache, cache_mla_krope, l, tab_mla)
        u_mla = _mm(jnp.concatenate([o_ctx, o_lat], axis=0), w_br_mla, l)

        qk = _ret_prep(proj_qk, tab_ret)
        r_ctx, st_ret = _ret_ctx(qk, proj_v, proj_rg, ret_decay, l)
        r_lat = _ret_lat(qk, proj_v, proj_rg, ret_decay, state_ret, l)
        u_ret = _mm(jnp.concatenate([r_ctx, r_lat], axis=0), w_br_ret, l)

        merged = _merge(gates, u_lru, u_mla, u_ret)
        u = _mm(merged, w_out, l)
        x, h2 = _resnorm(x, mod, g_all, res=(u, l * 4 + 1, l, 2), nxt=(l * 4 + 2, l, 3))
        ff = _mm(h2, w_ff1, l, out_dtype=BF16, act="relu2")
        y = _mm(ff, w_ff2, l)
        if l + 1 < DEPTH:
            x, h = _resnorm(x, mod, g_all, res=(y, l * 4 + 3, l, 5), nxt=((l + 1) * 4, l + 1, 0))
        else:
            x, = _resnorm(x, mod, g_all, res=(y, l * 4 + 3, l, 5))

        ckv_out.append(ckv[:N_CTX].reshape(BATCH, SEQ, MLA_KV_RANK))
        kr_out.append(kr[:N_CTX, :MLA_ROPE].reshape(BATCH, SEQ, MLA_ROPE))
        lru_out.append(st_lru.reshape(BATCH, 2, D_RNN))
        ret_out.append(st_ret)

    y_prompt = x[:N_CTX].reshape(BATCH, SEQ, D_MODEL)
    y_sample = x[N_CTX:].reshape(DEC_BATCH, DEC_SEQ, D_MODEL)
    return (y_prompt, y_sample, jnp.stack(ckv_out, axis=1), jnp.stack(kr_out, axis=1),
            jnp.stack(lru_out, axis=1), jnp.stack(ret_out, axis=1))
```

```python
import functools

import jax
import jax.numpy as jnp
from jax import lax
from jax.experimental import pallas as pl
from jax.experimental.pallas import tpu as pltpu

F32 = jnp.float32
BF16 = jnp.bfloat16

D_MODEL = 2048
BATCH = 16
SEQ = 256
DEPTH = 4
DEC_BATCH = 4
DEC_SEQ = 1024
PAST_LEN = 256
GRID_W = 64
EPS = 1e-6
ROPE_BASE = 10000.0
D_RNN = D_MODEL // 2
LRU_BLOCKS = 8
LRU_BS = D_RNN // LRU_BLOCKS
CONV_W = 4
LRU_C = 8.0
MLA_HEADS = 8
MLA_NOPE = 128
MLA_ROPE = 64
MLA_V = 128
MLA_Q_RANK = D_MODEL // 4
MLA_KV_RANK = D_MODEL // 8
RET_HEADS = 8
RET_DK = 128
RET_DV = 128
D_FF = 4 * D_MODEL

N_CTX = BATCH * SEQ
N_LAT = DEC_BATCH * DEC_SEQ
N_TOK = N_CTX + N_LAT
COL_TAIL = 2 * D_RNN + MLA_Q_RANK + MLA_KV_RANK + MLA_ROPE
MLA_QH = 256

LANES = 128
SUBLANES = 8
VMEM_LIMIT = 56 * 1024 * 1024
ROW_TILE = 256


def _params(*sem):
    return pltpu.CompilerParams(dimension_semantics=sem, vmem_limit_bytes=VMEM_LIMIT)


def _sigmoid(x):
    return 1.0 / (1.0 + jnp.exp(-x))


def _softplus(x):
    return jnp.maximum(x, 0.0) + jnp.log(1.0 + jnp.exp(-jnp.abs(x)))


def _rms(x, g):
    return x * lax.rsqrt(jnp.mean(x * x, axis=-1, keepdims=True) + EPS) * g


def _dot(a, b):
    return jnp.dot(a, b, preferred_element_type=F32)


def _dot_nt(a, b):
    return lax.dot_general(a, b, (((1,), (1,)), ((), ())), preferred_element_type=F32)


def _dot_tn(a, b):
    return lax.dot_general(a, b, (((0,), (0,)), ((), ())), preferred_element_type=F32)


def _rope(x, cos, sa, sb, quarter):
    w = x.shape[-1]
    return x * cos + pltpu.roll(x, quarter, 1) * sa + pltpu.roll(x, w - quarter, 1) * sb


def _mod_row(row0):
    return jnp.where(row0 < N_CTX, 0, 1 + (row0 - N_CTX) // DEC_SEQ)


def _ada_kernel(c_ref, w_ref, b_ref, o_ref):
    c = c_ref[...]
    s = c * _sigmoid(c)
    o_ref[...] = _dot(s.astype(BF16), w_ref[...].astype(BF16)) + b_ref[...]


def _ada(cond8, w_mod, b_mod):
    tn = 1024
    n = w_mod.shape[-1]
    return pl.pallas_call(
        _ada_kernel,
        grid=(DEPTH, n // tn),
        in_specs=[pl.BlockSpec((8, D_MODEL), lambda l, j: (0, 0)),
                  pl.BlockSpec((None, D_MODEL, tn), lambda l, j: (l, 0, j)),
                  pl.BlockSpec((None, 1, tn), lambda l, j: (l, 0, j))],
        out_specs=pl.BlockSpec((None, 8, tn), lambda l, j: (l, 0, j)),
        out_shape=jax.ShapeDtypeStruct((DEPTH, 8, n), F32),
        compiler_params=_params("arbitrary", "arbitrary"),
        name="ada",
    )(cond8, w_mod, b_mod.reshape(DEPTH, 1, n))


def _resnorm_kernel(*refs, has_res, has_next):
    refs = list(refs)
    x_ref = refs.pop(0)
    x = x_ref[...]
    if has_res:
        u_ref, gpost_ref, gate_ref = refs[:3]
        refs = refs[3:]
        x = x + gate_ref[...] * _rms(u_ref[...], gpost_ref[...])
    if has_next:
        gpre_ref, shift_ref, scale_ref = refs[:3]
        refs = refs[3:]
    if has_res:
        xo_ref = refs.pop(0)
        xo_ref[...] = x
    if has_next:
        ho_ref = refs.pop(0)
        h = _rms(x, gpre_ref[...]) * (1.0 + scale_ref[...]) + shift_ref[...]
        ho_ref[...] = h.astype(BF16)


def _resnorm(x, mod, g_norm, *, res=None, nxt=None):
    tr = ROW_TILE
    row = pl.BlockSpec((tr, D_MODEL), lambda i: (i, 0))

    def gspec(k):
        return pl.BlockSpec((None, 1, D_MODEL), lambda i: (k, 0, 0))

    def mspec(l, chunk):
        return pl.BlockSpec((None, 1, D_MODEL), lambda i: (l * 8 + _mod_row(i * tr), 0, chunk))

    args, specs, outs, ospecs = [x], [row], [], []
    if res is not None:
        u, gk, l, chunk = res
        args += [u, g_norm, mod]
        specs += [row, gspec(gk), mspec(l, chunk)]
        outs.append(jax.ShapeDtypeStruct((N_TOK, D_MODEL), F32))
        ospecs.append(row)
    if nxt is not None:
        gk, l, chunk = nxt
        args += [g_norm, mod, mod]
        specs += [gspec(gk), mspec(l, chunk), mspec(l, chunk + 1)]
        outs.append(jax.ShapeDtypeStruct((N_TOK, D_MODEL), BF16))
        ospecs.append(row)
    res_out = pl.pallas_call(
        functools.partial(_resnorm_kernel, has_res=res is not None, has_next=nxt is not None),
        grid=(N_TOK // tr,),
        in_specs=specs, out_specs=ospecs, out_shape=outs,
        compiler_params=_params("arbitrary"),
        name="resnorm",
    )(*args)
    return res_out


def _mm_kernel(x_ref, w_ref, o_ref, *, nk, act):
    if nk == 1:
        part = _dot(x_ref[...].astype(BF16), w_ref[...].astype(BF16))
        if act == "relu2":
            part = jnp.square(jnp.maximum(part, 0.0))
        elif act == "sigmoid":
            part = _sigmoid(part)
        o_ref[...] = part.astype(o_ref.dtype)
        return

    @pl.when(pl.program_id(2) == 0)
    def _():
        o_ref[...] = jnp.zeros(o_ref.shape, o_ref.dtype)

    o_ref[...] += _dot(x_ref[...].astype(BF16), w_ref[...].astype(BF16))


def _mm(x, w, l, *, col0=0, ncols=None, out_dtype=F32, act=None, tm=2048, tn=512, tk=None,
        m=None, row_map=None):
    kdim = w.shape[1]
    ncols = w.shape[2] - col0 if ncols is None else ncols
    m = x.shape[0] if m is None else m
    tk = kdim if tk is None else tk
    tm, tn = min(tm, m), min(tn, ncols)
    assert m % tm == 0 and ncols % tn == 0 and kdim % tk == 0 and col0 % tn == 0
    nk = kdim // tk
    assert nk == 1 or (act is None and out_dtype == F32)
    jb = col0 // tn
    row_map = (lambda i: i) if row_map is None else row_map
    return pl.pallas_call(
        functools.partial(_mm_kernel, nk=nk, act=act),
        grid=(m // tm, ncols // tn, nk),
        in_specs=[pl.BlockSpec((tm, tk), lambda i, j, k: (row_map(i), k)),
                  pl.BlockSpec((None, tk, tn), lambda i, j, k: (l, k, jb + j))],
        out_specs=pl.BlockSpec((tm, tn), lambda i, j, k: (i, j)),
        out_shape=jax.ShapeDtypeStruct((m, ncols), out_dtype),
        compiler_params=_params("arbitrary", "arbitrary", "arbitrary"),
        name="mm",
    )(x, w)


def _bm_kernel(a0_ref, a1_ref, a2_ref, w0_ref, w1_ref, w2_ref, g0_ref, g1_ref, g2_ref, o_ref):
    acc = g0_ref[...].astype(F32) * _dot(a0_ref[...], w0_ref[...].astype(BF16))
    acc = acc + g1_ref[...].astype(F32) * _dot(a1_ref[...], w1_ref[...].astype(BF16))
    acc = acc + g2_ref[...].astype(F32) * _dot(a2_ref[...], w2_ref[...].astype(BF16))
    o_ref[...] = acc.astype(BF16)


def _branch_merge(acts, ws, gates, l, *, tm=1024, tn=512):
    nj = D_MODEL // tn
    kdim = acts[0].shape[1]
    aspec = pl.BlockSpec((tm, kdim), lambda i, j: (i, 0))
    wspec = pl.BlockSpec((None, kdim, tn), lambda i, j: (l, 0, j))
    gspecs = [pl.BlockSpec((tm, tn), lambda i, j, b=b: (i, b * nj + j)) for b in range(3)]
    return pl.pallas_call(
        _bm_kernel,
        grid=(N_TOK // tm, nj),
        in_specs=[aspec] * 3 + [wspec] * 3 + gspecs,
        out_specs=pl.BlockSpec((tm, tn), lambda i, j: (i, j)),
        out_shape=jax.ShapeDtypeStruct((N_TOK, D_MODEL), BF16),
        compiler_params=_params("arbitrary", "arbitrary"),
        name="branch_merge",
    )(*acts, *ws, gates, gates, gates)


LRU_CT = 512


def _lru_kernel(*refs, T, has_h0):
    if has_h0:
        (xr_ref, xg_ref, cw_ref, cb_ref, wa_ref, ba_ref, wx_ref, bx_ref, lam_ref, h0_ref, _,
         y_ref, st_ref, xpad_ref, a_ref, b_ref, h_ref) = refs
    else:
        (xr_ref, xg_ref, cw_ref, cb_ref, wa_ref, ba_ref, wx_ref, bx_ref, lam_ref,
         y_ref, st_ref, xpad_ref, a_ref, b_ref, h_ref) = refs
    ct = xr_ref.shape[1]
    groups = T // SUBLANES
    pad = SUBLANES
    xpad_ref[0:pad, :] = jnp.zeros((pad, ct), F32)
    xpad_ref[pad + T:2 * pad + T, :] = jnp.zeros((pad, ct), F32)
    xpad_ref[pad:pad + T, :] = xr_ref[...]
    row8 = lax.broadcasted_iota(jnp.int32, (SUBLANES, ct), 0)

    for d in range(2):
        for n in range(ct // LRU_BS):
            cs = slice(n * LRU_BS, (n + 1) * LRU_BS)
            xc = jnp.zeros((T, LRU_BS), F32) + cb_ref[d, :, cs]
            for j in range(CONV_W):
                off = (j - (CONV_W - 1)) if d == 0 else ((CONV_W - 1) - j)
                xc = xc + cw_ref[d, j:j + 1, cs] * xpad_ref[pad + off:pad + off + T, cs]
            xcb = xc.astype(BF16)
            r = _sigmoid(_dot(xcb, wa_ref[d, n].astype(BF16)) + ba_ref[d, :, cs])
            i = _sigmoid(_dot(xcb, wx_ref[d, n].astype(BF16)) + bx_ref[d, :, cs])
            log_a = (-LRU_C) * r * _softplus(-lam_ref[d, :, cs])
            a = jnp.exp(log_a)
            a_ref[:, cs] = a
            b_ref[:, cs] = jnp.sqrt(1.0 - a * a) * (i * xc)

        h0 = h0_ref[d] if has_h0 else jnp.zeros((1, ct), F32)

        def body(it, carry, d=d):
            g = it if d == 0 else groups - 1 - it
            r0 = pl.multiple_of(g * SUBLANES, SUBLANES)
            av = a_ref[pl.ds(r0, SUBLANES), :]
            bv = b_ref[pl.ds(r0, SUBLANES), :]
            for s in (1, 2, 4):
                if d == 0:
                    ok = row8 >= s
                    shift = s
                else:
                    ok = row8 < SUBLANES - s
                    shift = SUBLANES - s
                a_sh = jnp.where(ok, pltpu.roll(av, shift, 0), 1.0)
                b_sh = jnp.where(ok, pltpu.roll(bv, shift, 0), 0.0)
                bv = av * b_sh + bv
                av = av * a_sh
            h = av * carry + bv
            if d == 0:
                h_ref[pl.ds(r0, SUBLANES), :] = h
                return h[SUBLANES - 1:SUBLANES, :]
            h_ref[pl.ds(r0, SUBLANES), :] = h_ref[pl.ds(r0, SUBLANES), :] + h
            return h[0:1, :]

        last = lax.fori_loop(0, groups, body, h0)
        st_ref[d] = last

    xg = xg_ref[...]
    gelu = 0.5 * xg * (1.0 + jnp.tanh(0.7978845608028654 * (xg + 0.044715 * (xg * xg * xg))))
    y_ref[...] = (h_ref[...] * gelu).astype(BF16)


def _lru(proj_a, lw, l, *, T, nb, blk0, h0=None, prev=None):
    ct = LRU_CT
    ncb = D_RNN // ct
    bpc = ct // LRU_BS
    in_specs = [
        pl.BlockSpec((T, ct), lambda b, c: (blk0 + b, c)),
        pl.BlockSpec((T, ct), lambda b, c: (blk0 + b, ncb + c)),
        pl.BlockSpec((None, 2, CONV_W, ct), lambda b, c: (l, 0, 0, c)),
        pl.BlockSpec((None, 2, 1, ct), lambda b, c: (l, 0, 0, c)),
        pl.BlockSpec((None, 2, bpc, LRU_BS, LRU_BS), lambda b, c: (l, 0, c, 0, 0)),
        pl.BlockSpec((None, 2, 1, ct), lambda b, c: (l, 0, 0, c)),
        pl.BlockSpec((None, 2, bpc, LRU_BS, LRU_BS), lambda b, c: (l, 0, c, 0, 0)),
        pl.BlockSpec((None, 2, 1, ct), lambda b, c: (l, 0, 0, c)),
        pl.BlockSpec((None, 2, 1, ct), lambda b, c: (l, 0, 0, c)),
    ]
    args = [proj_a, proj_a, lw["conv_w"], lw["conv_b"], lw["wa"], lw["ba"], lw["wx"], lw["bx"], lw["lam"]]
    if h0 is not None:
        in_specs.append(pl.BlockSpec((None, None, 2, 1, ct), lambda b, c: (b, l, 0, 0, c)))
        args.append(h0)
        in_specs.append(pl.BlockSpec(memory_space=pl.ANY))
        args.append(prev)
    return pl.pallas_call(
        functools.partial(_lru_kernel, T=T, has_h0=h0 is not None),
        grid=(nb, ncb),
        in_specs=in_specs,
        out_specs=[pl.BlockSpec((T, ct), lambda b, c: (blk0 + b, c)),
                   pl.BlockSpec((None, 2, 1, ct), lambda b, c: (b, 0, 0, c))],
        out_shape=[jax.ShapeDtypeStruct((N_TOK, D_RNN), BF16),
                   jax.ShapeDtypeStruct((nb, 2, 1, D_RNN), F32)],
        input_output_aliases={} if prev is None else {len(args) - 1: 0},
        scratch_shapes=[pltpu.VMEM((T + 2 * SUBLANES, ct), F32), pltpu.VMEM((T, ct), F32),
                        pltpu.VMEM((T, ct), F32), pltpu.VMEM((T, ct), F32)],
        compiler_params=_params("arbitrary", "arbitrary"),
        name="lru",
    )(*args)


def _mla_prep_kernel(p_ref, gq_ref, gkv_ref, cos_ref, sa_ref, sb_ref, cq_ref, ckv_ref, ckvb_ref, kr_ref):
    i = pl.program_id(0)
    cq_ref[...] = _rms(p_ref[:, 0:MLA_Q_RANK], gq_ref[...]).astype(BF16)
    ckv = _rms(p_ref[:, MLA_Q_RANK:MLA_Q_RANK + MLA_KV_RANK], gkv_ref[...])
    ckv_ref[...] = ckv
    ckvb_ref[...] = ckv.astype(BF16)
    k0 = MLA_Q_RANK + MLA_KV_RANK
    kr = p_ref[:, k0:k0 + LANES]

    @pl.when(i < N_CTX // ROW_TILE)
    def _():
        kr_ref[...] = kr

    @pl.when(i >= N_CTX // ROW_TILE)
    def _():
        kr_ref[...] = _rope(kr, cos_ref[...], sa_ref[...], sb_ref[...], MLA_ROPE // 4)


def _lat_tile(i):
    return (jnp.maximum(i - N_CTX // ROW_TILE, 0)) % (DEC_SEQ // ROW_TILE)


def _mla_prep(proj_a, gq, gkv, l, tab):
    tr = ROW_TILE
    tspec = pl.BlockSpec((tr, LANES), lambda i: (_lat_tile(i), 0))
    return pl.pallas_call(
        _mla_prep_kernel,
        grid=(N_TOK // tr,),
        in_specs=[pl.BlockSpec((tr, 1024), lambda i: (i, 2)),
                  pl.BlockSpec((None, 1, MLA_Q_RANK), lambda i: (l, 0, 0)),
                  pl.BlockSpec((None, 1, MLA_KV_RANK), lambda i: (l, 0, 0)),
                  tspec, tspec, tspec],
        out_specs=[pl.BlockSpec((tr, MLA_Q_RANK), lambda i: (i, 0)),
                   pl.BlockSpec((tr, MLA_KV_RANK), lambda i: (i, 0)),
                   pl.BlockSpec((tr, MLA_KV_RANK), lambda i: (i, 0)),
                   pl.BlockSpec((tr, LANES), lambda i: (i, 0))],
        out_shape=[jax.ShapeDtypeStruct((N_TOK, MLA_Q_RANK), BF16),
                   jax.ShapeDtypeStruct((N_TOK, MLA_KV_RANK), F32),
                   jax.ShapeDtypeStruct((N_TOK, MLA_KV_RANK), BF16),
                   jax.ShapeDtypeStruct((N_TOK, LANES), F32)],
        compiler_params=_params("arbitrary"),
        name="mla_prep",
    )(proj_a, gq, gkv, *tab)


def _attn_kernel(*refs, lat):
    if lat:
        q_ref, kvn_ref, krn_ref, kvc_ref, krc_ref, cos_ref, sa_ref, sb_ref, _, o_ref = refs
    else:
        q_ref, kvn_ref, krn_ref, o_ref = refs
    scale = (MLA_NOPE + MLA_ROPE) ** -0.5
    krn = krn_ref[:, 0:MLA_ROPE].astype(BF16)
    if lat:
        krc = krc_ref[...].astype(BF16)
        kvc = kvc_ref[...].astype(BF16)
    for h in range(MLA_HEADS):
        c0 = h * MLA_QH
        qn = q_ref[:, c0:c0 + MLA_NOPE].astype(BF16)
        qr = q_ref[:, c0 + MLA_NOPE:c0 + MLA_QH]
        if lat:
            qr = _rope(qr, cos_ref[...], sa_ref[...], sb_ref[...], MLA_ROPE // 4)
        qr = qr[:, 0:MLA_ROPE].astype(BF16)
        kn = kvn_ref[:, c0:c0 + MLA_NOPE]
        v = kvn_ref[:, c0 + MLA_NOPE:c0 + MLA_QH]
        s2 = _dot_nt(qn, kn) + _dot_nt(qr, krn)
        m = jnp.max(s2, axis=-1, keepdims=True)
        if lat:
            s1 = _dot_nt(qn, kvc[:, c0:c0 + MLA_NOPE]) + _dot_nt(qr, krc)
            m = jnp.maximum(m, jnp.max(s1, axis=-1, keepdims=True))
            p1 = jnp.exp((s1 - m) * scale)
        p2 = jnp.exp((s2 - m) * scale)
        den = jnp.sum(p2, axis=-1, keepdims=True)
        o = _dot(p2.astype(BF16), v)
        if lat:
            den = den + jnp.sum(p1, axis=-1, keepdims=True)
            o = o + _dot(p1.astype(BF16), kvc[:, c0 + MLA_NOPE:c0 + MLA_QH])
        o_ref[:, h * MLA_V:(h + 1) * MLA_V] = (o / den).astype(BF16)


def _attn_ctx(q, kv, kr):
    t = SEQ
    return pl.pallas_call(
        functools.partial(_attn_kernel, lat=False),
        grid=(BATCH,),
        in_specs=[pl.BlockSpec((t, MLA_HEADS * MLA_QH), lambda b: (b, 0)),
                  pl.BlockSpec((t, MLA_HEADS * MLA_QH), lambda b: (b, 0)),
                  pl.BlockSpec((t, LANES), lambda b: (b, 0))],
        out_specs=pl.BlockSpec((t, MLA_HEADS * MLA_V), lambda b: (b, 0)),
        out_shape=jax.ShapeDtypeStruct((N_TOK, MLA_HEADS * MLA_V), BF16),
        compiler_params=_params("arbitrary"),
        name="attn_ctx",
    )(q, kv, kr)


def _attn_lat(q, kv, kr, kv_cache, cache_krope, l, tab, prev):
    tq = ROW_TILE
    nq = DEC_SEQ // tq
    blk0 = N_CTX // DEC_SEQ
    qblk0 = N_CTX // tq
    tspec = pl.BlockSpec((tq, LANES), lambda b, i: (i, 0))
    return pl.pallas_call(
        functools.partial(_attn_kernel, lat=True),
        grid=(DEC_BATCH, nq),
        in_specs=[pl.BlockSpec((tq, MLA_HEADS * MLA_QH), lambda b, i: (qblk0 + b * nq + i, 0)),
                  pl.BlockSpec((DEC_SEQ, MLA_HEADS * MLA_QH), lambda b, i: (blk0 + b, 0)),
                  pl.BlockSpec((DEC_SEQ, LANES), lambda b, i: (blk0 + b, 0)),
                  pl.BlockSpec((PAST_LEN, MLA_HEADS * MLA_QH), lambda b, i: (b, 0)),
                  pl.BlockSpec((None, None, PAST_LEN, MLA_ROPE), lambda b, i: (b, l, 0, 0)),
                  tspec, tspec, tspec, pl.BlockSpec(memory_space=pl.ANY)],
        out_specs=pl.BlockSpec((tq, MLA_HEADS * MLA_V), lambda b, i: (qblk0 + b * nq + i, 0)),
        out_shape=jax.ShapeDtypeStruct((N_TOK, MLA_HEADS * MLA_V), BF16),
        input_output_aliases={8: 0},
        compiler_params=_params("arbitrary", "arbitrary"),
        name="attn_lat",
    )(q, kv, kr, kv_cache, cache_krope, *tab, prev)


def _ret_prep_kernel(p_ref, cos_ref, sa_ref, sb_ref, o_ref):
    i = pl.program_id(0)
    w = RET_HEADS * RET_DK
    q = p_ref[:, 0:w]
    k = p_ref[:, w:2 * w] * (RET_DK ** -0.5)

    @pl.when(i < N_CTX // ROW_TILE)
    def _():
        o_ref[:, 0:w] = q.astype(BF16)
        o_ref[:, w:2 * w] = k.astype(BF16)

    @pl.when(i >= N_CTX // ROW_TILE)
    def _():
        cos, sa, sb = cos_ref[...], sa_ref[...], sb_ref[...]
        o_ref[:, 0:w] = _rope(q, cos, sa, sb, RET_DK // 4).astype(BF16)
        o_ref[:, w:2 * w] = _rope(k, cos, sa, sb, RET_DK // 4).astype(BF16)


def _ret_prep(proj_qk, tab):
    tr = ROW_TILE
    w = RET_HEADS * RET_DK
    tspec = pl.BlockSpec((tr, w), lambda i: (_lat_tile(i), 0))
    return pl.pallas_call(
        _ret_prep_kernel,
        grid=(N_TOK // tr,),
        in_specs=[pl.BlockSpec((tr, 2 * w), lambda i: (i, 0)), tspec, tspec, tspec],
        out_specs=pl.BlockSpec((tr, 2 * w), lambda i: (i, 0)),
        out_shape=jax.ShapeDtypeStruct((N_TOK, 2 * w), BF16),
        compiler_params=_params("arbitrary"),
        name="ret_prep",
    )(proj_qk, *tab)


def _log_sigmoid(x):
    return -_softplus(-x)


def _ret_kernel(*refs, lat, T):
    if lat:
        q_ref, k_ref, v_ref, rg_ref, dec_ref, s0_ref, _, o_ref, decay_ref = refs
    else:
        q_ref, k_ref, v_ref, rg_ref, dec_ref, o_ref, st_ref, decay_ref = refs
    tq = q_ref.shape[0]
    t0 = pl.program_id(0) * tq if lat else 0
    batch_id = pl.program_id(1) if lat else pl.program_id(0)
    lg = _log_sigmoid(dec_ref[...])
    tcol = (t0 + lax.broadcasted_iota(jnp.int32, (tq, 1), 0)).astype(F32)

    @pl.when(batch_id == 0)
    def _():
        rows = (t0 + lax.broadcasted_iota(jnp.int32, (tq, T), 0)).astype(F32)
        cols = lax.broadcasted_iota(jnp.int32, (tq, T), 1).astype(F32)
        diff = rows - cols
        fwd = diff >= 0.0
        bwd = diff <= 0.0
        dpos = jnp.maximum(diff, 0.0)
        dneg = jnp.maximum(-diff, 0.0)
        for h in range(RET_HEADS):
            decay_ref[h] = (jnp.where(fwd, jnp.exp(lg[0:1, h:h + 1] * dpos), 0.0)
                            + jnp.where(bwd, jnp.exp(lg[1:2, h:h + 1] * dneg), 0.0))

    for h in range(RET_HEADS):
        cs = slice(h * RET_DK, (h + 1) * RET_DK)
        lgf = lg[0:1, h:h + 1]
        lgb = lg[1:2, h:h + 1]
        q = q_ref[:, cs]
        k = k_ref[:, cs]
        v = v_ref[:, cs]
        sc = _dot_nt(q, k) * decay_ref[h]
        o = _dot(sc.astype(BF16), v)
        if lat:
            qf = q.astype(F32)
            o = o + _dot((qf * jnp.exp(lgf * (tcol + 1.0))).astype(BF16), s0_ref[0, h].astype(BF16))
            o = o + _dot((qf * jnp.exp(lgb * (T - tcol))).astype(BF16), s0_ref[1, h].astype(BF16))
        else:
            kf = k.astype(F32)
            st_ref[0, h] = _dot_tn((kf * jnp.exp(lgf * (T - 1.0 - tcol))).astype(BF16), v)
            st_ref[1, h] = _dot_tn((kf * jnp.exp(lgb * tcol)).astype(BF16), v)
        mu = jnp.mean(o, axis=-1, keepdims=True)
        oc = o - mu
        var = jnp.mean(oc * oc, axis=-1, keepdims=True)
        rg = rg_ref[:, cs]
        o_ref[:, cs] = (oc * lax.rsqrt(var + EPS) * (rg * _sigmoid(rg))).astype(BF16)


def _ret_ctx(qk, v, rg, decay, l):
    t = SEQ
    w = RET_HEADS * RET_DK
    return pl.pallas_call(
        functools.partial(_ret_kernel, lat=False, T=t),
        grid=(BATCH,),
        in_specs=[pl.BlockSpec((t, w), lambda b: (b, 0)),
                  pl.BlockSpec((t, w), lambda b: (b, 1)),
                  pl.BlockSpec((t, w), lambda b: (b, 0)),
                  pl.BlockSpec((t, w), lambda b: (b, 0)),
                  pl.BlockSpec((None, 2, RET_HEADS), lambda b: (l, 0, 0))],
        out_specs=[pl.BlockSpec((t, w), lambda b: (b, 0)),
                   pl.BlockSpec((None, 2, RET_HEADS, RET_DK, RET_DV), lambda b: (b, 0, 0, 0, 0))],
        out_shape=[jax.ShapeDtypeStruct((N_TOK, w), BF16),
                   jax.ShapeDtypeStruct((BATCH, 2, RET_HEADS, RET_DK, RET_DV), F32)],
        scratch_shapes=[pltpu.VMEM((RET_HEADS, t, t), F32)],
        compiler_params=_params("arbitrary"),
        name="ret_ctx",
    )(qk, qk, v, rg, decay)


def _ret_lat(qk, v, rg, decay, state_ret, l, prev):
    tq = ROW_TILE
    t = DEC_SEQ
    nq = t // tq
    w = RET_HEADS * RET_DK
    blk0 = N_CTX // t
    qblk0 = N_CTX // tq
    return pl.pallas_call(
        functools.partial(_ret_kernel, lat=True, T=t),
        grid=(nq, DEC_BATCH),
        in_specs=[pl.BlockSpec((tq, w), lambda i, b: (qblk0 + b * nq + i, 0)),
                  pl.BlockSpec((t, w), lambda i, b: (blk0 + b, 1)),
                  pl.BlockSpec((t, w), lambda i, b: (blk0 + b, 0)),
                  pl.BlockSpec((tq, w), lambda i, b: (qblk0 + b * nq + i, 0)),
                  pl.BlockSpec((None, 2, RET_HEADS), lambda i, b: (l, 0, 0)),
                  pl.BlockSpec((None, None, 2, RET_HEADS, RET_DK, RET_DV), lambda i, b: (b, l, 0, 0, 0, 0)),
                  pl.BlockSpec(memory_space=pl.ANY)],
        out_specs=pl.BlockSpec((tq, w), lambda i, b: (qblk0 + b * nq + i, 0)),
        out_shape=jax.ShapeDtypeStruct((N_TOK, w), BF16),
        scratch_shapes=[pltpu.VMEM((RET_HEADS, tq, t), F32)],
        input_output_aliases={6: 0},
        compiler_params=_params("arbitrary", "arbitrary"),
        name="ret_lat",
    )(qk, qk, v, rg, decay, state_ret, prev)


def _rope_tables(dim, reps):
    half = dim // 2
    quarter = half // 2
    t = jnp.arange(DEC_SEQ)
    row = (t // GRID_W).astype(F32)
    col = (t % GRID_W).astype(F32)
    inv = ROPE_BASE ** (-jnp.arange(0, half, 2, dtype=F32) / half)
    zeros = jnp.zeros((DEC_SEQ, quarter), F32)
    cos_parts, sa_parts, sb_parts = [], [], []
    for pos in (row, col):
        ang = pos[:, None] * inv[None, :]
        c, s = jnp.cos(ang), jnp.sin(ang)
        cos_parts += [c, c]
        sa_parts += [zeros, s]
        sb_parts += [-s, zeros]
    padw = LANES - dim

    def build(parts):
        tab = jnp.concatenate(parts + ([jnp.zeros((DEC_SEQ, padw), F32)] if padw else []), axis=-1)
        return jnp.tile(tab, (1, reps))

    return build(cos_parts), build(sa_parts), build(sb_parts)


def kernel(x_prompt, x_sample, c, cache_mla_ckv, cache_mla_krope, state_lru, state_ret, c_ctx, w_mod, b_mod,
           g_norm, w_in, lru_conv_w, lru_conv_b, lru_wa, lru_ba, lru_wx, lru_bx, lru_lam, mla_gq, mla_gkv,
           mla_wuq, mla_wukv, ret_decay, w_br_lru, w_br_mla, w_br_ret, w_out, w_ff1, w_ff2):
    x = jnp.concatenate([x_prompt.reshape(N_CTX, D_MODEL), x_sample.reshape(N_LAT, D_MODEL)], axis=0)
    cond8 = jnp.concatenate([c_ctx[None, :], c, jnp.zeros((8 - 1 - DEC_BATCH, D_MODEL), F32)], axis=0)
    mod = _ada(cond8, w_mod, b_mod).reshape(DEPTH * 8, 1, 6 * D_MODEL)
    g_all = g_norm.reshape(DEPTH * 4, 1, D_MODEL)

    w_in_tail = w_in[:, :, COL_TAIL:].astype(BF16)
    wuq = jnp.pad(mla_wuq.reshape(DEPTH, MLA_Q_RANK, MLA_HEADS, MLA_NOPE + MLA_ROPE),
                  ((0, 0), (0, 0), (0, 0), (0, MLA_QH - MLA_NOPE - MLA_ROPE)))
    wuq = wuq.reshape(DEPTH, MLA_Q_RANK, MLA_HEADS * MLA_QH)
    lw = {"conv_w": lru_conv_w, "conv_b": lru_conv_b.reshape(DEPTH, 2, 1, D_RNN), "wa": lru_wa,
          "ba": lru_ba.reshape(DEPTH, 2, 1, D_RNN), "wx": lru_wx, "bx": lru_bx.reshape(DEPTH, 2, 1, D_RNN),
          "lam": lru_lam.reshape(DEPTH, 2, 1, D_RNN)}
    gq = mla_gq.reshape(DEPTH, 1, MLA_Q_RANK)
    gkv = mla_gkv.reshape(DEPTH, 1, MLA_KV_RANK)
    h0_lat = state_lru.reshape(DEC_BATCH, DEPTH, 2, 1, D_RNN)
    cache_ckv_rows = cache_mla_ckv.reshape(DEC_BATCH * DEPTH * PAST_LEN, MLA_KV_RANK)
    tab_mla = _rope_tables(MLA_ROPE, 1)
    tab_ret = _rope_tables(RET_DK, RET_HEADS)
    wq = RET_HEADS * RET_DK

    ckv_out, kr_out, lru_out, ret_out = [], [], [], []
    h, = _resnorm(x, mod, g_all, nxt=(0, 0, 0))
    for l in range(DEPTH):
        proj_a = _mm(h, w_in, l, col0=0, ncols=3072)
        proj_qk = _mm(h, w_in_tail, l, col0=0, ncols=2 * wq)
        proj_v = _mm(h, w_in_tail, l, col0=2 * wq, ncols=wq, out_dtype=BF16)
        proj_rg = _mm(h, w_in_tail, l, col0=3 * wq, ncols=wq)
        gates = _mm(h, w_in_tail, l, col0=4 * wq, ncols=3 * D_MODEL, out_dtype=BF16, act="sigmoid")

        y_lru, st_lru = _lru(proj_a, lw, l, T=SEQ, nb=BATCH, blk0=0)
        y_lru, _ = _lru(proj_a, lw, l, T=DEC_SEQ, nb=DEC_BATCH, blk0=N_CTX // DEC_SEQ, h0=h0_lat, prev=y_lru)

        cqn, ckv, ckvb, kr = _mla_prep(proj_a, gq, gkv, l, tab_mla)
        q = _mm(cqn, wuq, l)
        kv = _mm(ckvb, mla_wukv, l, out_dtype=BF16)
        kv_cache = _mm(cache_ckv_rows, mla_wukv, l, out_dtype=BF16, tm=PAST_LEN, m=DEC_BATCH * PAST_LEN,
                       row_map=lambda i, l=l: i * DEPTH + l)
        y_mla = _attn_ctx(q, kv, kr)
        y_mla = _attn_lat(q, kv, kr, kv_cache, cache_mla_krope, l, tab_mla, y_mla)

        qk = _ret_prep(proj_qk, tab_ret)
        y_ret, st_ret = _ret_ctx(qk, proj_v, proj_rg, ret_decay, l)
        y_ret = _ret_lat(qk, proj_v, proj_rg, ret_decay, state_ret, l, y_ret)

        merged = _branch_merge((y_lru, y_mla, y_ret), (w_br_lru, w_br_mla, w_br_ret), gates, l)
        u = _mm(merged, w_out, l)
        x, h2 = _resnorm(x, mod, g_all, res=(u, l * 4 + 1, l, 2), nxt=(l * 4 + 2, l, 3))
        ff = _mm(h2, w_ff1, l, out_dtype=BF16, act="relu2")
        y = _mm(ff, w_ff2, l, tn=1024, tk=1024)
        if l + 1 < DEPTH:
            x, h = _resnorm(x, mod, g_all, res=(y, l * 4 + 3, l, 5), nxt=((l + 1) * 4, l + 1, 0))
        else:
            x, = _resnorm(x, mod, g_all, res=(y, l * 4 + 3, l, 5))

        ckv_out.append(ckv[:N_CTX].reshape(BATCH, SEQ, MLA_KV_RANK))
        kr_out.append(kr[:N_CTX, :MLA_ROPE].reshape(BATCH, SEQ, MLA_ROPE))
        lru_out.append(st_lru.reshape(BATCH, 2, D_RNN))
        ret_out.append(st_ret)

    y_prompt = x[:N_CTX].reshape(BATCH, SEQ, D_MODEL)
    y_sample = x[N_CTX:].reshape(DEC_BATCH, DEC_SEQ, D_MODEL)
    return (y_prompt, y_sample, jnp.stack(ckv_out, axis=1), jnp.stack(kr_out, axis=1),
            jnp.stack(lru_out, axis=1), jnp.stack(ret_out, axis=1))
```

```python
import functools

import jax
import jax.numpy as jnp
from jax import lax
from jax.experimental import pallas as pl
from jax.experimental.pallas import tpu as pltpu

F32 = jnp.float32
BF16 = jnp.bfloat16

D_MODEL = 2048
BATCH = 16
SEQ = 256
DEPTH = 4
DEC_BATCH = 4
DEC_SEQ = 1024
PAST_LEN = 256
GRID_W = 64
EPS = 1e-6
ROPE_BASE = 10000.0
D_RNN = D_MODEL // 2
LRU_BLOCKS = 8
LRU_BS = D_RNN // LRU_BLOCKS
CONV_W = 4
LRU_C = 8.0
MLA_HEADS = 8
MLA_NOPE = 128
MLA_ROPE = 64
MLA_V = 128
MLA_Q_RANK = D_MODEL // 4
MLA_KV_RANK = D_MODEL // 8
RET_HEADS = 8
RET_DK = 128
RET_DV = 128
D_FF = 4 * D_MODEL

N_CTX = BATCH * SEQ
N_LAT = DEC_BATCH * DEC_SEQ
N_TOK = N_CTX + N_LAT
COL_TAIL = 2 * D_RNN + MLA_Q_RANK + MLA_KV_RANK + MLA_ROPE
MLA_QH = 256

LANES = 128
SUBLANES = 8
VMEM_LIMIT = 56 * 1024 * 1024
ROW_TILE = 256


def _params(*sem):
    return pltpu.CompilerParams(dimension_semantics=sem, vmem_limit_bytes=VMEM_LIMIT)


def _sigmoid(x):
    return 0.5 * (1.0 + jnp.tanh(0.5 * x))


def _softplus(x):
    return jnp.maximum(x, 0.0) + jnp.log(1.0 + jnp.exp(-jnp.abs(x)))


def _rms(x, g):
    return x * lax.rsqrt(jnp.mean(x * x, axis=-1, keepdims=True) + EPS) * g


def _dot(a, b):
    return jnp.dot(a, b, preferred_element_type=F32)


def _dot_nt(a, b):
    return lax.dot_general(a, b, (((1,), (1,)), ((), ())), preferred_element_type=F32)


def _dot_tn(a, b):
    return lax.dot_general(a, b, (((0,), (0,)), ((), ())), preferred_element_type=F32)


def _rope(x, cos, sa, sb, quarter):
    w = x.shape[-1]
    return x * cos + pltpu.roll(x, quarter, 1) * sa + pltpu.roll(x, w - quarter, 1) * sb


def _mod_row(row0):
    return jnp.where(row0 < N_CTX, 0, 1 + (row0 - N_CTX) // DEC_SEQ)


def _ada_kernel(c_ref, w_ref, b_ref, o_ref):
    c = c_ref[...]
    s = c * _sigmoid(c)
    o_ref[...] = _dot(s.astype(BF16), w_ref[...].astype(BF16)) + b_ref[...]


def _ada(cond8, w_mod, b_mod):
    tn = 1024
    n = w_mod.shape[-1]
    return pl.pallas_call(
        _ada_kernel,
        grid=(DEPTH, n // tn),
        in_specs=[pl.BlockSpec((8, D_MODEL), lambda l, j: (0, 0)),
                  pl.BlockSpec((None, D_MODEL, tn), lambda l, j: (l, 0, j)),
                  pl.BlockSpec((None, 1, tn), lambda l, j: (l, 0, j))],
        out_specs=pl.BlockSpec((None, 8, tn), lambda l, j: (l, 0, j)),
        out_shape=jax.ShapeDtypeStruct((DEPTH, 8, n), F32),
        compiler_params=_params("arbitrary", "arbitrary"),
        name="ada",
    )(cond8, w_mod, b_mod.reshape(DEPTH, 1, n))


def _resnorm_kernel(*refs, has_res, has_next):
    refs = list(refs)
    x_ref = refs.pop(0)
    x = x_ref[...]
    if has_res:
        u_ref, gpost_ref, gate_ref = refs[:3]
        refs = refs[3:]
        x = x + gate_ref[...] * _rms(u_ref[...], gpost_ref[...])
    if has_next:
        gpre_ref, shift_ref, scale_ref = refs[:3]
        refs = refs[3:]
    if has_res:
        xo_ref = refs.pop(0)
        xo_ref[...] = x
    if has_next:
        ho_ref = refs.pop(0)
        h = _rms(x, gpre_ref[...]) * (1.0 + scale_ref[...]) + shift_ref[...]
        ho_ref[...] = h.astype(BF16)


def _resnorm(x, mod, g_norm, *, res=None, nxt=None):
    tr = ROW_TILE
    row = pl.BlockSpec((tr, D_MODEL), lambda i: (i, 0))

    def gspec(k):
        return pl.BlockSpec((None, 1, D_MODEL), lambda i: (k, 0, 0))

    def mspec(l, chunk):
        return pl.BlockSpec((None, 1, D_MODEL), lambda i: (l * 8 + _mod_row(i * tr), 0, chunk))

    args, specs, outs, ospecs = [x], [row], [], []
    if res is not None:
        u, gk, l, chunk = res
        args += [u, g_norm, mod]
        specs += [row, gspec(gk), mspec(l, chunk)]
        outs.append(jax.ShapeDtypeStruct((N_TOK, D_MODEL), F32))
        ospecs.append(row)
    if nxt is not None:
        gk, l, chunk = nxt
        args += [g_norm, mod, mod]
        specs += [gspec(gk), mspec(l, chunk), mspec(l, chunk + 1)]
        outs.append(jax.ShapeDtypeStruct((N_TOK, D_MODEL), BF16))
        ospecs.append(row)
    res_out = pl.pallas_call(
        functools.partial(_resnorm_kernel, has_res=res is not None, has_next=nxt is not None),
        grid=(N_TOK // tr,),
        in_specs=specs, out_specs=ospecs, out_shape=outs,
        compiler_params=_params("arbitrary"),
        name="resnorm",
    )(*args)
    return res_out


def _mm_kernel(x_ref, w_ref, *rest, nk, act, kscale, rope_quarter):
    o_ref = rest[-1]
    if nk == 1:
        part = _dot(x_ref[...].astype(BF16), w_ref[...].astype(BF16))
        if act == "relu2":
            part = jnp.square(jnp.maximum(part, 0.0))
        elif act == "sigmoid":
            part = _sigmoid(part)
        if kscale is not None:
            part = part * jnp.where(pl.program_id(1) >= kscale[0], kscale[1], 1.0)
        if rope_quarter is not None:
            cos_ref, sa_ref, sb_ref = rest[:3]
            part = _rope(part, cos_ref[...], sa_ref[...], sb_ref[...], rope_quarter)
        o_ref[...] = part.astype(o_ref.dtype)
        return

    @pl.when(pl.program_id(2) == 0)
    def _():
        o_ref[...] = jnp.zeros(o_ref.shape, o_ref.dtype)

    o_ref[...] += _dot(x_ref[...].astype(BF16), w_ref[...].astype(BF16))


def _mm(x, w, l, *, col0=0, ncols=None, out_dtype=F32, act=None, tm=2048, tn=512, tk=None,
        m=None, row_map=None, kscale=None, rope=None):
    kdim = w.shape[1]
    ncols = w.shape[2] - col0 if ncols is None else ncols
    m = x.shape[0] if m is None else m
    tk = kdim if tk is None else tk
    tm, tn = min(tm, m), min(tn, ncols)
    assert m % tm == 0 and ncols % tn == 0 and kdim % tk == 0 and col0 % tn == 0
    nk = kdim // tk
    assert nk == 1 or (act is None and out_dtype == F32 and kscale is None and rope is None)
    jb = col0 // tn
    row_map = (lambda i: i) if row_map is None else row_map
    args = [x, w]
    in_specs = [pl.BlockSpec((tm, tk), lambda i, j, k: (row_map(i), k)),
                pl.BlockSpec((None, tk, tn), lambda i, j, k: (l, k, jb + j))]
    if kscale is not None:
        assert kscale[0] % tn == 0
        kscale = (kscale[0] // tn, kscale[1])
    if rope is not None:
        tabs, quarter, ntab = rope
        assert tm == DEC_SEQ and tabs[0].shape == (DEC_SEQ, ntab * tn)
        args += list(tabs)
        in_specs += [pl.BlockSpec((tm, tn), lambda i, j, k: (0, j % ntab))] * 3
    return pl.pallas_call(
        functools.partial(_mm_kernel, nk=nk, act=act, kscale=kscale,
                          rope_quarter=None if rope is None else rope[1]),
        grid=(m // tm, ncols // tn, nk),
        in_specs=in_specs,
        out_specs=pl.BlockSpec((tm, tn), lambda i, j, k: (i, j)),
        out_shape=jax.ShapeDtypeStruct((m, ncols), out_dtype),
        compiler_params=_params("arbitrary", "arbitrary", "arbitrary"),
        name="mm",
    )(*args)


TAIL_BASE = COL_TAIL - LANES // 2
TAIL_COLS = 4 * RET_HEADS * RET_DK + 3 * D_MODEL
REALIGN_TN = 256


def _realign_kernel(w1_ref, w2_ref, o_ref):
    half = LANES // 2
    o_ref[...] = jnp.concatenate([w1_ref[:, half:], w2_ref[:, :half]], axis=1).astype(BF16)


def _realign_tail(w_in):
    tn = REALIGN_TN
    assert TAIL_BASE % tn == 0 and TAIL_COLS % tn == 0
    j0 = TAIL_BASE // tn
    per = tn // LANES
    return pl.pallas_call(
        _realign_kernel,
        grid=(DEPTH, TAIL_COLS // tn),
        in_specs=[pl.BlockSpec((None, D_MODEL, tn), lambda l, j: (l, 0, j0 + j)),
                  pl.BlockSpec((None, D_MODEL, LANES), lambda l, j: (l, 0, (j0 + j + 1) * per))],
        out_specs=pl.BlockSpec((None, D_MODEL, tn), lambda l, j: (l, 0, j)),
        out_shape=jax.ShapeDtypeStruct((DEPTH, D_MODEL, TAIL_COLS), BF16),
        compiler_params=_params("arbitrary", "arbitrary"),
        name="realign_tail",
    )(w_in, w_in)


def _bm_kernel(a0_ref, a1_ref, a2_ref, w0_ref, w1_ref, w2_ref, g0_ref, g1_ref, g2_ref, o_ref):
    acc = g0_ref[...].astype(F32) * _dot(a0_ref[...], w0_ref[...].astype(BF16))
    acc = acc + g1_ref[...].astype(F32) * _dot(a1_ref[...], w1_ref[...].astype(BF16))
    acc = acc + g2_ref[...].astype(F32) * _dot(a2_ref[...], w2_ref[...].astype(BF16))
    o_ref[...] = acc.astype(BF16)


def _branch_merge(acts, ws, gates, l, *, tm=1024, tn=512):
    nj = D_MODEL // tn
    kdim = acts[0].shape[1]
    aspec = pl.BlockSpec((tm, kdim), lambda i, j: (i, 0))
    wspec = pl.BlockSpec((None, kdim, tn), lambda i, j: (l, 0, j))
    gspecs = [pl.BlockSpec((tm, tn), lambda i, j, b=b: (i, b * nj + j)) for b in range(3)]
    return pl.pallas_call(
        _bm_kernel,
        grid=(N_TOK // tm, nj),
        in_specs=[aspec] * 3 + [wspec] * 3 + gspecs,
        out_specs=pl.BlockSpec((tm, tn), lambda i, j: (i, j)),
        out_shape=jax.ShapeDtypeStruct((N_TOK, D_MODEL), BF16),
        compiler_params=_params("arbitrary", "arbitrary"),
        name="branch_merge",
    )(*acts, *ws, gates, gates, gates)


LRU_CT = 512


def _lru_kernel(*refs, T, has_h0):
    if has_h0:
        (xr_ref, xg_ref, cw_ref, cb_ref, wa_ref, ba_ref, wx_ref, bx_ref, lam_ref, h0_ref, _,
         y_ref, st_ref, xpad_ref, a_ref, b_ref, h_ref) = refs
    else:
        (xr_ref, xg_ref, cw_ref, cb_ref, wa_ref, ba_ref, wx_ref, bx_ref, lam_ref,
         y_ref, st_ref, xpad_ref, a_ref, b_ref, h_ref) = refs
    ct = xr_ref.shape[1]
    groups = T // SUBLANES
    pad = SUBLANES
    xpad_ref[0:pad, :] = jnp.zeros((pad, ct), F32)
    xpad_ref[pad + T:2 * pad + T, :] = jnp.zeros((pad, ct), F32)
    xpad_ref[pad:pad + T, :] = xr_ref[...]
    row8 = lax.broadcasted_iota(jnp.int32, (SUBLANES, ct), 0)

    for d in range(2):
        for n in range(ct // LRU_BS):
            cs = slice(n * LRU_BS, (n + 1) * LRU_BS)
            xc = jnp.zeros((T, LRU_BS), F32) + cb_ref[d, :, cs]
            for j in range(CONV_W):
                off = (j - (CONV_W - 1)) if d == 0 else ((CONV_W - 1) - j)
                xc = xc + cw_ref[d, j:j + 1, cs] * xpad_ref[pad + off:pad + off + T, cs]
            xcb = xc.astype(BF16)
            r = _sigmoid(_dot(xcb, wa_ref[d, n].astype(BF16)) + ba_ref[d, :, cs])
            i = _sigmoid(_dot(xcb, wx_ref[d, n].astype(BF16)) + bx_ref[d, :, cs])
            log_a = (-LRU_C) * r * _softplus(-lam_ref[d, :, cs])
            a = jnp.exp(log_a)
            a_ref[:, cs] = a
            b_ref[:, cs] = jnp.sqrt(1.0 - a * a) * (i * xc)

        h0 = h0_ref[d] if has_h0 else jnp.zeros((1, ct), F32)

        def body(it, carry, d=d):
            g = it if d == 0 else groups - 1 - it
            r0 = pl.multiple_of(g * SUBLANES, SUBLANES)
            av = a_ref[pl.ds(r0, SUBLANES), :]
            bv = b_ref[pl.ds(r0, SUBLANES), :]
            for s in (1, 2, 4):
                if d == 0:
                    ok = row8 >= s
                    shift = s
                else:
                    ok = row8 < SUBLANES - s
                    shift = SUBLANES - s
                a_sh = jnp.where(ok, pltpu.roll(av, shift, 0), 1.0)
                b_sh = jnp.where(ok, pltpu.roll(bv, shift, 0), 0.0)
                bv = av * b_sh + bv
                av = av * a_sh
            h = av * carry + bv
            if d == 0:
                h_ref[pl.ds(r0, SUBLANES), :] = h
                return h[SUBLANES - 1:SUBLANES, :]
            h_ref[pl.ds(r0, SUBLANES), :] = h_ref[pl.ds(r0, SUBLANES), :] + h
            return h[0:1, :]

        last = lax.fori_loop(0, groups, body, h0, unroll=4)
        st_ref[d] = last

    xg = xg_ref[...]
    gelu = 0.5 * xg * (1.0 + jnp.tanh(0.7978845608028654 * (xg + 0.044715 * (xg * xg * xg))))
    y_ref[...] = (h_ref[...] * gelu).astype(BF16)


def _lru(proj_a, lw, l, *, T, nb, blk0, h0=None, prev=None):
    ct = LRU_CT
    ncb = D_RNN // ct
    bpc = ct // LRU_BS
    in_specs = [
        pl.BlockSpec((T, ct), lambda b, c: (blk0 + b, c)),
        pl.BlockSpec((T, ct), lambda b, c: (blk0 + b, ncb + c)),
        pl.BlockSpec((None, 2, CONV_W, ct), lambda b, c: (l, 0, 0, c)),
        pl.BlockSpec((None, 2, 1, ct), lambda b, c: (l, 0, 0, c)),
        pl.BlockSpec((None, 2, bpc, LRU_BS, LRU_BS), lambda b, c: (l, 0, c, 0, 0)),
        pl.BlockSpec((None, 2, 1, ct), lambda b, c: (l, 0, 0, c)),
        pl.BlockSpec((None, 2, bpc, LRU_BS, LRU_BS), lambda b, c: (l, 0, c, 0, 0)),
        pl.BlockSpec((None, 2, 1, ct), lambda b, c: (l, 0, 0, c)),
        pl.BlockSpec((None, 2, 1, ct), lambda b, c: (l, 0, 0, c)),
    ]
    args = [proj_a, proj_a, lw["conv_w"], lw["conv_b"], lw["wa"], lw["ba"], lw["wx"], lw["bx"], lw["lam"]]
    if h0 is not None:
        in_specs.append(pl.BlockSpec((None, None, 2, 1, ct), lambda b, c: (b, l, 0, 0, c)))
        args.append(h0)
        in_specs.append(pl.BlockSpec(memory_space=pl.ANY))
        args.append(prev)
    return pl.pallas_call(
        functools.partial(_lru_kernel, T=T, has_h0=h0 is not None),
        grid=(nb, ncb),
        in_specs=in_specs,
        out_specs=[pl.BlockSpec((T, ct), lambda b, c: (blk0 + b, c)),
                   pl.BlockSpec((None, 2, 1, ct), lambda b, c: (b, 0, 0, c))],
        out_shape=[jax.ShapeDtypeStruct((N_TOK, D_RNN), BF16),
                   jax.ShapeDtypeStruct((nb, 2, 1, D_RNN), F32)],
        input_output_aliases={} if prev is None else {len(args) - 1: 0},
        scratch_shapes=[pltpu.VMEM((T + 2 * SUBLANES, ct), F32), pltpu.VMEM((T, ct), F32),
                        pltpu.VMEM((T, ct), F32), pltpu.VMEM((T, ct), F32)],
        compiler_params=_params("arbitrary", "arbitrary"),
        name="lru",
    )(*args)


def _mla_prep_kernel(p_ref, gq_ref, gkv_ref, cos_ref, sa_ref, sb_ref, cq_ref, ckv_ref, ckvb_ref, kr_ref):
    i = pl.program_id(0)
    cq_ref[...] = _rms(p_ref[:, 0:MLA_Q_RANK], gq_ref[...]).astype(BF16)
    ckv = _rms(p_ref[:, MLA_Q_RANK:MLA_Q_RANK + MLA_KV_RANK], gkv_ref[...])
    ckv_ref[...] = ckv
    ckvb_ref[...] = ckv.astype(BF16)
    k0 = MLA_Q_RANK + MLA_KV_RANK
    kr = p_ref[:, k0:k0 + LANES]

    @pl.when(i < N_CTX // ROW_TILE)
    def _():
        kr_ref[...] = kr

    @pl.when(i >= N_CTX // ROW_TILE)
    def _():
        kr_ref[...] = _rope(kr, cos_ref[...], sa_ref[...], sb_ref[...], MLA_ROPE // 4)


def _lat_tile(i):
    return (jnp.maximum(i - N_CTX // ROW_TILE, 0)) % (DEC_SEQ // ROW_TILE)


def _mla_prep(proj_a, gq, gkv, l, tab):
    tr = ROW_TILE
    tspec = pl.BlockSpec((tr, LANES), lambda i: (_lat_tile(i), 0))
    return pl.pallas_call(
        _mla_prep_kernel,
        grid=(N_TOK // tr,),
        in_specs=[pl.BlockSpec((tr, 1024), lambda i: (i, 2)),
                  pl.BlockSpec((None, 1, MLA_Q_RANK), lambda i: (l, 0, 0)),
                  pl.BlockSpec((None, 1, MLA_KV_RANK), lambda i: (l, 0, 0)),
                  tspec, tspec, tspec],
        out_specs=[pl.BlockSpec((tr, MLA_Q_RANK), lambda i: (i, 0)),
                   pl.BlockSpec((tr, MLA_KV_RANK), lambda i: (i, 0)),
                   pl.BlockSpec((tr, MLA_KV_RANK), lambda i: (i, 0)),
                   pl.BlockSpec((tr, LANES), lambda i: (i, 0))],
        out_shape=[jax.ShapeDtypeStruct((N_TOK, MLA_Q_RANK), BF16),
                   jax.ShapeDtypeStruct((N_TOK, MLA_KV_RANK), F32),
                   jax.ShapeDtypeStruct((N_TOK, MLA_KV_RANK), BF16),
                   jax.ShapeDtypeStruct((N_TOK, LANES), F32)],
        compiler_params=_params("arbitrary"),
        name="mla_prep",
    )(proj_a, gq, gkv, *tab)


def _attn_kernel(*refs, lat):
    if lat:
        q_ref, kvn_ref, krn_ref, kvc_ref, krc_ref, cos_ref, sa_ref, sb_ref, _, o_ref = refs
    else:
        q_ref, kvn_ref, krn_ref, o_ref = refs
    scale = (MLA_NOPE + MLA_ROPE) ** -0.5
    krn = krn_ref[:, 0:MLA_ROPE].astype(BF16)
    if lat:
        krc = krc_ref[...].astype(BF16)
        kvc = kvc_ref[...].astype(BF16)
    for h in range(MLA_HEADS):
        c0 = h * MLA_QH
        qn = q_ref[:, c0:c0 + MLA_NOPE].astype(BF16)
        qr = q_ref[:, c0 + MLA_NOPE:c0 + MLA_QH]
        if lat:
            qr = _rope(qr, cos_ref[...], sa_ref[...], sb_ref[...], MLA_ROPE // 4)
        qr = qr[:, 0:MLA_ROPE].astype(BF16)
        kn = kvn_ref[:, c0:c0 + MLA_NOPE]
        v = kvn_ref[:, c0 + MLA_NOPE:c0 + MLA_QH]
        s2 = _dot_nt(qn, kn) + _dot_nt(qr, krn)
        m = jnp.max(s2, axis=-1, keepdims=True)
        if lat:
            s1 = _dot_nt(qn, kvc[:, c0:c0 + MLA_NOPE]) + _dot_nt(qr, krc)
            m = jnp.maximum(m, jnp.max(s1, axis=-1, keepdims=True))
            p1 = jnp.exp((s1 - m) * scale)
        p2 = jnp.exp((s2 - m) * scale)
        den = jnp.sum(p2, axis=-1, keepdims=True)
        o = _dot(p2.astype(BF16), v)
        if lat:
            den = den + jnp.sum(p1, axis=-1, keepdims=True)
            o = o + _dot(p1.astype(BF16), kvc[:, c0 + MLA_NOPE:c0 + MLA_QH])
        o_ref[:, h * MLA_V:(h + 1) * MLA_V] = (o / den).astype(BF16)


def _attn_ctx(q, kv, kr):
    t = SEQ
    return pl.pallas_call(
        functools.partial(_attn_kernel, lat=False),
        grid=(BATCH,),
        in_specs=[pl.BlockSpec((t, MLA_HEADS * MLA_QH), lambda b: (b, 0)),
                  pl.BlockSpec((t, MLA_HEADS * MLA_QH), lambda b: (b, 0)),
                  pl.BlockSpec((t, LANES), lambda b: (b, 0))],
        out_specs=pl.BlockSpec((t, MLA_HEADS * MLA_V), lambda b: (b, 0)),
        out_shape=jax.ShapeDtypeStruct((N_TOK, MLA_HEADS * MLA_V), BF16),
        compiler_params=_params("arbitrary"),
        name="attn_ctx",
    )(q, kv, kr)


def _attn_lat(q, kv, kr, kv_cache, cache_krope, l, tab, prev):
    tq = ROW_TILE
    nq = DEC_SEQ // tq
    blk0 = N_CTX // DEC_SEQ
    qblk0 = N_CTX // tq
    tspec = pl.BlockSpec((tq, LANES), lambda b, i: (i, 0))
    return pl.pallas_call(
        functools.partial(_attn_kernel, lat=True),
        grid=(DEC_BATCH, nq),
        in_specs=[pl.BlockSpec((tq, MLA_HEADS * MLA_QH), lambda b, i: (qblk0 + b * nq + i, 0)),
                  pl.BlockSpec((DEC_SEQ, MLA_HEADS * MLA_QH), lambda b, i: (blk0 + b, 0)),
                  pl.BlockSpec((DEC_SEQ, LANES), lambda b, i: (blk0 + b, 0)),
                  pl.BlockSpec((PAST_LEN, MLA_HEADS * MLA_QH), lambda b, i: (b, 0)),
                  pl.BlockSpec((None, None, PAST_LEN, MLA_ROPE), lambda b, i: (b, l, 0, 0)),
                  tspec, tspec, tspec, pl.BlockSpec(memory_space=pl.ANY)],
        out_specs=pl.BlockSpec((tq, MLA_HEADS * MLA_V), lambda b, i: (qblk0 + b * nq + i, 0)),
        out_shape=jax.ShapeDtypeStruct((N_TOK, MLA_HEADS * MLA_V), BF16),
        input_output_aliases={8: 0},
        compiler_params=_params("arbitrary", "arbitrary"),
        name="attn_lat",
    )(q, kv, kr, kv_cache, cache_krope, *tab, prev)


def _log_sigmoid(x):
    return -_softplus(-x)


def _ret_kernel(*refs, lat, T):
    if lat:
        q_ref, k_ref, v_ref, rg_ref, dec_ref, s0_ref, _, o_ref, decay_ref = refs
    else:
        q_ref, k_ref, v_ref, rg_ref, dec_ref, o_ref, st_ref, decay_ref = refs
    tq = q_ref.shape[0]
    t0 = pl.program_id(0) * tq if lat else 0
    batch_id = pl.program_id(1) if lat else pl.program_id(0)
    lg = _log_sigmoid(dec_ref[...])
    tcol = (t0 + lax.broadcasted_iota(jnp.int32, (tq, 1), 0)).astype(F32)

    @pl.when(batch_id == 0)
    def _():
        rows = (t0 + lax.broadcasted_iota(jnp.int32, (tq, T), 0)).astype(F32)
        cols = lax.broadcasted_iota(jnp.int32, (tq, T), 1).astype(F32)
        diff = rows - cols
        fwd = diff >= 0.0
        bwd = diff <= 0.0
        dpos = jnp.maximum(diff, 0.0)
        dneg = jnp.maximum(-diff, 0.0)
        for h in range(RET_HEADS):
            decay_ref[h] = (jnp.where(fwd, jnp.exp(lg[0:1, h:h + 1] * dpos), 0.0)
                            + jnp.where(bwd, jnp.exp(lg[1:2, h:h + 1] * dneg), 0.0))

    for h in range(RET_HEADS):
        cs = slice(h * RET_DK, (h + 1) * RET_DK)
        lgf = lg[0:1, h:h + 1]
        lgb = lg[1:2, h:h + 1]
        q = q_ref[:, cs]
        k = k_ref[:, cs]
        v = v_ref[:, cs]
        sc = _dot_nt(q, k) * decay_ref[h]
        o = _dot(sc.astype(BF16), v)
        if lat:
            qf = q.astype(F32)
            o = o + _dot((qf * jnp.exp(lgf * (tcol + 1.0))).astype(BF16), s0_ref[0, h].astype(BF16))
            o = o + _dot((qf * jnp.exp(lgb * (T - tcol))).astype(BF16), s0_ref[1, h].astype(BF16))
        else:
            kf = k.astype(F32)
            st_ref[0, h] = _dot_tn((kf * jnp.exp(lgf * (T - 1.0 - tcol))).astype(BF16), v)
            st_ref[1, h] = _dot_tn((kf * jnp.exp(lgb * tcol)).astype(BF16), v)
        mu = jnp.mean(o, axis=-1, keepdims=True)
        oc = o - mu
        var = jnp.mean(oc * oc, axis=-1, keepdims=True)
        rg = rg_ref[:, cs]
        o_ref[:, cs] = (oc * lax.rsqrt(var + EPS) * (rg * _sigmoid(rg))).astype(BF16)


def _ret_ctx(qk, v, rg, decay, l):
    t = SEQ
    w = RET_HEADS * RET_DK
    return pl.pallas_call(
        functools.partial(_ret_kernel, lat=False, T=t),
        grid=(BATCH,),
        in_specs=[pl.BlockSpec((t, w), lambda b: (b, 0)),
                  pl.BlockSpec((t, w), lambda b: (b, 1)),
                  pl.BlockSpec((t, w), lambda b: (b, 0)),
                  pl.BlockSpec((t, w), lambda b: (b, 0)),
                  pl.BlockSpec((None, 2, RET_HEADS), lambda b: (l, 0, 0))],
        out_specs=[pl.BlockSpec((t, w), lambda b: (b, 0)),
                   pl.BlockSpec((None, 2, RET_HEADS, RET_DK, RET_DV), lambda b: (b, 0, 0, 0, 0))],
        out_shape=[jax.ShapeDtypeStruct((N_TOK, w), BF16),
                   jax.ShapeDtypeStruct((BATCH, 2, RET_HEADS, RET_DK, RET_DV), F32)],
        scratch_shapes=[pltpu.VMEM((RET_HEADS, t, t), F32)],
        compiler_params=_params("arbitrary"),
        name="ret_ctx",
    )(qk, qk, v, rg, decay)


def _ret_lat(qk, v, rg, decay, state_ret, l, prev):
    tq = ROW_TILE
    t = DEC_SEQ
    nq = t // tq
    w = RET_HEADS * RET_DK
    blk0 = N_CTX // t
    qblk0 = N_CTX // tq
    return pl.pallas_call(
        functools.partial(_ret_kernel, lat=True, T=t),
        grid=(nq, DEC_BATCH),
        in_specs=[pl.BlockSpec((tq, w), lambda i, b: (b * nq + i, 0)),
                  pl.BlockSpec((t, w), lambda i, b: (b, 1)),
                  pl.BlockSpec((t, w), lambda i, b: (blk0 + b, 0)),
                  pl.BlockSpec((tq, w), lambda i, b: (qblk0 + b * nq + i, 0)),
                  pl.BlockSpec((None, 2, RET_HEADS), lambda i, b: (l, 0, 0)),
                  pl.BlockSpec((None, None, 2, RET_HEADS, RET_DK, RET_DV), lambda i, b: (b, l, 0, 0, 0, 0)),
                  pl.BlockSpec(memory_space=pl.ANY)],
        out_specs=pl.BlockSpec((tq, w), lambda i, b: (qblk0 + b * nq + i, 0)),
        out_shape=jax.ShapeDtypeStruct((N_TOK, w), BF16),
        scratch_shapes=[pltpu.VMEM((RET_HEADS, tq, t), F32)],
        input_output_aliases={6: 0},
        compiler_params=_params("arbitrary", "arbitrary"),
        name="ret_lat",
    )(qk, qk, v, rg, decay, state_ret, prev)


def _rope_tables(dim, reps):
    half = dim // 2
    quarter = half // 2
    t = jnp.arange(DEC_SEQ)
    row = (t // GRID_W).astype(F32)
    col = (t % GRID_W).astype(F32)
    inv = ROPE_BASE ** (-jnp.arange(0, half, 2, dtype=F32) / half)
    zeros = jnp.zeros((DEC_SEQ, quarter), F32)
    cos_parts, sa_parts, sb_parts = [], [], []
    for pos in (row, col):
        ang = pos[:, None] * inv[None, :]
        c, s = jnp.cos(ang), jnp.sin(ang)
        cos_parts += [c, c]
        sa_parts += [zeros, s]
        sb_parts += [-s, zeros]
    padw = LANES - dim

    def build(parts):
        tab = jnp.concatenate(parts + ([jnp.zeros((DEC_SEQ, padw), F32)] if padw else []), axis=-1)
        return jnp.tile(tab, (1, reps))

    return build(cos_parts), build(sa_parts), build(sb_parts)


def kernel(x_prompt, x_sample, c, cache_mla_ckv, cache_mla_krope, state_lru, state_ret, c_ctx, w_mod, b_mod,
           g_norm, w_in, lru_conv_w, lru_conv_b, lru_wa, lru_ba, lru_wx, lru_bx, lru_lam, mla_gq, mla_gkv,
           mla_wuq, mla_wukv, ret_decay, w_br_lru, w_br_mla, w_br_ret, w_out, w_ff1, w_ff2):
    x = jnp.concatenate([x_prompt.reshape(N_CTX, D_MODEL), x_sample.reshape(N_LAT, D_MODEL)], axis=0)
    cond8 = jnp.concatenate([c_ctx[None, :], c, jnp.zeros((8 - 1 - DEC_BATCH, D_MODEL), F32)], axis=0)
    mod = _ada(cond8, w_mod, b_mod).reshape(DEPTH * 8, 1, 6 * D_MODEL)
    g_all = g_norm.reshape(DEPTH * 4, 1, D_MODEL)

    w_in_tail = _realign_tail(w_in)
    wuq = jnp.pad(mla_wuq.reshape(DEPTH, MLA_Q_RANK, MLA_HEADS, MLA_NOPE + MLA_ROPE),
                  ((0, 0), (0, 0), (0, 0), (0, MLA_QH - MLA_NOPE - MLA_ROPE)))
    wuq = wuq.reshape(DEPTH, MLA_Q_RANK, MLA_HEADS * MLA_QH)
    lw = {"conv_w": lru_conv_w, "conv_b": lru_conv_b.reshape(DEPTH, 2, 1, D_RNN), "wa": lru_wa,
          "ba": lru_ba.reshape(DEPTH, 2, 1, D_RNN), "wx": lru_wx, "bx": lru_bx.reshape(DEPTH, 2, 1, D_RNN),
          "lam": lru_lam.reshape(DEPTH, 2, 1, D_RNN)}
    gq = mla_gq.reshape(DEPTH, 1, MLA_Q_RANK)
    gkv = mla_gkv.reshape(DEPTH, 1, MLA_KV_RANK)
    h0_lat = state_lru.reshape(DEC_BATCH, DEPTH, 2, 1, D_RNN)
    cache_ckv_rows = cache_mla_ckv.reshape(DEC_BATCH * DEPTH * PAST_LEN, MLA_KV_RANK)
    tab_mla = _rope_tables(MLA_ROPE, 1)
    tab_ret = _rope_tables(RET_DK, RET_HEADS)
    wq = RET_HEADS * RET_DK

    ckv_out, kr_out, lru_out, ret_out = [], [], [], []
    h, = _resnorm(x, mod, g_all, nxt=(0, 0, 0))
    for l in range(DEPTH):
        proj_a = _mm(h, w_in, l, col0=0, ncols=3072)
        kscale = (wq, RET_DK ** -0.5)
        qk_ctx = _mm(h, w_in_tail, l, col0=0, ncols=2 * wq, out_dtype=BF16, m=N_CTX, kscale=kscale)
        qk_lat = _mm(h, w_in_tail, l, col0=0, ncols=2 * wq, out_dtype=BF16, m=N_LAT, tm=DEC_SEQ,
                     row_map=lambda i: N_CTX // DEC_SEQ + i, kscale=kscale,
                     rope=(tab_ret, RET_DK // 4, 2))
        proj_v = _mm(h, w_in_tail, l, col0=2 * wq, ncols=wq, out_dtype=BF16)
        proj_rg = _mm(h, w_in_tail, l, col0=3 * wq, ncols=wq)
        gates = _mm(h, w_in_tail, l, col0=4 * wq, ncols=3 * D_MODEL, out_dtype=BF16, act="sigmoid")

        y_lru, st_lru = _lru(proj_a, lw, l, T=SEQ, nb=BATCH, blk0=0)
        y_lru, _ = _lru(proj_a, lw, l, T=DEC_SEQ, nb=DEC_BATCH, blk0=N_CTX // DEC_SEQ, h0=h0_lat, prev=y_lru)

        cqn, ckv, ckvb, kr = _mla_prep(proj_a, gq, gkv, l, tab_mla)
        q = _mm(cqn, wuq, l)
        kv = _mm(ckvb, mla_wukv, l, out_dtype=BF16)
        kv_cache = _mm(cache_ckv_rows, mla_wukv, l, out_dtype=BF16, tm=PAST_LEN, m=DEC_BATCH * PAST_LEN,
                       row_map=lambda i, l=l: i * DEPTH + l)
        y_mla = _attn_ctx(q, kv, kr)
        y_mla = _attn_lat(q, kv, kr, kv_cache, cache_mla_krope, l, tab_mla, y_mla)

        y_ret, st_ret = _ret_ctx(qk_ctx, proj_v, proj_rg, ret_decay, l)
        y_ret = _ret_lat(qk_lat, proj_v, proj_rg, ret_decay, state_ret, l, y_ret)

        merged = _branch_merge((y_lru, y_mla, y_ret), (w_br_lru, w_br_mla, w_br_ret), gates, l)
        u = _mm(merged, w_out, l)
        x, h2 = _resnorm(x, mod, g_all, res=(u, l * 4 + 1, l, 2), nxt=(l * 4 + 2, l, 3))
        ff = _mm(h2, w_ff1, l, out_dtype=BF16, act="relu2")
        y = _mm(ff, w_ff2, l, tn=1024, tk=1024)
        if l + 1 < DEPTH:
            x, h = _resnorm(x, mod, g_all, res=(y, l * 4 + 3, l, 5), nxt=((l + 1) * 4, l + 1, 0))
        else:
            x, = _resnorm(x, mod, g_all, res=(y, l * 4 + 3, l, 5))

        ckv_out.append(ckv[:N_CTX].reshape(BATCH, SEQ, MLA_KV_RANK))
        kr_out.append(kr[:N_CTX, :MLA_ROPE].reshape(BATCH, SEQ, MLA_ROPE))
        lru_out.append(st_lru.reshape(BATCH, 2, D_RNN))
        ret_out.append(st_ret)

    y_prompt = x[:N_CTX].reshape(BATCH, SEQ, D_MODEL)
    y_sample = x[N_CTX:].reshape(DEC_BATCH, DEC_SEQ, D_MODEL)
    return (y_prompt, y_sample, jnp.stack(ckv_out, axis=1), jnp.stack(kr_out, axis=1),
            jnp.stack(lru_out, axis=1), jnp.stack(ret_out, axis=1))
```

```python
import functools

import jax
import jax.numpy as jnp
from jax import lax
from jax.experimental import pallas as pl
from jax.experimental.pallas import tpu as pltpu

F32 = jnp.float32
BF16 = jnp.bfloat16

D_MODEL = 2048
BATCH = 16
SEQ = 256
DEPTH = 4
DEC_BATCH = 4
DEC_SEQ = 1024
PAST_LEN = 256
GRID_W = 64
EPS = 1e-6
ROPE_BASE = 10000.0
D_RNN = D_MODEL // 2
LRU_BLOCKS = 8
LRU_BS = D_RNN // LRU_BLOCKS
CONV_W = 4
LRU_C = 8.0
MLA_HEADS = 8
MLA_NOPE = 128
MLA_ROPE = 64
MLA_V = 128
MLA_Q_RANK = D_MODEL // 4
MLA_KV_RANK = D_MODEL // 8
RET_HEADS = 8
RET_DK = 128
RET_DV = 128
D_FF = 4 * D_MODEL

N_CTX = BATCH * SEQ
N_LAT = DEC_BATCH * DEC_SEQ
N_TOK = N_CTX + N_LAT
COL_TAIL = 2 * D_RNN + MLA_Q_RANK + MLA_KV_RANK + MLA_ROPE
MLA_QH = 256

LANES = 128
SUBLANES = 8
VMEM_LIMIT = 56 * 1024 * 1024
ROW_TILE = 256


def _params(*sem):
    return pltpu.CompilerParams(dimension_semantics=sem, vmem_limit_bytes=VMEM_LIMIT)


def _sigmoid(x):
    return 0.5 * (1.0 + jnp.tanh(0.5 * x))


def _softplus(x):
    return jnp.maximum(x, 0.0) + jnp.log(1.0 + jnp.exp(-jnp.abs(x)))


def _rms(x, g):
    return x * lax.rsqrt(jnp.mean(x * x, axis=-1, keepdims=True) + EPS) * g


def _dot(a, b):
    return jnp.dot(a, b, preferred_element_type=F32)


def _dot_nt(a, b):
    return lax.dot_general(a, b, (((1,), (1,)), ((), ())), preferred_element_type=F32)


def _dot_tn(a, b):
    return lax.dot_general(a, b, (((0,), (0,)), ((), ())), preferred_element_type=F32)


def _rope(x, cos, sa, sb, quarter):
    w = x.shape[-1]
    return x * cos + pltpu.roll(x, quarter, 1) * sa + pltpu.roll(x, w - quarter, 1) * sb


def _mod_row(row0):
    return jnp.where(row0 < N_CTX, 0, 1 + (row0 - N_CTX) // DEC_SEQ)


def _ada_kernel(c_ref, w_ref, b_ref, o_ref):
    c = c_ref[...]
    s = c * _sigmoid(c)
    o_ref[...] = _dot(s.astype(BF16), w_ref[...].astype(BF16)) + b_ref[...]


def _ada(cond8, w_mod, b_mod):
    tn = 1024
    n = w_mod.shape[-1]
    return pl.pallas_call(
        _ada_kernel,
        grid=(DEPTH, n // tn),
        in_specs=[pl.BlockSpec((8, D_MODEL), lambda l, j: (0, 0)),
                  pl.BlockSpec((None, D_MODEL, tn), lambda l, j: (l, 0, j)),
                  pl.BlockSpec((None, 1, tn), lambda l, j: (l, 0, j))],
        out_specs=pl.BlockSpec((None, 8, tn), lambda l, j: (l, 0, j)),
        out_shape=jax.ShapeDtypeStruct((DEPTH, 8, n), F32),
        compiler_params=_params("arbitrary", "arbitrary"),
        name="ada",
    )(cond8, w_mod, b_mod.reshape(DEPTH, 1, n))


def _resnorm_kernel(*refs, has_res, has_next):
    refs = list(refs)
    xc_ref, xl_ref = refs[:2]
    refs = refs[2:]
    is_ctx = pl.program_id(0) < N_CTX // ROW_TILE
    x = jnp.where(is_ctx, xc_ref[...], xl_ref[...])
    if has_res:
        u_ref, gpost_ref, gate_ref = refs[:3]
        refs = refs[3:]
        x = x + gate_ref[...] * _rms(u_ref[...], gpost_ref[...])
    if has_next:
        gpre_ref, shift_ref, scale_ref = refs[:3]
        refs = refs[3:]
    if has_res:
        xco_ref, xlo_ref = refs[:2]
        refs = refs[2:]

        @pl.when(is_ctx)
        def _():
            xco_ref[...] = x

        @pl.when(jnp.logical_not(is_ctx))
        def _():
            xlo_ref[...] = x
    if has_next:
        ho_ref = refs.pop(0)
        h = _rms(x, gpre_ref[...]) * (1.0 + scale_ref[...]) + shift_ref[...]
        ho_ref[...] = h.astype(BF16)


def _resnorm(x_ctx, x_lat, mod, g_norm, *, res=None, nxt=None):
    tr = ROW_TILE
    nctx = N_CTX // tr
    row = pl.BlockSpec((tr, D_MODEL), lambda i: (i, 0))
    crow = pl.BlockSpec((tr, D_MODEL), lambda i: (jnp.minimum(i, nctx - 1), 0))
    lrow = pl.BlockSpec((tr, D_MODEL), lambda i: (jnp.maximum(i - nctx, 0), 0))

    def gspec(k):
        return pl.BlockSpec((None, 1, D_MODEL), lambda i: (k, 0, 0))

    def mspec(l, chunk):
        return pl.BlockSpec((None, 1, D_MODEL), lambda i: (l * 8 + _mod_row(i * tr), 0, chunk))

    args, specs, outs, ospecs = [x_ctx, x_lat], [crow, lrow], [], []
    if res is not None:
        u, gk, l, chunk = res
        args += [u, g_norm, mod]
        specs += [row, gspec(gk), mspec(l, chunk)]
        outs += [jax.ShapeDtypeStruct((N_CTX, D_MODEL), F32), jax.ShapeDtypeStruct((N_LAT, D_MODEL), F32)]
        ospecs += [crow, lrow]
    if nxt is not None:
        gk, l, chunk = nxt
        args += [g_norm, mod, mod]
        specs += [gspec(gk), mspec(l, chunk), mspec(l, chunk + 1)]
        outs.append(jax.ShapeDtypeStruct((N_TOK, D_MODEL), BF16))
        ospecs.append(row)
    return pl.pallas_call(
        functools.partial(_resnorm_kernel, has_res=res is not None, has_next=nxt is not None),
        grid=(N_TOK // tr,),
        in_specs=specs, out_specs=ospecs, out_shape=outs,
        compiler_params=_params("arbitrary"),
        name="resnorm",
    )(*args)


def _mm_kernel(x_ref, w_ref, *rest, nk, act, kscale, rope_quarter):
    o_ref = rest[-1]
    if nk == 1:
        part = _dot(x_ref[...].astype(BF16), w_ref[...].astype(BF16))
        if act == "relu2":
            part = jnp.square(jnp.maximum(part, 0.0))
        elif act == "sigmoid":
            part = _sigmoid(part)
        if kscale is not None:
            part = part * jnp.where(pl.program_id(1) >= kscale[0], kscale[1], 1.0)
        if rope_quarter is not None:
            cos_ref, sa_ref, sb_ref = rest[:3]
            part = _rope(part, cos_ref[...], sa_ref[...], sb_ref[...], rope_quarter)
        o_ref[...] = part.astype(o_ref.dtype)
        return

    @pl.when(pl.program_id(2) == 0)
    def _():
        o_ref[...] = jnp.zeros(o_ref.shape, o_ref.dtype)

    o_ref[...] += _dot(x_ref[...].astype(BF16), w_ref[...].astype(BF16))


def _mm(x, w, l, *, col0=0, ncols=None, out_dtype=F32, act=None, tm=2048, tn=512, tk=None,
        m=None, row_map=None, kscale=None, rope=None):
    kdim = w.shape[1]
    ncols = w.shape[2] - col0 if ncols is None else ncols
    m = x.shape[0] if m is None else m
    tk = kdim if tk is None else tk
    tm, tn = min(tm, m), min(tn, ncols)
    assert m % tm == 0 and ncols % tn == 0 and kdim % tk == 0 and col0 % tn == 0
    nk = kdim // tk
    assert nk == 1 or (act is None and out_dtype == F32 and kscale is None and rope is None)
    jb = col0 // tn
    row_map = (lambda i: i) if row_map is None else row_map
    args = [x, w]
    in_specs = [pl.BlockSpec((tm, tk), lambda i, j, k: (row_map(i), k)),
                pl.BlockSpec((None, tk, tn), lambda i, j, k: (l, k, jb + j))]
    if kscale is not None:
        assert kscale[0] % tn == 0
        kscale = (kscale[0] // tn, kscale[1])
    if rope is not None:
        tabs, quarter, ntab = rope
        assert tm == DEC_SEQ and tabs[0].shape == (DEC_SEQ, ntab * tn)
        args += list(tabs)
        in_specs += [pl.BlockSpec((tm, tn), lambda i, j, k: (0, j % ntab))] * 3
    return pl.pallas_call(
        functools.partial(_mm_kernel, nk=nk, act=act, kscale=kscale,
                          rope_quarter=None if rope is None else rope[1]),
        grid=(m // tm, ncols // tn, nk),
        in_specs=in_specs,
        out_specs=pl.BlockSpec((tm, tn), lambda i, j, k: (i, j)),
        out_shape=jax.ShapeDtypeStruct((m, ncols), out_dtype),
        compiler_params=_params("arbitrary", "arbitrary", "arbitrary"),
        name="mm",
    )(*args)


TAIL_BASE = COL_TAIL - LANES // 2
TAIL_COLS = 4 * RET_HEADS * RET_DK + 3 * D_MODEL
REALIGN_TN = 256


def _realign_kernel(w1_ref, w2_ref, o_ref):
    half = LANES // 2
    o_ref[...] = jnp.concatenate([w1_ref[:, half:], w2_ref[:, :half]], axis=1).astype(BF16)


def _realign_tail(w_in):
    tn = REALIGN_TN
    assert TAIL_BASE % tn == 0 and TAIL_COLS % tn == 0
    j0 = TAIL_BASE // tn
    per = tn // LANES
    return pl.pallas_call(
        _realign_kernel,
        grid=(DEPTH, TAIL_COLS // tn),
        in_specs=[pl.BlockSpec((None, D_MODEL, tn), lambda l, j: (l, 0, j0 + j)),
                  pl.BlockSpec((None, D_MODEL, LANES), lambda l, j: (l, 0, (j0 + j + 1) * per))],
        out_specs=pl.BlockSpec((None, D_MODEL, tn), lambda l, j: (l, 0, j)),
        out_shape=jax.ShapeDtypeStruct((DEPTH, D_MODEL, TAIL_COLS), BF16),
        compiler_params=_params("arbitrary", "arbitrary"),
        name="realign_tail",
    )(w_in, w_in)


def _bm_kernel(a0_ref, a1_ref, a2_ref, w0_ref, w1_ref, w2_ref, g0_ref, g1_ref, g2_ref, o_ref):
    acc = g0_ref[...].astype(F32) * _dot(a0_ref[...], w0_ref[...].astype(BF16))
    acc = acc + g1_ref[...].astype(F32) * _dot(a1_ref[...], w1_ref[...].astype(BF16))
    acc = acc + g2_ref[...].astype(F32) * _dot(a2_ref[...], w2_ref[...].astype(BF16))
    o_ref[...] = acc.astype(BF16)


def _branch_merge(acts, ws, gates, l, *, tm=1024, tn=512):
    nj = D_MODEL // tn
    kdim = acts[0].shape[1]
    aspec = pl.BlockSpec((tm, kdim), lambda i, j: (i, 0))
    wspec = pl.BlockSpec((None, kdim, tn), lambda i, j: (l, 0, j))
    gspecs = [pl.BlockSpec((tm, tn), lambda i, j, b=b: (i, b * nj + j)) for b in range(3)]
    return pl.pallas_call(
        _bm_kernel,
        grid=(N_TOK // tm, nj),
        in_specs=[aspec] * 3 + [wspec] * 3 + gspecs,
        out_specs=pl.BlockSpec((tm, tn), lambda i, j: (i, j)),
        out_shape=jax.ShapeDtypeStruct((N_TOK, D_MODEL), BF16),
        compiler_params=_params("arbitrary", "arbitrary"),
        name="branch_merge",
    )(*acts, *ws, gates, gates, gates)


LRU_CT = 512


def _lru_kernel(*refs, T, has_h0, n_alias):
    xr_ref, xg_ref, cw_ref, cb_ref, wa_ref, ba_ref, wx_ref, bx_ref, lam_ref = refs[:9]
    refs = refs[9:]
    if has_h0:
        h0_ref = refs[0]
        refs = refs[1:]
    y_ref, st_ref, xpad_ref, a_ref, b_ref, h_ref = refs[n_alias:]
    ct = xr_ref.shape[1]
    groups = T // SUBLANES
    pad = SUBLANES
    xpad_ref[0:pad, :] = jnp.zeros((pad, ct), F32)
    xpad_ref[pad + T:2 * pad + T, :] = jnp.zeros((pad, ct), F32)
    xpad_ref[pad:pad + T, :] = xr_ref[...]
    row8 = lax.broadcasted_iota(jnp.int32, (SUBLANES, ct), 0)

    for d in range(2):
        for n in range(ct // LRU_BS):
            cs = slice(n * LRU_BS, (n + 1) * LRU_BS)
            xc = jnp.zeros((T, LRU_BS), F32) + cb_ref[d, :, cs]
            for j in range(CONV_W):
                off = (j - (CONV_W - 1)) if d == 0 else ((CONV_W - 1) - j)
                xc = xc + cw_ref[d, j:j + 1, cs] * xpad_ref[pad + off:pad + off + T, cs]
            xcb = xc.astype(BF16)
            r = _sigmoid(_dot(xcb, wa_ref[d, n].astype(BF16)) + ba_ref[d, :, cs])
            i = _sigmoid(_dot(xcb, wx_ref[d, n].astype(BF16)) + bx_ref[d, :, cs])
            log_a = (-LRU_C) * r * _softplus(-lam_ref[d, :, cs])
            a = jnp.exp(log_a)
            a_ref[:, cs] = a
            b_ref[:, cs] = jnp.sqrt(1.0 - a * a) * (i * xc)

        h0 = h0_ref[d] if has_h0 else jnp.zeros((1, ct), F32)

        def body(it, carry, d=d):
            g = it if d == 0 else groups - 1 - it
            r0 = pl.multiple_of(g * SUBLANES, SUBLANES)
            av = a_ref[pl.ds(r0, SUBLANES), :]
            bv = b_ref[pl.ds(r0, SUBLANES), :]
            for s in (1, 2, 4):
                if d == 0:
                    ok = row8 >= s
                    shift = s
                else:
                    ok = row8 < SUBLANES - s
                    shift = SUBLANES - s
                a_sh = jnp.where(ok, pltpu.roll(av, shift, 0), 1.0)
                b_sh = jnp.where(ok, pltpu.roll(bv, shift, 0), 0.0)
                bv = av * b_sh + bv
                av = av * a_sh
            h = av * carry + bv
            if d == 0:
                h_ref[pl.ds(r0, SUBLANES), :] = h
                return h[SUBLANES - 1:SUBLANES, :]
            h_ref[pl.ds(r0, SUBLANES), :] = h_ref[pl.ds(r0, SUBLANES), :] + h
            return h[0:1, :]

        last = lax.fori_loop(0, groups, body, h0, unroll=4)
        st_ref[d:d + 1, :] = last

    xg = xg_ref[...]
    gelu = 0.5 * xg * (1.0 + jnp.tanh(0.7978845608028654 * (xg + 0.044715 * (xg * xg * xg))))
    y_ref[...] = (h_ref[...] * gelu).astype(BF16)


def _lru(proj_a, lw, l, *, T, nb, blk0, h0=None, prev=None, st_prev=None):
    ct = LRU_CT
    ncb = D_RNN // ct
    bpc = ct // LRU_BS
    in_specs = [
        pl.BlockSpec((T, ct), lambda b, c: (blk0 + b, c)),
        pl.BlockSpec((T, ct), lambda b, c: (blk0 + b, ncb + c)),
        pl.BlockSpec((None, 2, CONV_W, ct), lambda b, c: (l, 0, 0, c)),
        pl.BlockSpec((None, 2, 1, ct), lambda b, c: (l, 0, 0, c)),
        pl.BlockSpec((None, 2, bpc, LRU_BS, LRU_BS), lambda b, c: (l, 0, c, 0, 0)),
        pl.BlockSpec((None, 2, 1, ct), lambda b, c: (l, 0, 0, c)),
        pl.BlockSpec((None, 2, bpc, LRU_BS, LRU_BS), lambda b, c: (l, 0, c, 0, 0)),
        pl.BlockSpec((None, 2, 1, ct), lambda b, c: (l, 0, 0, c)),
        pl.BlockSpec((None, 2, 1, ct), lambda b, c: (l, 0, 0, c)),
    ]
    args = [proj_a, proj_a, lw["conv_w"], lw["conv_b"], lw["wa"], lw["ba"], lw["wx"], lw["bx"], lw["lam"]]
    if h0 is not None:
        in_specs.append(pl.BlockSpec((None, None, 2, 1, ct), lambda b, c: (b, l, 0, 0, c)))
        args.append(h0)
    aliases = {}
    for buf, out_idx in ((prev, 0), (st_prev, 1)):
        if buf is not None:
            aliases[len(args)] = out_idx
            in_specs.append(pl.BlockSpec(memory_space=pl.ANY))
            args.append(buf)
    return pl.pallas_call(
        functools.partial(_lru_kernel, T=T, has_h0=h0 is not None, n_alias=len(aliases)),
        grid=(nb, ncb),
        in_specs=in_specs,
        out_specs=[pl.BlockSpec((T, ct), lambda b, c: (blk0 + b, c)),
                   pl.BlockSpec((None, None, 2, ct), lambda b, c: (b, l, 0, c))],
        out_shape=[jax.ShapeDtypeStruct((N_TOK, D_RNN), BF16),
                   jax.ShapeDtypeStruct((nb, DEPTH, 2, D_RNN), F32)],
        input_output_aliases=aliases,
        scratch_shapes=[pltpu.VMEM((T + 2 * SUBLANES, ct), F32), pltpu.VMEM((T, ct), F32),
                        pltpu.VMEM((T, ct), F32), pltpu.VMEM((T, ct), F32)],
        compiler_params=_params("arbitrary", "arbitrary"),
        name="lru",
    )(*args)


def _mla_prep_kernel(p_ref, gq_ref, gkv_ref, cos_ref, sa_ref, sb_ref, *rest):
    cq_ref, ckvb_ref, kr_ref, ckv_out_ref, kr_out_ref = rest[-5:]
    i = pl.program_id(0)
    cq_ref[...] = _rms(p_ref[:, 0:MLA_Q_RANK], gq_ref[...]).astype(BF16)
    ckv = _rms(p_ref[:, MLA_Q_RANK:MLA_Q_RANK + MLA_KV_RANK], gkv_ref[...])
    ckvb_ref[...] = ckv.astype(BF16)
    k0 = MLA_Q_RANK + MLA_KV_RANK
    kr = p_ref[:, k0:k0 + LANES]

    @pl.when(i < N_CTX // ROW_TILE)
    def _():
        kr_ref[...] = kr
        ckv_out_ref[...] = ckv
        kr_out_ref[...] = kr[:, 0:MLA_ROPE]

    @pl.when(i >= N_CTX // ROW_TILE)
    def _():
        kr_ref[...] = _rope(kr, cos_ref[...], sa_ref[...], sb_ref[...], MLA_ROPE // 4)


def _lat_tile(i):
    return (jnp.maximum(i - N_CTX // ROW_TILE, 0)) % (DEC_SEQ // ROW_TILE)


def _mla_prep(proj_a, gq, gkv, l, tab, caches=None):
    tr = ROW_TILE
    assert tr == SEQ
    last_ctx = BATCH - 1
    tspec = pl.BlockSpec((tr, LANES), lambda i: (_lat_tile(i), 0))
    args = [proj_a, gq, gkv, *tab]
    in_specs = [pl.BlockSpec((tr, 1024), lambda i: (i, 2)),
                pl.BlockSpec((None, 1, MLA_Q_RANK), lambda i: (l, 0, 0)),
                pl.BlockSpec((None, 1, MLA_KV_RANK), lambda i: (l, 0, 0)),
                tspec, tspec, tspec]
    aliases = {}
    if caches is not None:
        aliases = {len(args): 3, len(args) + 1: 4}
        args += list(caches)
        in_specs += [pl.BlockSpec(memory_space=pl.ANY)] * 2
    return pl.pallas_call(
        _mla_prep_kernel,
        grid=(N_TOK // tr,),
        in_specs=in_specs,
        out_specs=[pl.BlockSpec((tr, MLA_Q_RANK), lambda i: (i, 0)),
                   pl.BlockSpec((tr, MLA_KV_RANK), lambda i: (i, 0)),
                   pl.BlockSpec((tr, LANES), lambda i: (i, 0)),
                   pl.BlockSpec((None, None, SEQ, MLA_KV_RANK), lambda i: (jnp.minimum(i, last_ctx), l, 0, 0)),
                   pl.BlockSpec((None, None, SEQ, MLA_ROPE), lambda i: (jnp.minimum(i, last_ctx), l, 0, 0))],
        out_shape=[jax.ShapeDtypeStruct((N_TOK, MLA_Q_RANK), BF16),
                   jax.ShapeDtypeStruct((N_TOK, MLA_KV_RANK), BF16),
                   jax.ShapeDtypeStruct((N_TOK, LANES), F32),
                   jax.ShapeDtypeStruct((BATCH, DEPTH, SEQ, MLA_KV_RANK), F32),
                   jax.ShapeDtypeStruct((BATCH, DEPTH, SEQ, MLA_ROPE), F32)],
        input_output_aliases=aliases,
        compiler_params=_params("arbitrary"),
        name="mla_prep",
    )(*args)


def _attn_kernel(*refs, lat):
    if lat:
        q_ref, kvn_ref, krn_ref, kvc_ref, krc_ref, cos_ref, sa_ref, sb_ref, _, o_ref = refs
    else:
        q_ref, kvn_ref, krn_ref, o_ref = refs
    scale = (MLA_NOPE + MLA_ROPE) ** -0.5
    krn = krn_ref[:, 0:MLA_ROPE].astype(BF16)
    if lat:
        krc = krc_ref[...].astype(BF16)
        kvc = kvc_ref[...].astype(BF16)
    for h in range(MLA_HEADS):
        c0 = h * MLA_QH
        qn = q_ref[:, c0:c0 + MLA_NOPE].astype(BF16)
        qr = q_ref[:, c0 + MLA_NOPE:c0 + MLA_QH]
        if lat:
            qr = _rope(qr, cos_ref[...], sa_ref[...], sb_ref[...], MLA_ROPE // 4)
        qr = qr[:, 0:MLA_ROPE].astype(BF16)
        kn = kvn_ref[:, c0:c0 + MLA_NOPE]
        v = kvn_ref[:, c0 + MLA_NOPE:c0 + MLA_QH]
        s2 = _dot_nt(qn, kn) + _dot_nt(qr, krn)
        m = jnp.max(s2, axis=-1, keepdims=True)
        if lat:
            s1 = _dot_nt(qn, kvc[:, c0:c0 + MLA_NOPE]) + _dot_nt(qr, krc)
            m = jnp.maximum(m, jnp.max(s1, axis=-1, keepdims=True))
            p1 = jnp.exp((s1 - m) * scale)
        p2 = jnp.exp((s2 - m) * scale)
        den = jnp.sum(p2, axis=-1, keepdims=True)
        o = _dot(p2.astype(BF16), v)
        if lat:
            den = den + jnp.sum(p1, axis=-1, keepdims=True)
            o = o + _dot(p1.astype(BF16), kvc[:, c0 + MLA_NOPE:c0 + MLA_QH])
        o_ref[:, h * MLA_V:(h + 1) * MLA_V] = (o / den).astype(BF16)


def _attn_ctx(q, kv, kr):
    t = SEQ
    return pl.pallas_call(
        functools.partial(_attn_kernel, lat=False),
        grid=(BATCH,),
        in_specs=[pl.BlockSpec((t, MLA_HEADS * MLA_QH), lambda b: (b, 0)),
                  pl.BlockSpec((t, MLA_HEADS * MLA_QH), lambda b: (b, 0)),
                  pl.BlockSpec((t, LANES), lambda b: (b, 0))],
        out_specs=pl.BlockSpec((t, MLA_HEADS * MLA_V), lambda b: (b, 0)),
        out_shape=jax.ShapeDtypeStruct((N_TOK, MLA_HEADS * MLA_V), BF16),
        compiler_params=_params("arbitrary"),
        name="attn_ctx",
    )(q, kv, kr)


def _attn_lat(q, kv, kr, kv_cache, cache_krope, l, tab, prev):
    tq = ROW_TILE
    nq = DEC_SEQ // tq
    blk0 = N_CTX // DEC_SEQ
    qblk0 = N_CTX // tq
    tspec = pl.BlockSpec((tq, LANES), lambda b, i: (i, 0))
    return pl.pallas_call(
        functools.partial(_attn_kernel, lat=True),
        grid=(DEC_BATCH, nq),
        in_specs=[pl.BlockSpec((tq, MLA_HEADS * MLA_QH), lambda b, i: (qblk0 + b * nq + i, 0)),
                  pl.BlockSpec((DEC_SEQ, MLA_HEADS * MLA_QH), lambda b, i: (blk0 + b, 0)),
                  pl.BlockSpec((DEC_SEQ, LANES), lambda b, i: (blk0 + b, 0)),
                  pl.BlockSpec((PAST_LEN, MLA_HEADS * MLA_QH), lambda b, i: (b, 0)),
                  pl.BlockSpec((None, None, PAST_LEN, MLA_ROPE), lambda b, i: (b, l, 0, 0)),
                  tspec, tspec, tspec, pl.BlockSpec(memory_space=pl.ANY)],
        out_specs=pl.BlockSpec((tq, MLA_HEADS * MLA_V), lambda b, i: (qblk0 + b * nq + i, 0)),
        out_shape=jax.ShapeDtypeStruct((N_TOK, MLA_HEADS * MLA_V), BF16),
        input_output_aliases={8: 0},
        compiler_params=_params("arbitrary", "arbitrary"),
        name="attn_lat",
    )(q, kv, kr, kv_cache, cache_krope, *tab, prev)


def _log_sigmoid(x):
    return -_softplus(-x)


def _ret_kernel(*refs, lat, T):
    if lat:
        q_ref, k_ref, v_ref, rg_ref, dec_ref, s0_ref, _, o_ref, decay_ref = refs
    else:
        q_ref, k_ref, v_ref, rg_ref, dec_ref = refs[:5]
        o_ref, st_ref, decay_ref = refs[-3:]
    tq = q_ref.shape[0]
    t0 = pl.program_id(0) * tq if lat else 0
    batch_id = pl.program_id(1) if lat else pl.program_id(0)
    lg = _log_sigmoid(dec_ref[...])
    tcol = (t0 + lax.broadcasted_iota(jnp.int32, (tq, 1), 0)).astype(F32)

    @pl.when(batch_id == 0)
    def _():
        rows = (t0 + lax.broadcasted_iota(jnp.int32, (tq, T), 0)).astype(F32)
        cols = lax.broadcasted_iota(jnp.int32, (tq, T), 1).astype(F32)
        diff = rows - cols
        fwd = diff >= 0.0
        bwd = diff <= 0.0
        dpos = jnp.maximum(diff, 0.0)
        dneg = jnp.maximum(-diff, 0.0)
        for h in range(RET_HEADS):
            decay_ref[h] = (jnp.where(fwd, jnp.exp(lg[0:1, h:h + 1] * dpos), 0.0)
                            + jnp.where(bwd, jnp.exp(lg[1:2, h:h + 1] * dneg), 0.0))

    for h in range(RET_HEADS):
        cs = slice(h * RET_DK, (h + 1) * RET_DK)
        lgf = lg[0:1, h:h + 1]
        lgb = lg[1:2, h:h + 1]
        q = q_ref[:, cs]
        k = k_ref[:, cs]
        v = v_ref[:, cs]
        sc = _dot_nt(q, k) * decay_ref[h]
        o = _dot(sc.astype(BF16), v)
        if lat:
            qf = q.astype(F32)
            o = o + _dot((qf * jnp.exp(lgf * (tcol + 1.0))).astype(BF16), s0_ref[0, h].astype(BF16))
            o = o + _dot((qf * jnp.exp(lgb * (T - tcol))).astype(BF16), s0_ref[1, h].astype(BF16))
        else:
            kf = k.astype(F32)
            st_ref[0, h] = _dot_tn((kf * jnp.exp(lgf * (T - 1.0 - tcol))).astype(BF16), v)
            st_ref[1, h] = _dot_tn((kf * jnp.exp(lgb * tcol)).astype(BF16), v)
        mu = jnp.mean(o, axis=-1, keepdims=True)
        oc = o - mu
        var = jnp.mean(oc * oc, axis=-1, keepdims=True)
        rg = rg_ref[:, cs]
        o_ref[:, cs] = (oc * lax.rsqrt(var + EPS) * (rg * _sigmoid(rg))).astype(BF16)


def _ret_ctx(qk, v, rg, decay, l, st_prev=None):
    t = SEQ
    w = RET_HEADS * RET_DK
    args = [qk, qk, v, rg, decay]
    in_specs = [pl.BlockSpec((t, w), lambda b: (b, 0)),
                pl.BlockSpec((t, w), lambda b: (b, 1)),
                pl.BlockSpec((t, w), lambda b: (b, 0)),
                pl.BlockSpec((t, w), lambda b: (b, 0)),
                pl.BlockSpec((None, 2, RET_HEADS), lambda b: (l, 0, 0))]
    aliases = {}
    if st_prev is not None:
        aliases = {len(args): 1}
        args.append(st_prev)
        in_specs.append(pl.BlockSpec(memory_space=pl.ANY))
    return pl.pallas_call(
        functools.partial(_ret_kernel, lat=False, T=t),
        grid=(BATCH,),
        in_specs=in_specs,
        out_specs=[pl.BlockSpec((t, w), lambda b: (b, 0)),
                   pl.BlockSpec((None, None, 2, RET_HEADS, RET_DK, RET_DV), lambda b: (b, l, 0, 0, 0, 0))],
        out_shape=[jax.ShapeDtypeStruct((N_TOK, w), BF16),
                   jax.ShapeDtypeStruct((BATCH, DEPTH, 2, RET_HEADS, RET_DK, RET_DV), F32)],
        input_output_aliases=aliases,
        scratch_shapes=[pltpu.VMEM((RET_HEADS, t, t), F32)],
        compiler_params=_params("arbitrary"),
        name="ret_ctx",
    )(*args)


def _ret_lat(qk, v, rg, decay, state_ret, l, prev):
    tq = ROW_TILE
    t = DEC_SEQ
    nq = t // tq
    w = RET_HEADS * RET_DK
    blk0 = N_CTX // t
    qblk0 = N_CTX // tq
    return pl.pallas_call(
        functools.partial(_ret_kernel, lat=True, T=t),
        grid=(nq, DEC_BATCH),
        in_specs=[pl.BlockSpec((tq, w), lambda i, b: (b * nq + i, 0)),
                  pl.BlockSpec((t, w), lambda i, b: (b, 1)),
                  pl.BlockSpec((t, w), lambda i, b: (blk0 + b, 0)),
                  pl.BlockSpec((tq, w), lambda i, b: (qblk0 + b * nq + i, 0)),
                  pl.BlockSpec((None, 2, RET_HEADS), lambda i, b: (l, 0, 0)),
                  pl.BlockSpec((None, None, 2, RET_HEADS, RET_DK, RET_DV), lambda i, b: (b, l, 0, 0, 0, 0)),
                  pl.BlockSpec(memory_space=pl.ANY)],
        out_specs=pl.BlockSpec((tq, w), lambda i, b: (qblk0 + b * nq + i, 0)),
        out_shape=jax.ShapeDtypeStruct((N_TOK, w), BF16),
        scratch_shapes=[pltpu.VMEM((RET_HEADS, tq, t), F32)],
        input_output_aliases={6: 0},
        compiler_params=_params("arbitrary", "arbitrary"),
        name="ret_lat",
    )(qk, qk, v, rg, decay, state_ret, prev)


def _rope_tables(dim, reps):
    half = dim // 2
    quarter = half // 2
    t = jnp.arange(DEC_SEQ)
    row = (t // GRID_W).astype(F32)
    col = (t % GRID_W).astype(F32)
    inv = ROPE_BASE ** (-jnp.arange(0, half, 2, dtype=F32) / half)
    zeros = jnp.zeros((DEC_SEQ, quarter), F32)
    cos_parts, sa_parts, sb_parts = [], [], []
    for pos in (row, col):
        ang = pos[:, None] * inv[None, :]
        c, s = jnp.cos(ang), jnp.sin(ang)
        cos_parts += [c, c]
        sa_parts += [zeros, s]
        sb_parts += [-s, zeros]
    padw = LANES - dim

    def build(parts):
        tab = jnp.concatenate(parts + ([jnp.zeros((DEC_SEQ, padw), F32)] if padw else []), axis=-1)
        return jnp.tile(tab, (1, reps))

    return build(cos_parts), build(sa_parts), build(sb_parts)


def kernel(x_prompt, x_sample, c, cache_mla_ckv, cache_mla_krope, state_lru, state_ret, c_ctx, w_mod, b_mod,
           g_norm, w_in, lru_conv_w, lru_conv_b, lru_wa, lru_ba, lru_wx, lru_bx, lru_lam, mla_gq, mla_gkv,
           mla_wuq, mla_wukv, ret_decay, w_br_lru, w_br_mla, w_br_ret, w_out, w_ff1, w_ff2):
    x_ctx = x_prompt.reshape(N_CTX, D_MODEL)
    x_lat = x_sample.reshape(N_LAT, D_MODEL)
    cond8 = jnp.concatenate([c_ctx[None, :], c, jnp.zeros((8 - 1 - DEC_BATCH, D_MODEL), F32)], axis=0)
    mod = _ada(cond8, w_mod, b_mod).reshape(DEPTH * 8, 1, 6 * D_MODEL)
    g_all = g_norm.reshape(DEPTH * 4, 1, D_MODEL)

    w_in_tail = _realign_tail(w_in)
    wuq = jnp.pad(mla_wuq.reshape(DEPTH, MLA_Q_RANK, MLA_HEADS, MLA_NOPE + MLA_ROPE),
                  ((0, 0), (0, 0), (0, 0), (0, MLA_QH - MLA_NOPE - MLA_ROPE)))
    wuq = wuq.reshape(DEPTH, MLA_Q_RANK, MLA_HEADS * MLA_QH)
    lw = {"conv_w": lru_conv_w, "conv_b": lru_conv_b.reshape(DEPTH, 2, 1, D_RNN), "wa": lru_wa,
          "ba": lru_ba.reshape(DEPTH, 2, 1, D_RNN), "wx": lru_wx, "bx": lru_bx.reshape(DEPTH, 2, 1, D_RNN),
          "lam": lru_lam.reshape(DEPTH, 2, 1, D_RNN)}
    gq = mla_gq.reshape(DEPTH, 1, MLA_Q_RANK)
    gkv = mla_gkv.reshape(DEPTH, 1, MLA_KV_RANK)
    h0_lat = state_lru.reshape(DEC_BATCH, DEPTH, 2, 1, D_RNN)
    cache_ckv_rows = cache_mla_ckv.reshape(DEC_BATCH * DEPTH * PAST_LEN, MLA_KV_RANK)
    tab_mla = _rope_tables(MLA_ROPE, 1)
    tab_ret = _rope_tables(RET_DK, RET_HEADS)
    wq = RET_HEADS * RET_DK

    caches = st_lru = st_ret = None
    h, = _resnorm(x_ctx, x_lat, mod, g_all, nxt=(0, 0, 0))
    for l in range(DEPTH):
        proj_a = _mm(h, w_in, l, col0=0, ncols=3072)
        kscale = (wq, RET_DK ** -0.5)
        qk_ctx = _mm(h, w_in_tail, l, col0=0, ncols=2 * wq, out_dtype=BF16, m=N_CTX, kscale=kscale)
        qk_lat = _mm(h, w_in_tail, l, col0=0, ncols=2 * wq, out_dtype=BF16, m=N_LAT, tm=DEC_SEQ,
                     row_map=lambda i: N_CTX // DEC_SEQ + i, kscale=kscale,
                     rope=(tab_ret, RET_DK // 4, 2))
        proj_v = _mm(h, w_in_tail, l, col0=2 * wq, ncols=wq, out_dtype=BF16)
        proj_rg = _mm(h, w_in_tail, l, col0=3 * wq, ncols=wq)
        gates = _mm(h, w_in_tail, l, col0=4 * wq, ncols=3 * D_MODEL, out_dtype=BF16, act="sigmoid")

        y_lru, st_lru = _lru(proj_a, lw, l, T=SEQ, nb=BATCH, blk0=0, st_prev=st_lru)
        y_lru, _ = _lru(proj_a, lw, l, T=DEC_SEQ, nb=DEC_BATCH, blk0=N_CTX // DEC_SEQ, h0=h0_lat, prev=y_lru)

        cqn, ckvb, kr, *caches = _mla_prep(proj_a, gq, gkv, l, tab_mla, caches)
        q = _mm(cqn, wuq, l)
        kv = _mm(ckvb, mla_wukv, l, out_dtype=BF16)
        kv_cache = _mm(cache_ckv_rows, mla_wukv, l, out_dtype=BF16, tm=PAST_LEN, m=DEC_BATCH * PAST_LEN,
                       row_map=lambda i, l=l: i * DEPTH + l)
        y_mla = _attn_ctx(q, kv, kr)
        y_mla = _attn_lat(q, kv, kr, kv_cache, cache_mla_krope, l, tab_mla, y_mla)

        y_ret, st_ret = _ret_ctx(qk_ctx, proj_v, proj_rg, ret_decay, l, st_ret)
        y_ret = _ret_lat(qk_lat, proj_v, proj_rg, ret_decay, state_ret, l, y_ret)

        merged = _branch_merge((y_lru, y_mla, y_ret), (w_br_lru, w_br_mla, w_br_ret), gates, l)
        u = _mm(merged, w_out, l)
        x_ctx, x_lat, h2 = _resnorm(x_ctx, x_lat, mod, g_all, res=(u, l * 4 + 1, l, 2), nxt=(l * 4 + 2, l, 3))
        ff = _mm(h2, w_ff1, l, out_dtype=BF16, act="relu2")
        y = _mm(ff, w_ff2, l, tn=1024, tk=1024)
        if l + 1 < DEPTH:
            x_ctx, x_lat, h = _resnorm(x_ctx, x_lat, mod, g_all, res=(y, l * 4 + 3, l, 5),
                                       nxt=((l + 1) * 4, l + 1, 0))
        else:
            x_ctx, x_lat = _resnorm(x_ctx, x_lat, mod, g_all, res=(y, l * 4 + 3, l, 5))

    new_ckv, new_krope = caches
    return (x_ctx.reshape(BATCH, SEQ, D_MODEL), x_lat.reshape(DEC_BATCH, DEC_SEQ, D_MODEL),
            new_ckv, new_krope, st_lru, st_ret)
```

```python
import functools

import jax
import jax.numpy as jnp
from jax import lax
from jax.experimental import pallas as pl
from jax.experimental.pallas import tpu as pltpu

F32 = jnp.float32
BF16 = jnp.bfloat16

D_MODEL = 2048
BATCH = 16
SEQ = 256
DEPTH = 4
DEC_BATCH = 4
DEC_SEQ = 1024
PAST_LEN = 256
GRID_W = 64
EPS = 1e-6
ROPE_BASE = 10000.0
D_RNN = D_MODEL // 2
LRU_BLOCKS = 8
LRU_BS = D_RNN // LRU_BLOCKS
CONV_W = 4
LRU_C = 8.0
MLA_HEADS = 8
MLA_NOPE = 128
MLA_ROPE = 64
MLA_V = 128
MLA_Q_RANK = D_MODEL // 4
MLA_KV_RANK = D_MODEL // 8
RET_HEADS = 8
RET_DK = 128
RET_DV = 128
D_FF = 4 * D_MODEL

N_CTX = BATCH * SEQ
N_LAT = DEC_BATCH * DEC_SEQ
N_TOK = N_CTX + N_LAT
COL_TAIL = 2 * D_RNN + MLA_Q_RANK + MLA_KV_RANK + MLA_ROPE
MLA_QH = 256

LANES = 128
SUBLANES = 8
VMEM_LIMIT = 56 * 1024 * 1024
ROW_TILE = 256


def _params(*sem):
    return pltpu.CompilerParams(dimension_semantics=sem, vmem_limit_bytes=VMEM_LIMIT)


def _sigmoid(x):
    return 0.5 * (1.0 + jnp.tanh(0.5 * x))


def _softplus(x):
    return jnp.maximum(x, 0.0) + jnp.log(1.0 + jnp.exp(-jnp.abs(x)))


def _rms(x, g):
    return x * lax.rsqrt(jnp.mean(x * x, axis=-1, keepdims=True) + EPS) * g


def _dot(a, b):
    return jnp.dot(a, b, preferred_element_type=F32)


def _dot_nt(a, b):
    return lax.dot_general(a, b, (((1,), (1,)), ((), ())), preferred_element_type=F32)


def _dot_tn(a, b):
    return lax.dot_general(a, b, (((0,), (0,)), ((), ())), preferred_element_type=F32)


def _rope(x, cos, sa, sb, quarter):
    w = x.shape[-1]
    return x * cos + pltpu.roll(x, quarter, 1) * sa + pltpu.roll(x, w - quarter, 1) * sb


def _mod_row(row0):
    return jnp.where(row0 < N_CTX, 0, 1 + (row0 - N_CTX) // DEC_SEQ)


def _ada_kernel(c_ref, w_ref, b_ref, o_ref):
    c = c_ref[...]
    s = c * _sigmoid(c)
    o_ref[...] = _dot(s.astype(BF16), w_ref[...].astype(BF16)) + b_ref[...]


def _ada(cond8, w_mod, b_mod):
    tn = 1024
    n = w_mod.shape[-1]
    return pl.pallas_call(
        _ada_kernel,
        grid=(DEPTH, n // tn),
        in_specs=[pl.BlockSpec((8, D_MODEL), lambda l, j: (0, 0)),
                  pl.BlockSpec((None, D_MODEL, tn), lambda l, j: (l, 0, j)),
                  pl.BlockSpec((None, 1, tn), lambda l, j: (l, 0, j))],
        out_specs=pl.BlockSpec((None, 8, tn), lambda l, j: (l, 0, j)),
        out_shape=jax.ShapeDtypeStruct((DEPTH, 8, n), F32),
        compiler_params=_params("arbitrary", "arbitrary"),
        name="ada",
    )(cond8, w_mod, b_mod.reshape(DEPTH, 1, n))


def _resnorm_kernel(*refs, has_res, has_next):
    refs = list(refs)
    xc_ref, xl_ref = refs[:2]
    refs = refs[2:]
    is_ctx = pl.program_id(0) < N_CTX // ROW_TILE
    x = jnp.where(is_ctx, xc_ref[...], xl_ref[...])
    if has_res:
        u_ref, gpost_ref, gate_ref = refs[:3]
        refs = refs[3:]
        x = x + gate_ref[...] * _rms(u_ref[...], gpost_ref[...])
    if has_next:
        gpre_ref, shift_ref, scale_ref = refs[:3]
        refs = refs[3:]
    if has_res:
        xco_ref, xlo_ref = refs[:2]
        refs = refs[2:]

        @pl.when(is_ctx)
        def _():
            xco_ref[...] = x

        @pl.when(jnp.logical_not(is_ctx))
        def _():
            xlo_ref[...] = x
    if has_next:
        ho_ref = refs.pop(0)
        h = _rms(x, gpre_ref[...]) * (1.0 + scale_ref[...]) + shift_ref[...]
        ho_ref[...] = h.astype(BF16)


def _resnorm(x_ctx, x_lat, mod, g_norm, *, res=None, nxt=None):
    tr = ROW_TILE
    nctx = N_CTX // tr
    row = pl.BlockSpec((tr, D_MODEL), lambda i: (i, 0))
    crow = pl.BlockSpec((tr, D_MODEL), lambda i: (jnp.minimum(i, nctx - 1), 0))
    lrow = pl.BlockSpec((tr, D_MODEL), lambda i: (jnp.maximum(i - nctx, 0), 0))

    def gspec(k):
        return pl.BlockSpec((None, 1, D_MODEL), lambda i: (k, 0, 0))

    def mspec(l, chunk):
        return pl.BlockSpec((None, 1, D_MODEL), lambda i: (l * 8 + _mod_row(i * tr), 0, chunk))

    args, specs, outs, ospecs = [x_ctx, x_lat], [crow, lrow], [], []
    if res is not None:
        u, gk, l, chunk = res
        args += [u, g_norm, mod]
        specs += [row, gspec(gk), mspec(l, chunk)]
        outs += [jax.ShapeDtypeStruct((N_CTX, D_MODEL), F32), jax.ShapeDtypeStruct((N_LAT, D_MODEL), F32)]
        ospecs += [crow, lrow]
    if nxt is not None:
        gk, l, chunk = nxt
        args += [g_norm, mod, mod]
        specs += [gspec(gk), mspec(l, chunk), mspec(l, chunk + 1)]
        outs.append(jax.ShapeDtypeStruct((N_TOK, D_MODEL), BF16))
        ospecs.append(row)
    return pl.pallas_call(
        functools.partial(_resnorm_kernel, has_res=res is not None, has_next=nxt is not None),
        grid=(N_TOK // tr,),
        in_specs=specs, out_specs=ospecs, out_shape=outs,
        compiler_params=_params("arbitrary"),
        name="resnorm",
    )(*args)


def _mm_kernel(x_ref, w_ref, *rest, nk, act, kscale, rope_quarter, w_t):
    o_ref = rest[-1]
    if nk == 1:
        if w_t:
            part = _dot_nt(x_ref[...].astype(BF16), w_ref[0].astype(BF16))
        else:
            part = _dot(x_ref[...].astype(BF16), w_ref[...].astype(BF16))
        if act == "relu2":
            part = jnp.square(jnp.maximum(part, 0.0))
        elif act == "sigmoid":
            part = _sigmoid(part)
        if kscale is not None:
            part = part * jnp.where(pl.program_id(1) >= kscale[0], kscale[1], 1.0)
        if rope_quarter is not None:
            cos_ref, sa_ref, sb_ref = rest[:3]
            part = _rope(part, cos_ref[...], sa_ref[...], sb_ref[...], rope_quarter)
        o_ref[...] = part.astype(o_ref.dtype)
        return

    @pl.when(pl.program_id(2) == 0)
    def _():
        o_ref[...] = jnp.zeros(o_ref.shape, o_ref.dtype)

    o_ref[...] += _dot(x_ref[...].astype(BF16), w_ref[...].astype(BF16))


def _mm(x, w, l, *, col0=0, ncols=None, out_dtype=F32, act=None, tm=2048, tn=512, tk=None,
        m=None, row_map=None, kscale=None, rope=None, w_t=False):
    kdim = w.shape[2] if w_t else w.shape[1]
    ncols = (w.shape[1] if w_t else w.shape[2]) - col0 if ncols is None else ncols
    m = x.shape[0] if m is None else m
    tk = kdim if tk is None else tk
    tm, tn = min(tm, m), min(tn, ncols)
    assert m % tm == 0 and ncols % tn == 0 and kdim % tk == 0
    nk = kdim // tk
    assert nk == 1 or (act is None and out_dtype == F32 and kscale is None and rope is None and not w_t)
    row_map = (lambda i: i) if row_map is None else row_map
    args = [x, w]
    if w_t:
        assert col0 % SUBLANES == 0 and tn % SUBLANES == 0
        wspec = pl.BlockSpec((pl.Element(1), pl.Element(tn), pl.Element(tk)),
                             lambda i, j, k: (l, pl.multiple_of(col0 + j * tn, SUBLANES), 0))
    else:
        assert col0 % tn == 0
        jb = col0 // tn
        wspec = pl.BlockSpec((None, tk, tn), lambda i, j, k: (l, k, jb + j))
    in_specs = [pl.BlockSpec((tm, tk), lambda i, j, k: (row_map(i), k)), wspec]
    if kscale is not None:
        assert kscale[0] % tn == 0
        kscale = (kscale[0] // tn, kscale[1])
    if rope is not None:
        tabs, quarter, ntab = rope
        assert tm == DEC_SEQ and tabs[0].shape == (DEC_SEQ, ntab * tn)
        args += list(tabs)
        in_specs += [pl.BlockSpec((tm, tn), lambda i, j, k: (0, j % ntab))] * 3
    return pl.pallas_call(
        functools.partial(_mm_kernel, nk=nk, act=act, kscale=kscale,
                          rope_quarter=None if rope is None else rope[1], w_t=w_t),
        grid=(m // tm, ncols // tn, nk),
        in_specs=in_specs,
        out_specs=pl.BlockSpec((tm, tn), lambda i, j, k: (i, j)),
        out_shape=jax.ShapeDtypeStruct((m, ncols), out_dtype),
        compiler_params=_params("arbitrary", "arbitrary", "arbitrary"),
        name="mm",
    )(*args)


def _bm_kernel(a0_ref, a1_ref, a2_ref, w0_ref, w1_ref, w2_ref, g0_ref, g1_ref, g2_ref, o_ref):
    acc = g0_ref[...].astype(F32) * _dot(a0_ref[...], w0_ref[...].astype(BF16))
    acc = acc + g1_ref[...].astype(F32) * _dot(a1_ref[...], w1_ref[...].astype(BF16))
    acc = acc + g2_ref[...].astype(F32) * _dot(a2_ref[...], w2_ref[...].astype(BF16))
    o_ref[...] = acc.astype(BF16)


def _branch_merge(acts, ws, gates, l, *, tm=1024, tn=512):
    nj = D_MODEL // tn
    kdim = acts[0].shape[1]
    aspec = pl.BlockSpec((tm, kdim), lambda i, j: (i, 0))
    wspec = pl.BlockSpec((None, kdim, tn), lambda i, j: (l, 0, j))
    gspecs = [pl.BlockSpec((tm, tn), lambda i, j, b=b: (i, b * nj + j)) for b in range(3)]
    return pl.pallas_call(
        _bm_kernel,
        grid=(N_TOK // tm, nj),
        in_specs=[aspec] * 3 + [wspec] * 3 + gspecs,
        out_specs=pl.BlockSpec((tm, tn), lambda i, j: (i, j)),
        out_shape=jax.ShapeDtypeStruct((N_TOK, D_MODEL), BF16),
        compiler_params=_params("arbitrary", "arbitrary"),
        name="branch_merge",
    )(*acts, *ws, gates, gates, gates)


LRU_CT = 512


def _lru_kernel(*refs, T, has_h0, has_state):
    xr_ref, xg_ref, cw_ref, cb_ref, wa_ref, ba_ref, wx_ref, bx_ref, lam_ref = refs[:9]
    refs = refs[9:]
    if has_h0:
        h0_ref = refs[0]
        refs = refs[1:]
    if has_state:
        y_ref, st_ref, xpad_ref, a_ref, b_ref, h_ref = refs[2:]
    else:
        y_ref, xpad_ref, a_ref, b_ref, h_ref = refs[1:]
    ct = xr_ref.shape[1]
    groups = T // SUBLANES
    pad = SUBLANES
    xpad_ref[0:pad, :] = jnp.zeros((pad, ct), F32)
    xpad_ref[pad + T:2 * pad + T, :] = jnp.zeros((pad, ct), F32)
    xpad_ref[pad:pad + T, :] = xr_ref[...]
    row8 = lax.broadcasted_iota(jnp.int32, (SUBLANES, ct), 0)

    for d in range(2):
        for n in range(ct // LRU_BS):
            cs = slice(n * LRU_BS, (n + 1) * LRU_BS)
            xc = jnp.zeros((T, LRU_BS), F32) + cb_ref[d, :, cs]
            for j in range(CONV_W):
                off = (j - (CONV_W - 1)) if d == 0 else ((CONV_W - 1) - j)
                xc = xc + cw_ref[d, j:j + 1, cs] * xpad_ref[pad + off:pad + off + T, cs]
            xcb = xc.astype(BF16)
            r = _sigmoid(_dot(xcb, wa_ref[d, n].astype(BF16)) + ba_ref[d, :, cs])
            i = _sigmoid(_dot(xcb, wx_ref[d, n].astype(BF16)) + bx_ref[d, :, cs])
            log_a = (-LRU_C) * r * _softplus(-lam_ref[d, :, cs])
            a = jnp.exp(log_a)
            a_ref[:, cs] = a
            b_ref[:, cs] = jnp.sqrt(1.0 - a * a) * (i * xc)

        h0 = h0_ref[d] if has_h0 else jnp.zeros((1, ct), F32)

        def body(it, carry, d=d):
            g = it if d == 0 else groups - 1 - it
            r0 = pl.multiple_of(g * SUBLANES, SUBLANES)
            av = a_ref[pl.ds(r0, SUBLANES), :]
            bv = b_ref[pl.ds(r0, SUBLANES), :]
            for s in (1, 2, 4):
                if d == 0:
                    ok = row8 >= s
                    shift = s
                else:
                    ok = row8 < SUBLANES - s
                    shift = SUBLANES - s
                a_sh = jnp.where(ok, pltpu.roll(av, shift, 0), 1.0)
                b_sh = jnp.where(ok, pltpu.roll(bv, shift, 0), 0.0)
                bv = av * b_sh + bv
                av = av * a_sh
            h = av * carry + bv
            if d == 0:
                h_ref[pl.ds(r0, SUBLANES), :] = h
                return h[SUBLANES - 1:SUBLANES, :]
            h_ref[pl.ds(r0, SUBLANES), :] = h_ref[pl.ds(r0, SUBLANES), :] + h
            return h[0:1, :]

        last = lax.fori_loop(0, groups, body, h0, unroll=4)
        if has_state:
            st_ref[d:d + 1, :] = last

    xg = xg_ref[...]
    gelu = 0.5 * xg * (1.0 + jnp.tanh(0.7978845608028654 * (xg + 0.044715 * (xg * xg * xg))))
    y_ref[...] = (h_ref[...] * gelu).astype(BF16)


def _lru(proj_a, lw, l, *, T, nb, blk0, prev, h0=None, st_prev=None):
    ct = LRU_CT
    ncb = D_RNN // ct
    bpc = ct // LRU_BS
    in_specs = [
        pl.BlockSpec((T, ct), lambda b, c: (blk0 + b, c)),
        pl.BlockSpec((T, ct), lambda b, c: (blk0 + b, ncb + c)),
        pl.BlockSpec((None, 2, CONV_W, ct), lambda b, c: (l, 0, 0, c)),
        pl.BlockSpec((None, 2, 1, ct), lambda b, c: (l, 0, 0, c)),
        pl.BlockSpec((None, 2, bpc, LRU_BS, LRU_BS), lambda b, c: (l, 0, c, 0, 0)),
        pl.BlockSpec((None, 2, 1, ct), lambda b, c: (l, 0, 0, c)),
        pl.BlockSpec((None, 2, bpc, LRU_BS, LRU_BS), lambda b, c: (l, 0, c, 0, 0)),
        pl.BlockSpec((None, 2, 1, ct), lambda b, c: (l, 0, 0, c)),
        pl.BlockSpec((None, 2, 1, ct), lambda b, c: (l, 0, 0, c)),
    ]
    args = [proj_a, proj_a, lw["conv_w"], lw["conv_b"], lw["wa"], lw["ba"], lw["wx"], lw["bx"], lw["lam"]]
    if h0 is not None:
        in_specs.append(pl.BlockSpec((None, None, 2, 1, ct), lambda b, c: (b, l, 0, 0, c)))
        args.append(h0)
    has_state = st_prev is not None
    aliases = {len(args): 0}
    in_specs.append(pl.BlockSpec(memory_space=pl.ANY))
    args.append(prev)
    out_specs = [pl.BlockSpec((T, ct), lambda b, c: (blk0 + b, c))]
    out_shape = [jax.ShapeDtypeStruct(prev.shape, prev.dtype)]
    if has_state:
        aliases[len(args)] = 1
        in_specs.append(pl.BlockSpec(memory_space=pl.ANY))
        args.append(st_prev)
        out_specs.append(pl.BlockSpec((None, None, 2, ct), lambda b, c: (b, l, 0, c)))
        out_shape.append(jax.ShapeDtypeStruct(st_prev.shape, st_prev.dtype))
    return pl.pallas_call(
        functools.partial(_lru_kernel, T=T, has_h0=h0 is not None, has_state=has_state),
        grid=(nb, ncb),
        in_specs=in_specs,
        out_specs=out_specs,
        out_shape=out_shape,
        input_output_aliases=aliases,
        scratch_shapes=[pltpu.VMEM((T + 2 * SUBLANES, ct), F32), pltpu.VMEM((T, ct), F32),
                        pltpu.VMEM((T, ct), F32), pltpu.VMEM((T, ct), F32)],
        compiler_params=_params("arbitrary", "arbitrary"),
        name="lru",
    )(*args)


def _mla_prep_kernel(p_ref, gq_ref, gkv_ref, cos_ref, sa_ref, sb_ref, *rest):
    cq_ref, ckvb_ref, kr_ref, ckv_out_ref, kr_out_ref = rest[-5:]
    i = pl.program_id(0)
    cq_ref[...] = _rms(p_ref[:, 0:MLA_Q_RANK], gq_ref[...]).astype(BF16)
    ckv = _rms(p_ref[:, MLA_Q_RANK:MLA_Q_RANK + MLA_KV_RANK], gkv_ref[...])
    ckvb_ref[...] = ckv.astype(BF16)
    k0 = MLA_Q_RANK + MLA_KV_RANK
    kr = p_ref[:, k0:k0 + LANES]

    @pl.when(i < N_CTX // ROW_TILE)
    def _():
        kr_ref[...] = kr
        ckv_out_ref[...] = ckv
        kr_out_ref[...] = kr[:, 0:MLA_ROPE]

    @pl.when(i >= N_CTX // ROW_TILE)
    def _():
        kr_ref[...] = _rope(kr, cos_ref[...], sa_ref[...], sb_ref[...], MLA_ROPE // 4)


def _lat_tile(i):
    return (jnp.maximum(i - N_CTX // ROW_TILE, 0)) % (DEC_SEQ // ROW_TILE)


def _mla_prep(proj_a, gq, gkv, l, tab, caches):
    tr = ROW_TILE
    assert tr == SEQ
    last_ctx = BATCH - 1
    tspec = pl.BlockSpec((tr, LANES), lambda i: (_lat_tile(i), 0))
    args = [proj_a, gq, gkv, *tab]
    in_specs = [pl.BlockSpec((tr, 1024), lambda i: (i, 2)),
                pl.BlockSpec((None, 1, MLA_Q_RANK), lambda i: (l, 0, 0)),
                pl.BlockSpec((None, 1, MLA_KV_RANK), lambda i: (l, 0, 0)),
                tspec, tspec, tspec]
    aliases = {len(args): 3, len(args) + 1: 4}
    args += list(caches)
    in_specs += [pl.BlockSpec(memory_space=pl.ANY)] * 2
    return pl.pallas_call(
        _mla_prep_kernel,
        grid=(N_TOK // tr,),
        in_specs=in_specs,
        out_specs=[pl.BlockSpec((tr, MLA_Q_RANK), lambda i: (i, 0)),
                   pl.BlockSpec((tr, MLA_KV_RANK), lambda i: (i, 0)),
                   pl.BlockSpec((tr, LANES), lambda i: (i, 0)),
                   pl.BlockSpec((None, None, SEQ, MLA_KV_RANK), lambda i: (jnp.minimum(i, last_ctx), l, 0, 0)),
                   pl.BlockSpec((None, None, SEQ, MLA_ROPE), lambda i: (jnp.minimum(i, last_ctx), l, 0, 0))],
        out_shape=[jax.ShapeDtypeStruct((N_TOK, MLA_Q_RANK), BF16),
                   jax.ShapeDtypeStruct((N_TOK, MLA_KV_RANK), BF16),
                   jax.ShapeDtypeStruct((N_TOK, LANES), F32),
                   jax.ShapeDtypeStruct((BATCH, DEPTH, SEQ, MLA_KV_RANK), F32),
                   jax.ShapeDtypeStruct((BATCH, DEPTH, SEQ, MLA_ROPE), F32)],
        input_output_aliases=aliases,
        compiler_params=_params("arbitrary"),
        name="mla_prep",
    )(*args)


def _attn_kernel(*refs, lat):
    if lat:
        q_ref, kvn_ref, krn_ref, kvc_ref, krc_ref, cos_ref, sa_ref, sb_ref, _, o_ref = refs
    else:
        q_ref, kvn_ref, krn_ref, _, o_ref = refs
    scale = (MLA_NOPE + MLA_ROPE) ** -0.5
    krn = krn_ref[:, 0:MLA_ROPE].astype(BF16)
    if lat:
        krc = krc_ref[...].astype(BF16)
        kvc = kvc_ref[...].astype(BF16)
    for h in range(MLA_HEADS):
        c0 = h * MLA_QH
        qn = q_ref[:, c0:c0 + MLA_NOPE].astype(BF16)
        qr = q_ref[:, c0 + MLA_NOPE:c0 + MLA_QH]
        if lat:
            qr = _rope(qr, cos_ref[...], sa_ref[...], sb_ref[...], MLA_ROPE // 4)
        qr = qr[:, 0:MLA_ROPE].astype(BF16)
        kn = kvn_ref[:, c0:c0 + MLA_NOPE]
        v = kvn_ref[:, c0 + MLA_NOPE:c0 + MLA_QH]
        s2 = _dot_nt(qn, kn) + _dot_nt(qr, krn)
        m = jnp.max(s2, axis=-1, keepdims=True)
        if lat:
            s1 = _dot_nt(qn, kvc[:, c0:c0 + MLA_NOPE]) + _dot_nt(qr, krc)
            m = jnp.maximum(m, jnp.max(s1, axis=-1, keepdims=True))
            p1 = jnp.exp((s1 - m) * scale)
        p2 = jnp.exp((s2 - m) * scale)
        den = jnp.sum(p2, axis=-1, keepdims=True)
        o = _dot(p2.astype(BF16), v)
        if lat:
            den = den + jnp.sum(p1, axis=-1, keepdims=True)
            o = o + _dot(p1.astype(BF16), kvc[:, c0 + MLA_NOPE:c0 + MLA_QH])
        o_ref[:, h * MLA_V:(h + 1) * MLA_V] = (o / den).astype(BF16)


def _attn_ctx(q, kv, kr, prev):
    t = SEQ
    return pl.pallas_call(
        functools.partial(_attn_kernel, lat=False),
        grid=(BATCH,),
        in_specs=[pl.BlockSpec((t, MLA_HEADS * MLA_QH), lambda b: (b, 0)),
                  pl.BlockSpec((t, MLA_HEADS * MLA_QH), lambda b: (b, 0)),
                  pl.BlockSpec((t, LANES), lambda b: (b, 0)),
                  pl.BlockSpec(memory_space=pl.ANY)],
        out_specs=pl.BlockSpec((t, MLA_HEADS * MLA_V), lambda b: (b, 0)),
        out_shape=jax.ShapeDtypeStruct(prev.shape, prev.dtype),
        input_output_aliases={3: 0},
        compiler_params=_params("arbitrary"),
        name="attn_ctx",
    )(q, kv, kr, prev)


def _attn_lat(q, kv, kr, kv_cache, cache_krope, l, tab, prev):
    tq = ROW_TILE
    nq = DEC_SEQ // tq
    blk0 = N_CTX // DEC_SEQ
    qblk0 = N_CTX // tq
    tspec = pl.BlockSpec((tq, LANES), lambda b, i: (i, 0))
    return pl.pallas_call(
        functools.partial(_attn_kernel, lat=True),
        grid=(DEC_BATCH, nq),
        in_specs=[pl.BlockSpec((tq, MLA_HEADS * MLA_QH), lambda b, i: (qblk0 + b * nq + i, 0)),
                  pl.BlockSpec((DEC_SEQ, MLA_HEADS * MLA_QH), lambda b, i: (blk0 + b, 0)),
                  pl.BlockSpec((DEC_SEQ, LANES), lambda b, i: (blk0 + b, 0)),
                  pl.BlockSpec((PAST_LEN, MLA_HEADS * MLA_QH), lambda b, i: (b, 0)),
                  pl.BlockSpec((None, None, PAST_LEN, MLA_ROPE), lambda b, i: (b, l, 0, 0)),
                  tspec, tspec, tspec, pl.BlockSpec(memory_space=pl.ANY)],
        out_specs=pl.BlockSpec((tq, MLA_HEADS * MLA_V), lambda b, i: (qblk0 + b * nq + i, 0)),
        out_shape=jax.ShapeDtypeStruct((N_TOK, MLA_HEADS * MLA_V), BF16),
        input_output_aliases={8: 0},
        compiler_params=_params("arbitrary", "arbitrary"),
        name="attn_lat",
    )(q, kv, kr, kv_cache, cache_krope, *tab, prev)


def _log_sigmoid(x):
    return -_softplus(-x)


def _ret_kernel(*refs, lat, T):
    if lat:
        q_ref, k_ref, v_ref, rg_ref, dec_ref, s0_ref, _, o_ref, decay_ref = refs
    else:
        q_ref, k_ref, v_ref, rg_ref, dec_ref = refs[:5]
        o_ref, st_ref, decay_ref = refs[-3:]
    tq = q_ref.shape[0]
    t0 = pl.program_id(0) * tq if lat else 0
    batch_id = pl.program_id(1) if lat else pl.program_id(0)
    lg = _log_sigmoid(dec_ref[...])
    tcol = (t0 + lax.broadcasted_iota(jnp.int32, (tq, 1), 0)).astype(F32)

    @pl.when(batch_id == 0)
    def _():
        rows = (t0 + lax.broadcasted_iota(jnp.int32, (tq, T), 0)).astype(F32)
        cols = lax.broadcasted_iota(jnp.int32, (tq, T), 1).astype(F32)
        diff = rows - cols
        fwd = diff >= 0.0
        bwd = diff <= 0.0
        dpos = jnp.maximum(diff, 0.0)
        dneg = jnp.maximum(-diff, 0.0)
        for h in range(RET_HEADS):
            decay_ref[h] = (jnp.where(fwd, jnp.exp(lg[0:1, h:h + 1] * dpos), 0.0)
                            + jnp.where(bwd, jnp.exp(lg[1:2, h:h + 1] * dneg), 0.0))

    for h in range(RET_HEADS):
        cs = slice(h * RET_DK, (h + 1) * RET_DK)
        lgf = lg[0:1, h:h + 1]
        lgb = lg[1:2, h:h + 1]
        q = q_ref[:, cs]
        k = k_ref[:, cs]
        v = v_ref[:, cs]
        sc = _dot_nt(q, k) * decay_ref[h]
        o = _dot(sc.astype(BF16), v)
        if lat:
            qf = q.astype(F32)
            o = o + _dot((qf * jnp.exp(lgf * (tcol + 1.0))).astype(BF16), s0_ref[0, h].astype(BF16))
            o = o + _dot((qf * jnp.exp(lgb * (T - tcol))).astype(BF16), s0_ref[1, h].astype(BF16))
        else:
            kf = k.astype(F32)
            st_ref[0, h] = _dot_tn((kf * jnp.exp(lgf * (T - 1.0 - tcol))).astype(BF16), v)
            st_ref[1, h] = _dot_tn((kf * jnp.exp(lgb * tcol)).astype(BF16), v)
        mu = jnp.mean(o, axis=-1, keepdims=True)
        oc = o - mu
        var = jnp.mean(oc * oc, axis=-1, keepdims=True)
        rg = rg_ref[:, cs]
        o_ref[:, cs] = (oc * lax.rsqrt(var + EPS) * (rg * _sigmoid(rg))).astype(BF16)


def _ret_ctx(qk, v, rg, decay, l, prev, st_prev):
    t = SEQ
    w = RET_HEADS * RET_DK
    return pl.pallas_call(
        functools.partial(_ret_kernel, lat=False, T=t),
        grid=(BATCH,),
        in_specs=[pl.BlockSpec((t, w), lambda b: (b, 0)),
                  pl.BlockSpec((t, w), lambda b: (b, 1)),
                  pl.BlockSpec((t, w), lambda b: (b, 0)),
                  pl.BlockSpec((t, w), lambda b: (b, 0)),
                  pl.BlockSpec((None, 2, RET_HEADS), lambda b: (l, 0, 0)),
                  pl.BlockSpec(memory_space=pl.ANY), pl.BlockSpec(memory_space=pl.ANY)],
        out_specs=[pl.BlockSpec((t, w), lambda b: (b, 0)),
                   pl.BlockSpec((None, None, 2, RET_HEADS, RET_DK, RET_DV), lambda b: (b, l, 0, 0, 0, 0))],
        out_shape=[jax.ShapeDtypeStruct(prev.shape, prev.dtype),
                   jax.ShapeDtypeStruct(st_prev.shape, st_prev.dtype)],
        input_output_aliases={5: 0, 6: 1},
        scratch_shapes=[pltpu.VMEM((RET_HEADS, t, t), F32)],
        compiler_params=_params("arbitrary"),
        name="ret_ctx",
    )(qk, qk, v, rg, decay, prev, st_prev)


def _ret_lat(qk, v, rg, decay, state_ret, l, prev):
    tq = ROW_TILE
    t = DEC_SEQ
    nq = t // tq
    w = RET_HEADS * RET_DK
    blk0 = N_CTX // t
    qblk0 = N_CTX // tq
    return pl.pallas_call(
        functools.partial(_ret_kernel, lat=True, T=t),
        grid=(nq, DEC_BATCH),
        in_specs=[pl.BlockSpec((tq, w), lambda i, b: (b * nq + i, 0)),
                  pl.BlockSpec((t, w), lambda i, b: (b, 1)),
                  pl.BlockSpec((t, w), lambda i, b: (blk0 + b, 0)),
                  pl.BlockSpec((tq, w), lambda i, b: (qblk0 + b * nq + i, 0)),
                  pl.BlockSpec((None, 2, RET_HEADS), lambda i, b: (l, 0, 0)),
                  pl.BlockSpec((None, None, 2, RET_HEADS, RET_DK, RET_DV), lambda i, b: (b, l, 0, 0, 0, 0)),
                  pl.BlockSpec(memory_space=pl.ANY)],
        out_specs=pl.BlockSpec((tq, w), lambda i, b: (qblk0 + b * nq + i, 0)),
        out_shape=jax.ShapeDtypeStruct((N_TOK, w), BF16),
        scratch_shapes=[pltpu.VMEM((RET_HEADS, tq, t), F32)],
        input_output_aliases={6: 0},
        compiler_params=_params("arbitrary", "arbitrary"),
        name="ret_lat",
    )(qk, qk, v, rg, decay, state_ret, prev)


def _rope_tables(dim, reps):
    half = dim // 2
    quarter = half // 2
    t = jnp.arange(DEC_SEQ)
    row = (t // GRID_W).astype(F32)
    col = (t % GRID_W).astype(F32)
    inv = ROPE_BASE ** (-jnp.arange(0, half, 2, dtype=F32) / half)
    zeros = jnp.zeros((DEC_SEQ, quarter), F32)
    cos_parts, sa_parts, sb_parts = [], [], []
    for pos in (row, col):
        ang = pos[:, None] * inv[None, :]
        c, s = jnp.cos(ang), jnp.sin(ang)
        cos_parts += [c, c]
        sa_parts += [zeros, s]
        sb_parts += [-s, zeros]
    padw = LANES - dim

    def build(parts):
        tab = jnp.concatenate(parts + ([jnp.zeros((DEC_SEQ, padw), F32)] if padw else []), axis=-1)
        return jnp.tile(tab, (1, reps))

    return build(cos_parts), build(sa_parts), build(sb_parts)


def kernel(x_prompt, x_sample, c, cache_mla_ckv, cache_mla_krope, state_lru, state_ret, c_ctx, w_mod, b_mod,
           g_norm, w_in, lru_conv_w, lru_conv_b, lru_wa, lru_ba, lru_wx, lru_bx, lru_lam, mla_gq, mla_gkv,
           mla_wuq, mla_wukv, ret_decay, w_br_lru, w_br_mla, w_br_ret, w_out, w_ff1, w_ff2):
    x_ctx = x_prompt.reshape(N_CTX, D_MODEL)
    x_lat = x_sample.reshape(N_LAT, D_MODEL)
    cond8 = jnp.concatenate([c_ctx[None, :], c, jnp.zeros((8 - 1 - DEC_BATCH, D_MODEL), F32)], axis=0)
    mod = _ada(cond8, w_mod, b_mod).reshape(DEPTH * 8, 1, 6 * D_MODEL)
    g_all = g_norm.reshape(DEPTH * 4, 1, D_MODEL)

    w_in_t = jnp.swapaxes(w_in, 1, 2)
    wuq = jnp.pad(mla_wuq.reshape(DEPTH, MLA_Q_RANK, MLA_HEADS, MLA_NOPE + MLA_ROPE),
                  ((0, 0), (0, 0), (0, 0), (0, MLA_QH - MLA_NOPE - MLA_ROPE)))
    wuq = wuq.reshape(DEPTH, MLA_Q_RANK, MLA_HEADS * MLA_QH)
    lw = {"conv_w": lru_conv_w, "conv_b": lru_conv_b.reshape(DEPTH, 2, 1, D_RNN), "wa": lru_wa,
          "ba": lru_ba.reshape(DEPTH, 2, 1, D_RNN), "wx": lru_wx, "bx": lru_bx.reshape(DEPTH, 2, 1, D_RNN),
          "lam": lru_lam.reshape(DEPTH, 2, 1, D_RNN)}
    gq = mla_gq.reshape(DEPTH, 1, MLA_Q_RANK)
    gkv = mla_gkv.reshape(DEPTH, 1, MLA_KV_RANK)
    h0_lat = state_lru.reshape(DEC_BATCH, DEPTH, 2, 1, D_RNN)
    cache_ckv_rows = cache_mla_ckv.reshape(DEC_BATCH * DEPTH * PAST_LEN, MLA_KV_RANK)
    tab_mla = _rope_tables(MLA_ROPE, 1)
    tab_ret = _rope_tables(RET_DK, RET_HEADS)
    wq = RET_HEADS * RET_DK

    caches = (jnp.zeros((BATCH, DEPTH, SEQ, MLA_KV_RANK), F32), jnp.zeros((BATCH, DEPTH, SEQ, MLA_ROPE), F32))
    st_lru = jnp.zeros((BATCH, DEPTH, 2, D_RNN), F32)
    st_ret = jnp.zeros((BATCH, DEPTH, 2, RET_HEADS, RET_DK, RET_DV), F32)
    y_lru = jnp.zeros((N_TOK, D_RNN), BF16)
    y_mla = jnp.zeros((N_TOK, MLA_HEADS * MLA_V), BF16)
    y_ret = jnp.zeros((N_TOK, RET_HEADS * RET_DV), BF16)

    h, = _resnorm(x_ctx, x_lat, mod, g_all, nxt=(0, 0, 0))
    for l in range(DEPTH):
        proj_a = _mm(h, w_in_t, l, col0=0, ncols=3072, w_t=True)
        kscale = (wq, RET_DK ** -0.5)
        qk_ctx = _mm(h, w_in_t, l, col0=COL_TAIL, ncols=2 * wq, out_dtype=BF16, m=N_CTX, kscale=kscale,
                     w_t=True)
        qk_lat = _mm(h, w_in_t, l, col0=COL_TAIL, ncols=2 * wq, out_dtype=BF16, m=N_LAT, tm=DEC_SEQ,
                     row_map=lambda i: N_CTX // DEC_SEQ + i, kscale=kscale,
                     rope=(tab_ret, RET_DK // 4, 2), w_t=True)
        proj_v = _mm(h, w_in_t, l, col0=COL_TAIL + 2 * wq, ncols=wq, out_dtype=BF16, w_t=True)
        proj_rg = _mm(h, w_in_t, l, col0=COL_TAIL + 3 * wq, ncols=wq, w_t=True)
        gates = _mm(h, w_in_t, l, col0=COL_TAIL + 4 * wq, ncols=3 * D_MODEL, out_dtype=BF16, act="sigmoid",
                    w_t=True)

        y_lru, st_lru = _lru(proj_a, lw, l, T=SEQ, nb=BATCH, blk0=0, prev=y_lru, st_prev=st_lru)
        y_lru, = _lru(proj_a, lw, l, T=DEC_SEQ, nb=DEC_BATCH, blk0=N_CTX // DEC_SEQ, h0=h0_lat, prev=y_lru)

        cqn, ckvb, kr, *caches = _mla_prep(proj_a, gq, gkv, l, tab_mla, caches)
        q = _mm(cqn, wuq, l)
        kv = _mm(ckvb, mla_wukv, l, out_dtype=BF16)
        kv_cache = _mm(cache_ckv_rows, mla_wukv, l, out_dtype=BF16, tm=PAST_LEN, m=DEC_BATCH * PAST_LEN,
                       row_map=lambda i, l=l: i * DEPTH + l)
        y_mla = _attn_ctx(q, kv, kr, y_mla)
        y_mla = _attn_lat(q, kv, kr, kv_cache, cache_mla_krope, l, tab_mla, y_mla)

        y_ret, st_ret = _ret_ctx(qk_ctx, proj_v, proj_rg, ret_decay, l, y_ret, st_ret)
        y_ret = _ret_lat(qk_lat, proj_v, proj_rg, ret_decay, state_ret, l, y_ret)

        merged = _branch_merge((y_lru, y_mla, y_ret), (w_br_lru, w_br_mla, w_br_ret), gates, l)
        u = _mm(merged, w_out, l)
        x_ctx, x_lat, h2 = _resnorm(x_ctx, x_lat, mod, g_all, res=(u, l * 4 + 1, l, 2), nxt=(l * 4 + 2, l, 3))
        ff = _mm(h2, w_ff1, l, out_dtype=BF16, act="relu2")
        y = _mm(ff, w_ff2, l, tn=1024, tk=1024)
        if l + 1 < DEPTH:
            x_ctx, x_lat, h = _resnorm(x_ctx, x_lat, mod, g_all, res=(y, l * 4 + 3, l, 5),
                                       nxt=((l + 1) * 4, l + 1, 0))
        else:
            x_ctx, x_lat = _resnorm(x_ctx, x_lat, mod, g_all, res=(y, l * 4 + 3, l, 5))

    new_ckv, new_krope = caches
    return (x_ctx.reshape(BATCH, SEQ, D_MODEL), x_lat.reshape(DEC_BATCH, DEC_SEQ, D_MODEL),
            new_ckv, new_krope, st_lru, st_ret)
```

```python
import functools

import jax
import jax.numpy as jnp
from jax import lax
from jax.experimental import pallas as pl
from jax.experimental.pallas import tpu as pltpu

F32 = jnp.float32
BF16 = jnp.bfloat16

D_MODEL = 2048
BATCH = 16
SEQ = 256
DEPTH = 4
DEC_BATCH = 4
DEC_SEQ = 1024
PAST_LEN = 256
GRID_W = 64
EPS = 1e-6
ROPE_BASE = 10000.0
D_RNN = D_MODEL // 2
LRU_BLOCKS = 8
LRU_BS = D_RNN // LRU_BLOCKS
CONV_W = 4
LRU_C = 8.0
MLA_HEADS = 8
MLA_NOPE = 128
MLA_ROPE = 64
MLA_V = 128
MLA_Q_RANK = D_MODEL // 4
MLA_KV_RANK = D_MODEL // 8
RET_HEADS = 8
RET_DK = 128
RET_DV = 128
D_FF = 4 * D_MODEL

N_CTX = BATCH * SEQ
N_LAT = DEC_BATCH * DEC_SEQ
N_TOK = N_CTX + N_LAT
COL_TAIL = 2 * D_RNN + MLA_Q_RANK + MLA_KV_RANK + MLA_ROPE
MLA_QH = 256

LANES = 128
SUBLANES = 8
VMEM_LIMIT = 56 * 1024 * 1024
ROW_TILE = 256


def _params(*sem):
    return pltpu.CompilerParams(dimension_semantics=sem, vmem_limit_bytes=VMEM_LIMIT)


def _sigmoid(x):
    return 0.5 * (1.0 + jnp.tanh(0.5 * x))


def _softplus(x):
    return jnp.maximum(x, 0.0) + jnp.log(1.0 + jnp.exp(-jnp.abs(x)))


def _rms(x, g):
    return x * lax.rsqrt(jnp.mean(x * x, axis=-1, keepdims=True) + EPS) * g


def _dot(a, b):
    return jnp.dot(a, b, preferred_element_type=F32)


def _dot_nt(a, b):
    return lax.dot_general(a, b, (((1,), (1,)), ((), ())), preferred_element_type=F32)


def _dot_tn(a, b):
    return lax.dot_general(a, b, (((0,), (0,)), ((), ())), preferred_element_type=F32)


def _rope(x, cos, sa, sb, quarter):
    w = x.shape[-1]
    reps = w // cos.shape[-1]
    if reps > 1:
        cos, sa, sb = (jnp.tile(t, (1, reps)) for t in (cos, sa, sb))
    return x * cos + pltpu.roll(x, quarter, 1) * sa + pltpu.roll(x, w - quarter, 1) * sb


def _mod_row(row0):
    return jnp.where(row0 < N_CTX, 0, 1 + (row0 - N_CTX) // DEC_SEQ)


def _ada_kernel(c_ref, w_ref, b_ref, o_ref):
    c = c_ref[...]
    s = c * _sigmoid(c)
    o_ref[...] = _dot(s.astype(BF16), w_ref[...].astype(BF16)) + b_ref[...]


def _ada(cond8, w_mod, b_mod):
    tn = 1024
    n = w_mod.shape[-1]
    return pl.pallas_call(
        _ada_kernel,
        grid=(DEPTH, n // tn),
        in_specs=[pl.BlockSpec((8, D_MODEL), lambda l, j: (0, 0)),
                  pl.BlockSpec((None, D_MODEL, tn), lambda l, j: (l, 0, j)),
                  pl.BlockSpec((None, 1, tn), lambda l, j: (l, 0, j))],
        out_specs=pl.BlockSpec((None, 8, tn), lambda l, j: (l, 0, j)),
        out_shape=jax.ShapeDtypeStruct((DEPTH, 8, n), F32),
        compiler_params=_params("arbitrary", "arbitrary"),
        name="ada",
    )(cond8, w_mod, b_mod.reshape(DEPTH, 1, n))


def _resnorm_kernel(*refs, has_res, has_next):
    refs = list(refs)
    xc_ref, xl_ref = refs[:2]
    refs = refs[2:]
    is_ctx = pl.program_id(0) < N_CTX // ROW_TILE
    x = jnp.where(is_ctx, xc_ref[...], xl_ref[...])
    if has_res:
        u_ref, gpost_ref, gate_ref = refs[:3]
        refs = refs[3:]
        x = x + gate_ref[...] * _rms(u_ref[...], gpost_ref[...])
    if has_next:
        gpre_ref, shift_ref, scale_ref = refs[:3]
        refs = refs[3:]
    if has_res:
        xco_ref, xlo_ref = refs[:2]
        refs = refs[2:]

        @pl.when(is_ctx)
        def _():
            xco_ref[...] = x

        @pl.when(jnp.logical_not(is_ctx))
        def _():
            xlo_ref[...] = x
    if has_next:
        ho_ref = refs.pop(0)
        h = _rms(x, gpre_ref[...]) * (1.0 + scale_ref[...]) + shift_ref[...]
        ho_ref[...] = h.astype(BF16)


def _resnorm(x_ctx, x_lat, mod, g_norm, *, res=None, nxt=None):
    tr = ROW_TILE
    nctx = N_CTX // tr
    row = pl.BlockSpec((tr, D_MODEL), lambda i: (i, 0))
    crow = pl.BlockSpec((tr, D_MODEL), lambda i: (jnp.minimum(i, nctx - 1), 0))
    lrow = pl.BlockSpec((tr, D_MODEL), lambda i: (jnp.maximum(i - nctx, 0), 0))

    def gspec(k):
        return pl.BlockSpec((None, 1, D_MODEL), lambda i: (k, 0, 0))

    def mspec(l, chunk):
        return pl.BlockSpec((None, 1, D_MODEL), lambda i: (l * 8 + _mod_row(i * tr), 0, chunk))

    args, specs, outs, ospecs = [x_ctx, x_lat], [crow, lrow], [], []
    if res is not None:
        u, gk, l, chunk = res
        args += [u, g_norm, mod]
        specs += [row, gspec(gk), mspec(l, chunk)]
        outs += [jax.ShapeDtypeStruct((N_CTX, D_MODEL), F32), jax.ShapeDtypeStruct((N_LAT, D_MODEL), F32)]
        ospecs += [crow, lrow]
    if nxt is not None:
        gk, l, chunk = nxt
        args += [g_norm, mod, mod]
        specs += [gspec(gk), mspec(l, chunk), mspec(l, chunk + 1)]
        outs.append(jax.ShapeDtypeStruct((N_TOK, D_MODEL), BF16))
        ospecs.append(row)
    return pl.pallas_call(
        functools.partial(_resnorm_kernel, has_res=res is not None, has_next=nxt is not None),
        grid=(N_TOK // tr,),
        in_specs=specs, out_specs=ospecs, out_shape=outs,
        compiler_params=_params("arbitrary"),
        name="resnorm",
    )(*args)


def _mm_kernel(x_ref, w_ref, *rest, nk, act, kscale, rope_quarter, w_t):
    o_ref = rest[-1]
    if nk == 1:
        if w_t:
            part = _dot_nt(x_ref[...].astype(BF16), w_ref[0].astype(BF16))
        else:
            part = _dot(x_ref[...].astype(BF16), w_ref[...].astype(BF16))
        if act == "relu2":
            part = jnp.square(jnp.maximum(part, 0.0))
        elif act == "sigmoid":
            part = _sigmoid(part)
        if kscale is not None:
            part = part * jnp.where(pl.program_id(1) >= kscale[0], kscale[1], 1.0)
        if rope_quarter is not None:
            cos_ref, sa_ref, sb_ref = rest[:3]
            part = _rope(part, cos_ref[...], sa_ref[...], sb_ref[...], rope_quarter)
        o_ref[...] = part.astype(o_ref.dtype)
        return

    @pl.when(pl.program_id(2) == 0)
    def _():
        o_ref[...] = jnp.zeros(o_ref.shape, o_ref.dtype)

    o_ref[...] += _dot(x_ref[...].astype(BF16), w_ref[...].astype(BF16))


def _mm(x, w, l, *, col0=0, ncols=None, out_dtype=F32, act=None, tm=2048, tn=512, tk=None,
        m=None, row_map=None, kscale=None, rope=None, w_t=False):
    kdim = w.shape[2] if w_t else w.shape[1]
    ncols = (w.shape[1] if w_t else w.shape[2]) - col0 if ncols is None else ncols
    m = x.shape[0] if m is None else m
    tk = kdim if tk is None else tk
    tm, tn = min(tm, m), min(tn, ncols)
    assert m % tm == 0 and ncols % tn == 0 and kdim % tk == 0
    nk = kdim // tk
    assert nk == 1 or (act is None and out_dtype == F32 and kscale is None and rope is None and not w_t)
    row_map = (lambda i: i) if row_map is None else row_map
    args = [x, w]
    if w_t:
        assert col0 % SUBLANES == 0 and tn % SUBLANES == 0
        wspec = pl.BlockSpec((pl.Element(1), pl.Element(tn), pl.Element(tk)),
                             lambda i, j, k: (l, pl.multiple_of(col0 + j * tn, SUBLANES), 0))
    else:
        assert col0 % tn == 0
        jb = col0 // tn
        wspec = pl.BlockSpec((None, tk, tn), lambda i, j, k: (l, k, jb + j))
    in_specs = [pl.BlockSpec((tm, tk), lambda i, j, k: (row_map(i), k)), wspec]
    if kscale is not None:
        assert kscale[0] % tn == 0
        kscale = (kscale[0] // tn, kscale[1])
    if rope is not None:
        tabs, quarter = rope
        assert tm == DEC_SEQ and tabs[0].shape == (DEC_SEQ, LANES)
        args += list(tabs)
        in_specs += [pl.BlockSpec((tm, LANES), lambda i, j, k: (0, 0))] * 3
    return pl.pallas_call(
        functools.partial(_mm_kernel, nk=nk, act=act, kscale=kscale,
                          rope_quarter=None if rope is None else rope[1], w_t=w_t),
        grid=(m // tm, ncols // tn, nk),
        in_specs=in_specs,
        out_specs=pl.BlockSpec((tm, tn), lambda i, j, k: (i, j)),
        out_shape=jax.ShapeDtypeStruct((m, ncols), out_dtype),
        compiler_params=_params("arbitrary", "arbitrary", "arbitrary"),
        name="mm",
    )(*args)


def _bm_kernel(a0_ref, a1_ref, a2_ref, w0_ref, w1_ref, w2_ref, g0_ref, g1_ref, g2_ref, o_ref):
    acc = g0_ref[...].astype(F32) * _dot(a0_ref[...], w0_ref[...].astype(BF16))
    acc = acc + g1_ref[...].astype(F32) * _dot(a1_ref[...], w1_ref[...].astype(BF16))
    acc = acc + g2_ref[...].astype(F32) * _dot(a2_ref[...], w2_ref[...].astype(BF16))
    o_ref[...] = acc.astype(BF16)


def _branch_merge(acts, ws, gates, l, *, tm=1024, tn=512):
    nj = D_MODEL // tn
    kdim = acts[0].shape[1]
    aspec = pl.BlockSpec((tm, kdim), lambda i, j: (i, 0))
    wspec = pl.BlockSpec((None, kdim, tn), lambda i, j: (l, 0, j))
    gspecs = [pl.BlockSpec((tm, tn), lambda i, j, b=b: (i, b * nj + j)) for b in range(3)]
    return pl.pallas_call(
        _bm_kernel,
        grid=(N_TOK // tm, nj),
        in_specs=[aspec] * 3 + [wspec] * 3 + gspecs,
        out_specs=pl.BlockSpec((tm, tn), lambda i, j: (i, j)),
        out_shape=jax.ShapeDtypeStruct((N_TOK, D_MODEL), BF16),
        compiler_params=_params("arbitrary", "arbitrary"),
        name="branch_merge",
    )(*acts, *ws, gates, gates, gates)


LRU_CT = 512


def _lru_kernel(*refs, T, has_h0, has_state):
    xr_ref, xg_ref, cw_ref, cb_ref, wa_ref, ba_ref, wx_ref, bx_ref, lam_ref = refs[:9]
    refs = refs[9:]
    if has_h0:
        h0_ref = refs[0]
        refs = refs[1:]
    if has_state:
        y_ref, st_ref, xpad_ref, a_ref, b_ref, h_ref = refs[2:]
    else:
        y_ref, xpad_ref, a_ref, b_ref, h_ref = refs[1:]
    ct = xr_ref.shape[1]
    groups = T // SUBLANES
    pad = SUBLANES
    xpad_ref[0:pad, :] = jnp.zeros((pad, ct), F32)
    xpad_ref[pad + T:2 * pad + T, :] = jnp.zeros((pad, ct), F32)
    xpad_ref[pad:pad + T, :] = xr_ref[...]
    row8 = lax.broadcasted_iota(jnp.int32, (SUBLANES, ct), 0)

    for d in range(2):
        for n in range(ct // LRU_BS):
            cs = slice(n * LRU_BS, (n + 1) * LRU_BS)
            xc = jnp.zeros((T, LRU_BS), F32) + cb_ref[d, :, cs]
            for j in range(CONV_W):
                off = (j - (CONV_W - 1)) if d == 0 else ((CONV_W - 1) - j)
                xc = xc + cw_ref[d, j:j + 1, cs] * xpad_ref[pad + off:pad + off + T, cs]
            xcb = xc.astype(BF16)
            r = _sigmoid(_dot(xcb, wa_ref[d, n].astype(BF16)) + ba_ref[d, :, cs])
            i = _sigmoid(_dot(xcb, wx_ref[d, n].astype(BF16)) + bx_ref[d, :, cs])
            log_a = (-LRU_C) * r * _softplus(-lam_ref[d, :, cs])
            a = jnp.exp(log_a)
            a_ref[:, cs] = a
            b_ref[:, cs] = jnp.sqrt(1.0 - a * a) * (i * xc)

        h0 = h0_ref[d] if has_h0 else jnp.zeros((1, ct), F32)

        def body(it, carry, d=d):
            g = it if d == 0 else groups - 1 - it
            r0 = pl.multiple_of(g * SUBLANES, SUBLANES)
            av = a_ref[pl.ds(r0, SUBLANES), :]
            bv = b_ref[pl.ds(r0, SUBLANES), :]
            for s in (1, 2, 4):
                if d == 0:
                    ok = row8 >= s
                    shift = s
                else:
                    ok = row8 < SUBLANES - s
                    shift = SUBLANES - s
                a_sh = jnp.where(ok, pltpu.roll(av, shift, 0), 1.0)
                b_sh = jnp.where(ok, pltpu.roll(bv, shift, 0), 0.0)
                bv = av * b_sh + bv
                av = av * a_sh
            h = av * carry + bv
            if d == 0:
                h_ref[pl.ds(r0, SUBLANES), :] = h
                return h[SUBLANES - 1:SUBLANES, :]
            h_ref[pl.ds(r0, SUBLANES), :] = h_ref[pl.ds(r0, SUBLANES), :] + h
            return h[0:1, :]

        last = lax.fori_loop(0, groups, body, h0, unroll=4)
        if has_state:
            st_ref[d:d + 1, :] = last

    xg = xg_ref[...]
    gelu = 0.5 * xg * (1.0 + jnp.tanh(0.7978845608028654 * (xg + 0.044715 * (xg * xg * xg))))
    y_ref[...] = (h_ref[...] * gelu).astype(BF16)


def _lru(proj_a, lw, l, *, T, nb, blk0, prev, h0=None, st_prev=None):
    ct = LRU_CT
    ncb = D_RNN // ct
    bpc = ct // LRU_BS
    in_specs = [
        pl.BlockSpec((T, ct), lambda b, c: (blk0 + b, c)),
        pl.BlockSpec((T, ct), lambda b, c: (blk0 + b, ncb + c)),
        pl.BlockSpec((None, 2, CONV_W, ct), lambda b, c: (l, 0, 0, c)),
        pl.BlockSpec((None, 2, 1, ct), lambda b, c: (l, 0, 0, c)),
        pl.BlockSpec((None, 2, bpc, LRU_BS, LRU_BS), lambda b, c: (l, 0, c, 0, 0)),
        pl.BlockSpec((None, 2, 1, ct), lambda b, c: (l, 0, 0, c)),
        pl.BlockSpec((None, 2, bpc, LRU_BS, LRU_BS), lambda b, c: (l, 0, c, 0, 0)),
        pl.BlockSpec((None, 2, 1, ct), lambda b, c: (l, 0, 0, c)),
        pl.BlockSpec((None, 2, 1, ct), lambda b, c: (l, 0, 0, c)),
    ]
    args = [proj_a, proj_a, lw["conv_w"], lw["conv_b"], lw["wa"], lw["ba"], lw["wx"], lw["bx"], lw["lam"]]
    if h0 is not None:
        in_specs.append(pl.BlockSpec((None, None, 2, 1, ct), lambda b, c: (b, l, 0, 0, c)))
        args.append(h0)
    has_state = st_prev is not None
    aliases = {len(args): 0}
    in_specs.append(pl.BlockSpec(memory_space=pl.ANY))
    args.append(prev)
    out_specs = [pl.BlockSpec((T, ct), lambda b, c: (blk0 + b, c))]
    out_shape = [jax.ShapeDtypeStruct(prev.shape, prev.dtype)]
    if has_state:
        aliases[len(args)] = 1
        in_specs.append(pl.BlockSpec(memory_space=pl.ANY))
        args.append(st_prev)
        out_specs.append(pl.BlockSpec((None, None, 2, ct), lambda b, c: (b, l, 0, c)))
        out_shape.append(jax.ShapeDtypeStruct(st_prev.shape, st_prev.dtype))
    return pl.pallas_call(
        functools.partial(_lru_kernel, T=T, has_h0=h0 is not None, has_state=has_state),
        grid=(nb, ncb),
        in_specs=in_specs,
        out_specs=out_specs,
        out_shape=out_shape,
        input_output_aliases=aliases,
        scratch_shapes=[pltpu.VMEM((T + 2 * SUBLANES, ct), F32), pltpu.VMEM((T, ct), F32),
                        pltpu.VMEM((T, ct), F32), pltpu.VMEM((T, ct), F32)],
        compiler_params=_params("arbitrary", "arbitrary"),
        name="lru",
    )(*args)


def _mla_prep_kernel(p_ref, gq_ref, gkv_ref, cos_ref, sa_ref, sb_ref, *rest):
    cq_ref, ckvb_ref, kr_ref, ckv_out_ref, kr_out_ref = rest[-5:]
    i = pl.program_id(0)
    cq_ref[...] = _rms(p_ref[:, 0:MLA_Q_RANK], gq_ref[...]).astype(BF16)
    ckv = _rms(p_ref[:, MLA_Q_RANK:MLA_Q_RANK + MLA_KV_RANK], gkv_ref[...])
    ckvb_ref[...] = ckv.astype(BF16)
    k0 = MLA_Q_RANK + MLA_KV_RANK
    kr = p_ref[:, k0:k0 + LANES]

    @pl.when(i < N_CTX // ROW_TILE)
    def _():
        kr_ref[...] = kr
        ckv_out_ref[...] = ckv
        kr_out_ref[...] = kr[:, 0:MLA_ROPE]

    @pl.when(i >= N_CTX // ROW_TILE)
    def _():
        kr_ref[...] = _rope(kr, cos_ref[...], sa_ref[...], sb_ref[...], MLA_ROPE // 4)


def _lat_tile(i):
    return (jnp.maximum(i - N_CTX // ROW_TILE, 0)) % (DEC_SEQ // ROW_TILE)


def _mla_prep(proj_a, gq, gkv, l, tab, caches):
    tr = ROW_TILE
    assert tr == SEQ
    last_ctx = BATCH - 1
    tspec = pl.BlockSpec((tr, LANES), lambda i: (_lat_tile(i), 0))
    args = [proj_a, gq, gkv, *tab]
    in_specs = [pl.BlockSpec((tr, 1024), lambda i: (i, 2)),
                pl.BlockSpec((None, 1, MLA_Q_RANK), lambda i: (l, 0, 0)),
                pl.BlockSpec((None, 1, MLA_KV_RANK), lambda i: (l, 0, 0)),
                tspec, tspec, tspec]
    aliases = {len(args): 3, len(args) + 1: 4}
    args += list(caches)
    in_specs += [pl.BlockSpec(memory_space=pl.ANY)] * 2
    return pl.pallas_call(
        _mla_prep_kernel,
        grid=(N_TOK // tr,),
        in_specs=in_specs,
        out_specs=[pl.BlockSpec((tr, MLA_Q_RANK), lambda i: (i, 0)),
                   pl.BlockSpec((tr, MLA_KV_RANK), lambda i: (i, 0)),
                   pl.BlockSpec((tr, LANES), lambda i: (i, 0)),
                   pl.BlockSpec((None, None, SEQ, MLA_KV_RANK), lambda i: (jnp.minimum(i, last_ctx), l, 0, 0)),
                   pl.BlockSpec((None, None, SEQ, MLA_ROPE), lambda i: (jnp.minimum(i, last_ctx), l, 0, 0))],
        out_shape=[jax.ShapeDtypeStruct((N_TOK, MLA_Q_RANK), BF16),
                   jax.ShapeDtypeStruct((N_TOK, MLA_KV_RANK), BF16),
                   jax.ShapeDtypeStruct((N_TOK, LANES), F32),
                   jax.ShapeDtypeStruct((BATCH, DEPTH, SEQ, MLA_KV_RANK), F32),
                   jax.ShapeDtypeStruct((BATCH, DEPTH, SEQ, MLA_ROPE), F32)],
        input_output_aliases=aliases,
        compiler_params=_params("arbitrary"),
        name="mla_prep",
    )(*args)


def _attn_kernel(*refs, lat):
    if lat:
        (q_ref, kvn_ref, krn_ref, kvc_ref, krc_ref, cos_ref, sa_ref, sb_ref, _,
         o_ref, kcat_ref, vcat_ref) = refs
    else:
        q_ref, kvn_ref, krn_ref, _, o_ref, kcat_ref = refs
    scale = (MLA_NOPE + MLA_ROPE) ** -0.5
    s_new = kvn_ref.shape[0]
    off = PAST_LEN if lat else 0

    def build():
        lane = lax.broadcasted_iota(jnp.int32, (s_new, LANES), 1)
        kr_new = jnp.where(lane < MLA_ROPE, krn_ref[...], 0.0).astype(BF16)
        for h in range(MLA_HEADS):
            c0 = h * MLA_QH
            kcat_ref[h, off:off + s_new, 0:MLA_NOPE] = kvn_ref[:, c0:c0 + MLA_NOPE]
            kcat_ref[h, off:off + s_new, MLA_NOPE:MLA_QH] = kr_new
            if lat:
                kcat_ref[h, 0:off, 0:MLA_NOPE] = kvc_ref[:, c0:c0 + MLA_NOPE]
                kcat_ref[h, 0:off, MLA_NOPE:MLA_NOPE + MLA_ROPE] = krc_ref[...].astype(BF16)
                kcat_ref[h, 0:off, MLA_NOPE + MLA_ROPE:MLA_QH] = jnp.zeros((off, MLA_QH - MLA_NOPE - MLA_ROPE), BF16)
                vcat_ref[0:off, h * MLA_V:(h + 1) * MLA_V] = kvc_ref[:, c0 + MLA_NOPE:c0 + MLA_QH]
                vcat_ref[off:off + s_new, h * MLA_V:(h + 1) * MLA_V] = kvn_ref[:, c0 + MLA_NOPE:c0 + MLA_QH]

    if lat:
        pl.when(pl.program_id(1) == 0)(build)
    else:
        build()

    for h in range(MLA_HEADS):
        c0 = h * MLA_QH
        qn = q_ref[:, c0:c0 + MLA_NOPE].astype(F32)
        qr = q_ref[:, c0 + MLA_NOPE:c0 + MLA_QH].astype(F32)
        if lat:
            qr = _rope(qr, cos_ref[...], sa_ref[...], sb_ref[...], MLA_ROPE // 4)
        qh = (jnp.concatenate([qn, qr], axis=1) * scale).astype(BF16)
        s = _dot_nt(qh, kcat_ref[h])
        p = jnp.exp(s - jnp.max(s, axis=-1, keepdims=True))
        den = jnp.sum(p, axis=-1, keepdims=True)
        v = vcat_ref[:, h * MLA_V:(h + 1) * MLA_V] if lat else kvn_ref[:, c0 + MLA_NOPE:c0 + MLA_QH]
        o_ref[:, h * MLA_V:(h + 1) * MLA_V] = (_dot(p.astype(BF16), v) / den).astype(BF16)


def _attn_ctx(q, kv, kr, prev):
    t = SEQ
    return pl.pallas_call(
        functools.partial(_attn_kernel, lat=False),
        grid=(BATCH,),
        in_specs=[pl.BlockSpec((t, MLA_HEADS * MLA_QH), lambda b: (b, 0)),
                  pl.BlockSpec((t, MLA_HEADS * MLA_QH), lambda b: (b, 0)),
                  pl.BlockSpec((t, LANES), lambda b: (b, 0)),
                  pl.BlockSpec(memory_space=pl.ANY)],
        out_specs=pl.BlockSpec((t, MLA_HEADS * MLA_V), lambda b: (b, 0)),
        out_shape=jax.ShapeDtypeStruct(prev.shape, prev.dtype),
        input_output_aliases={3: 0},
        scratch_shapes=[pltpu.VMEM((MLA_HEADS, t, MLA_QH), BF16)],
        compiler_params=_params("arbitrary"),
        name="attn_ctx",
    )(q, kv, kr, prev)


def _attn_lat(q, kv, kr, kv_cache, cache_krope, l, tab, prev):
    tq = ROW_TILE
    nq = DEC_SEQ // tq
    blk0 = N_CTX // DEC_SEQ
    qblk0 = N_CTX // tq
    tspec = pl.BlockSpec((tq, LANES), lambda b, i: (i, 0))
    return pl.pallas_call(
        functools.partial(_attn_kernel, lat=True),
        grid=(DEC_BATCH, nq),
        in_specs=[pl.BlockSpec((tq, MLA_HEADS * MLA_QH), lambda b, i: (qblk0 + b * nq + i, 0)),
                  pl.BlockSpec((DEC_SEQ, MLA_HEADS * MLA_QH), lambda b, i: (blk0 + b, 0)),
                  pl.BlockSpec((DEC_SEQ, LANES), lambda b, i: (blk0 + b, 0)),
                  pl.BlockSpec((PAST_LEN, MLA_HEADS * MLA_QH), lambda b, i: (b, 0)),
                  pl.BlockSpec((None, None, PAST_LEN, MLA_ROPE), lambda b, i: (b, l, 0, 0)),
                  tspec, tspec, tspec, pl.BlockSpec(memory_space=pl.ANY)],
        out_specs=pl.BlockSpec((tq, MLA_HEADS * MLA_V), lambda b, i: (qblk0 + b * nq + i, 0)),
        out_shape=jax.ShapeDtypeStruct((N_TOK, MLA_HEADS * MLA_V), BF16),
        input_output_aliases={8: 0},
        scratch_shapes=[pltpu.VMEM((MLA_HEADS, PAST_LEN + DEC_SEQ, MLA_QH), BF16),
                        pltpu.VMEM((PAST_LEN + DEC_SEQ, MLA_HEADS * MLA_V), BF16)],
        compiler_params=_params("arbitrary", "arbitrary"),
        name="attn_lat",
    )(q, kv, kr, kv_cache, cache_krope, *tab, prev)


def _log_sigmoid(x):
    return -_softplus(-x)


def _ret_kernel(*refs, lat, T):
    if lat:
        q_ref, k_ref, v_ref, rg_ref, dec_ref, s0_ref, _, o_ref, decay_ref = refs
    else:
        q_ref, k_ref, v_ref, rg_ref, dec_ref = refs[:5]
        o_ref, st_ref, decay_ref = refs[-3:]
    tq = q_ref.shape[0]
    t0 = pl.program_id(0) * tq if lat else 0
    batch_id = pl.program_id(1) if lat else pl.program_id(0)
    lg = _log_sigmoid(dec_ref[...])
    tcol = (t0 + lax.broadcasted_iota(jnp.int32, (tq, 1), 0)).astype(F32)

    @pl.when(batch_id == 0)
    def _():
        rows = (t0 + lax.broadcasted_iota(jnp.int32, (tq, T), 0)).astype(F32)
        cols = lax.broadcasted_iota(jnp.int32, (tq, T), 1).astype(F32)
        diff = rows - cols
        fwd = diff >= 0.0
        bwd = diff <= 0.0
        dpos = jnp.maximum(diff, 0.0)
        dneg = jnp.maximum(-diff, 0.0)
        for h in range(RET_HEADS):
            decay_ref[h] = (jnp.where(fwd, jnp.exp(lg[0:1, h:h + 1] * dpos), 0.0)
                            + jnp.where(bwd, jnp.exp(lg[1:2, h:h + 1] * dneg), 0.0))

    for h in range(RET_HEADS):
        cs = slice(h * RET_DK, (h + 1) * RET_DK)
        lgf = lg[0:1, h:h + 1]
        lgb = lg[1:2, h:h + 1]
        q = q_ref[:, cs]
        k = k_ref[:, cs]
        v = v_ref[:, cs]
        sc = _dot_nt(q, k) * decay_ref[h]
        o = _dot(sc.astype(BF16), v)
        if lat:
            qf = q.astype(F32)
            o = o + _dot((qf * jnp.exp(lgf * (tcol + 1.0))).astype(BF16), s0_ref[0, h].astype(BF16))
            o = o + _dot((qf * jnp.exp(lgb * (T - tcol))).astype(BF16), s0_ref[1, h].astype(BF16))
        else:
            kf = k.astype(F32)
            st_ref[0, h] = _dot_tn((kf * jnp.exp(lgf * (T - 1.0 - tcol))).astype(BF16), v)
            st_ref[1, h] = _dot_tn((kf * jnp.exp(lgb * tcol)).astype(BF16), v)
        mu = jnp.mean(o, axis=-1, keepdims=True)
        oc = o - mu
        var = jnp.mean(oc * oc, axis=-1, keepdims=True)
        rg = rg_ref[:, cs].astype(F32)
        o_ref[:, cs] = (oc * lax.rsqrt(var + EPS) * (rg * _sigmoid(rg))).astype(BF16)


def _ret_ctx(qk, v, rg, decay, l, prev, st_prev):
    t = SEQ
    w = RET_HEADS * RET_DK
    return pl.pallas_call(
        functools.partial(_ret_kernel, lat=False, T=t),
        grid=(BATCH,),
        in_specs=[pl.BlockSpec((t, w), lambda b: (b, 0)),
                  pl.BlockSpec((t, w), lambda b: (b, 1)),
                  pl.BlockSpec((t, w), lambda b: (b, 0)),
                  pl.BlockSpec((t, w), lambda b: (b, 0)),
                  pl.BlockSpec((None, 2, RET_HEADS), lambda b: (l, 0, 0)),
                  pl.BlockSpec(memory_space=pl.ANY), pl.BlockSpec(memory_space=pl.ANY)],
        out_specs=[pl.BlockSpec((t, w), lambda b: (b, 0)),
                   pl.BlockSpec((None, None, 2, RET_HEADS, RET_DK, RET_DV), lambda b: (b, l, 0, 0, 0, 0))],
        out_shape=[jax.ShapeDtypeStruct(prev.shape, prev.dtype),
                   jax.ShapeDtypeStruct(st_prev.shape, st_prev.dtype)],
        input_output_aliases={5: 0, 6: 1},
        scratch_shapes=[pltpu.VMEM((RET_HEADS, t, t), F32)],
        compiler_params=_params("arbitrary"),
        name="ret_ctx",
    )(qk, qk, v, rg, decay, prev, st_prev)


def _ret_lat(qk, v, rg, decay, state_ret, l, prev):
    tq = ROW_TILE
    t = DEC_SEQ
    nq = t // tq
    w = RET_HEADS * RET_DK
    blk0 = N_CTX // t
    qblk0 = N_CTX // tq
    return pl.pallas_call(
        functools.partial(_ret_kernel, lat=True, T=t),
        grid=(nq, DEC_BATCH),
        in_specs=[pl.BlockSpec((tq, w), lambda i, b: (b * nq + i, 0)),
                  pl.BlockSpec((t, w), lambda i, b: (b, 1)),
                  pl.BlockSpec((t, w), lambda i, b: (blk0 + b, 0)),
                  pl.BlockSpec((tq, w), lambda i, b: (qblk0 + b * nq + i, 0)),
                  pl.BlockSpec((None, 2, RET_HEADS), lambda i, b: (l, 0, 0)),
                  pl.BlockSpec((None, None, 2, RET_HEADS, RET_DK, RET_DV), lambda i, b: (b, l, 0, 0, 0, 0)),
                  pl.BlockSpec(memory_space=pl.ANY)],
        out_specs=pl.BlockSpec((tq, w), lambda i, b: (qblk0 + b * nq + i, 0)),
        out_shape=jax.ShapeDtypeStruct((N_TOK, w), BF16),
        scratch_shapes=[pltpu.VMEM((RET_HEADS, tq, t), F32)],
        input_output_aliases={6: 0},
        compiler_params=_params("arbitrary", "arbitrary"),
        name="ret_lat",
    )(qk, qk, v, rg, decay, state_ret, prev)


def _rope_tables(dim):
    half = dim // 2
    quarter = half // 2
    t = jnp.arange(DEC_SEQ)
    row = (t // GRID_W).astype(F32)
    col = (t % GRID_W).astype(F32)
    inv = ROPE_BASE ** (-jnp.arange(0, half, 2, dtype=F32) / half)
    zeros = jnp.zeros((DEC_SEQ, quarter), F32)
    cos_parts, sa_parts, sb_parts = [], [], []
    for pos in (row, col):
        ang = pos[:, None] * inv[None, :]
        c, s = jnp.cos(ang), jnp.sin(ang)
        cos_parts += [c, c]
        sa_parts += [zeros, s]
        sb_parts += [-s, zeros]
    padw = LANES - dim

    def build(parts):
        return jnp.concatenate(parts + ([jnp.zeros((DEC_SEQ, padw), F32)] if padw else []), axis=-1)

    return build(cos_parts), build(sa_parts), build(sb_parts)


def kernel(x_prompt, x_sample, c, cache_mla_ckv, cache_mla_krope, state_lru, state_ret, c_ctx, w_mod, b_mod,
           g_norm, w_in, lru_conv_w, lru_conv_b, lru_wa, lru_ba, lru_wx, lru_bx, lru_lam, mla_gq, mla_gkv,
           mla_wuq, mla_wukv, ret_decay, w_br_lru, w_br_mla, w_br_ret, w_out, w_ff1, w_ff2):
    x_ctx = x_prompt.reshape(N_CTX, D_MODEL)
    x_lat = x_sample.reshape(N_LAT, D_MODEL)
    cond8 = jnp.concatenate([c_ctx[None, :], c, jnp.zeros((8 - 1 - DEC_BATCH, D_MODEL), F32)], axis=0)
    mod = _ada(cond8, w_mod, b_mod).reshape(DEPTH * 8, 1, 6 * D_MODEL)
    g_all = g_norm.reshape(DEPTH * 4, 1, D_MODEL)

    w_in_t = jnp.swapaxes(w_in, 1, 2)
    wuq = jnp.pad(mla_wuq.reshape(DEPTH, MLA_Q_RANK, MLA_HEADS, MLA_NOPE + MLA_ROPE),
                  ((0, 0), (0, 0), (0, 0), (0, MLA_QH - MLA_NOPE - MLA_ROPE)))
    wuq = wuq.reshape(DEPTH, MLA_Q_RANK, MLA_HEADS * MLA_QH)
    lw = {"conv_w": lru_conv_w, "conv_b": lru_conv_b.reshape(DEPTH, 2, 1, D_RNN), "wa": lru_wa,
          "ba": lru_ba.reshape(DEPTH, 2, 1, D_RNN), "wx": lru_wx, "bx": lru_bx.reshape(DEPTH, 2, 1, D_RNN),
          "lam": lru_lam.reshape(DEPTH, 2, 1, D_RNN)}
    gq = mla_gq.reshape(DEPTH, 1, MLA_Q_RANK)
    gkv = mla_gkv.reshape(DEPTH, 1, MLA_KV_RANK)
    h0_lat = state_lru.reshape(DEC_BATCH, DEPTH, 2, 1, D_RNN)
    cache_ckv_rows = cache_mla_ckv.reshape(DEC_BATCH * DEPTH * PAST_LEN, MLA_KV_RANK)
    tab_mla = _rope_tables(MLA_ROPE)
    tab_ret = _rope_tables(RET_DK)
    wq = RET_HEADS * RET_DK

    caches = (jnp.zeros((BATCH, DEPTH, SEQ, MLA_KV_RANK), F32), jnp.zeros((BATCH, DEPTH, SEQ, MLA_ROPE), F32))
    st_lru = jnp.zeros((BATCH, DEPTH, 2, D_RNN), F32)
    st_ret = jnp.zeros((BATCH, DEPTH, 2, RET_HEADS, RET_DK, RET_DV), F32)
    y_lru = jnp.zeros((N_TOK, D_RNN), BF16)
    y_mla = jnp.zeros((N_TOK, MLA_HEADS * MLA_V), BF16)
    y_ret = jnp.zeros((N_TOK, RET_HEADS * RET_DV), BF16)

    h, = _resnorm(x_ctx, x_lat, mod, g_all, nxt=(0, 0, 0))
    for l in range(DEPTH):
        proj_a = _mm(h, w_in_t, l, col0=0, ncols=3072, w_t=True)
        kscale = (wq, RET_DK ** -0.5)
        qk_ctx = _mm(h, w_in_t, l, col0=COL_TAIL, ncols=2 * wq, out_dtype=BF16, m=N_CTX, kscale=kscale,
                     w_t=True)
        qk_lat = _mm(h, w_in_t, l, col0=COL_TAIL, ncols=2 * wq, out_dtype=BF16, m=N_LAT, tm=DEC_SEQ,
                     row_map=lambda i: N_CTX // DEC_SEQ + i, kscale=kscale,
                     rope=(tab_ret, RET_DK // 4), w_t=True)
        proj_v = _mm(h, w_in_t, l, col0=COL_TAIL + 2 * wq, ncols=wq, out_dtype=BF16, w_t=True)
        proj_rg = _mm(h, w_in_t, l, col0=COL_TAIL + 3 * wq, ncols=wq, out_dtype=BF16, w_t=True)
        gates = _mm(h, w_in_t, l, col0=COL_TAIL + 4 * wq, ncols=3 * D_MODEL, out_dtype=BF16, act="sigmoid",
                    w_t=True)

        y_lru, st_lru = _lru(proj_a, lw, l, T=SEQ, nb=BATCH, blk0=0, prev=y_lru, st_prev=st_lru)
        y_lru, = _lru(proj_a, lw, l, T=DEC_SEQ, nb=DEC_BATCH, blk0=N_CTX // DEC_SEQ, h0=h0_lat, prev=y_lru)

        cqn, ckvb, kr, *caches = _mla_prep(proj_a, gq, gkv, l, tab_mla, caches)
        q = _mm(cqn, wuq, l, out_dtype=BF16)
        kv = _mm(ckvb, mla_wukv, l, out_dtype=BF16)
        kv_cache = _mm(cache_ckv_rows, mla_wukv, l, out_dtype=BF16, tm=PAST_LEN, m=DEC_BATCH * PAST_LEN,
                       row_map=lambda i, l=l: i * DEPTH + l)
        y_mla = _attn_ctx(q, kv, kr, y_mla)
        y_mla = _attn_lat(q, kv, kr, kv_cache, cache_mla_krope, l, tab_mla, y_mla)

        y_ret, st_ret = _ret_ctx(qk_ctx, proj_v, proj_rg, ret_decay, l, y_ret, st_ret)
        y_ret = _ret_lat(qk_lat, proj_v, proj_rg, ret_decay, state_ret, l, y_ret)

        merged = _branch_merge((y_lru, y_mla, y_ret), (w_br_lru, w_br_mla, w_br_ret), gates, l)
        u = _mm(merged, w_out, l)
        x_ctx, x_lat, h2 = _resnorm(x_ctx, x_lat, mod, g_all, res=(u, l * 4 + 1, l, 2), nxt=(l * 4 + 2, l, 3))
        ff = _mm(h2, w_ff1, l, out_dtype=BF16, act="relu2")
        y = _mm(ff, w_ff2, l, tn=1024, tk=1024)
        if l + 1 < DEPTH:
            x_ctx, x_lat, h = _resnorm(x_ctx, x_lat, mod, g_all, res=(y, l * 4 + 3, l, 5),
                                       nxt=((l + 1) * 4, l + 1, 0))
        else:
            x_ctx, x_lat = _resnorm(x_ctx, x_lat, mod, g_all, res=(y, l * 4 + 3, l, 5))

    new_ckv, new_krope = caches
    return (x_ctx.reshape(BATCH, SEQ, D_MODEL), x_lat.reshape(DEC_BATCH, DEC_SEQ, D_MODEL),
            new_ckv, new_krope, st_lru, st_ret)
```

```python
import functools

import jax
import jax.numpy as jnp
from jax import lax
from jax.experimental import pallas as pl
from jax.experimental.pallas import tpu as pltpu

F32 = jnp.float32
BF16 = jnp.bfloat16

D_MODEL = 2048
BATCH = 16
SEQ = 256
DEPTH = 4
DEC_BATCH = 4
DEC_SEQ = 1024
PAST_LEN = 256
GRID_W = 64
EPS = 1e-6
ROPE_BASE = 10000.0
D_RNN = D_MODEL // 2
LRU_BLOCKS = 8
LRU_BS = D_RNN // LRU_BLOCKS
CONV_W = 4
LRU_C = 8.0
MLA_HEADS = 8
MLA_NOPE = 128
MLA_ROPE = 64
MLA_V = 128
MLA_Q_RANK = D_MODEL // 4
MLA_KV_RANK = D_MODEL // 8
RET_HEADS = 8
RET_DK = 128
RET_DV = 128
D_FF = 4 * D_MODEL

N_CTX = BATCH * SEQ
N_LAT = DEC_BATCH * DEC_SEQ
N_TOK = N_CTX + N_LAT
COL_TAIL = 2 * D_RNN + MLA_Q_RANK + MLA_KV_RANK + MLA_ROPE
MLA_QH = 256

LANES = 128
SUBLANES = 8
VMEM_LIMIT = 56 * 1024 * 1024
ROW_TILE = 256


def _params(*sem):
    return pltpu.CompilerParams(dimension_semantics=sem, vmem_limit_bytes=VMEM_LIMIT)


def _sigmoid(x):
    return 0.5 * (1.0 + jnp.tanh(0.5 * x))


def _softplus(x):
    return jnp.maximum(x, 0.0) + jnp.log(1.0 + jnp.exp(-jnp.abs(x)))


def _rms(x, g):
    return x * lax.rsqrt(jnp.mean(x * x, axis=-1, keepdims=True) + EPS) * g


def _dot(a, b):
    return jnp.dot(a, b, preferred_element_type=F32)


def _dot_nt(a, b):
    return lax.dot_general(a, b, (((1,), (1,)), ((), ())), preferred_element_type=F32)


def _dot_tn(a, b):
    return lax.dot_general(a, b, (((0,), (0,)), ((), ())), preferred_element_type=F32)


def _rope(x, cos, sa, sb, quarter):
    w = x.shape[-1]
    reps = w // cos.shape[-1]
    if reps > 1:
        cos, sa, sb = (jnp.tile(t, (1, reps)) for t in (cos, sa, sb))
    return x * cos + pltpu.roll(x, quarter, 1) * sa + pltpu.roll(x, w - quarter, 1) * sb


def _mod_row(row0):
    return jnp.where(row0 < N_CTX, 0, 1 + (row0 - N_CTX) // DEC_SEQ)


def _ada_kernel(c_ref, w_ref, b_ref, o_ref):
    c = c_ref[...]
    s = c * _sigmoid(c)
    o_ref[...] = _dot(s.astype(BF16), w_ref[...].astype(BF16)) + b_ref[...]


def _ada(cond8, w_mod, b_mod):
    tn = 1024
    n = w_mod.shape[-1]
    return pl.pallas_call(
        _ada_kernel,
        grid=(DEPTH, n // tn),
        in_specs=[pl.BlockSpec((8, D_MODEL), lambda l, j: (0, 0)),
                  pl.BlockSpec((None, D_MODEL, tn), lambda l, j: (l, 0, j)),
                  pl.BlockSpec((None, 1, tn), lambda l, j: (l, 0, j))],
        out_specs=pl.BlockSpec((None, 8, tn), lambda l, j: (l, 0, j)),
        out_shape=jax.ShapeDtypeStruct((DEPTH, 8, n), F32),
        compiler_params=_params("arbitrary", "arbitrary"),
        name="ada",
    )(cond8, w_mod, b_mod.reshape(DEPTH, 1, n))


def _resnorm_kernel(*refs, has_res, has_next):
    refs = list(refs)
    xc_ref, xl_ref = refs[:2]
    refs = refs[2:]
    is_ctx = pl.program_id(0) < N_CTX // ROW_TILE
    x = jnp.where(is_ctx, xc_ref[...], xl_ref[...])
    if has_res:
        u_ref, gpost_ref, gate_ref = refs[:3]
        refs = refs[3:]
        x = x + gate_ref[...] * _rms(u_ref[...], gpost_ref[...])
    if has_next:
        gpre_ref, shift_ref, scale_ref = refs[:3]
        refs = refs[3:]
    if has_res:
        xco_ref, xlo_ref = refs[:2]
        refs = refs[2:]

        @pl.when(is_ctx)
        def _():
            xco_ref[...] = x

        @pl.when(jnp.logical_not(is_ctx))
        def _():
            xlo_ref[...] = x
    if has_next:
        ho_ref = refs.pop(0)
        h = _rms(x, gpre_ref[...]) * (1.0 + scale_ref[...]) + shift_ref[...]
        ho_ref[...] = h.astype(BF16)


def _resnorm(x_ctx, x_lat, mod, g_norm, *, res=None, nxt=None):
    tr = ROW_TILE
    nctx = N_CTX // tr
    row = pl.BlockSpec((tr, D_MODEL), lambda i: (i, 0))
    crow = pl.BlockSpec((tr, D_MODEL), lambda i: (jnp.minimum(i, nctx - 1), 0))
    lrow = pl.BlockSpec((tr, D_MODEL), lambda i: (jnp.maximum(i - nctx, 0), 0))

    def gspec(k):
        return pl.BlockSpec((None, 1, D_MODEL), lambda i: (k, 0, 0))

    def mspec(l, chunk):
        return pl.BlockSpec((None, 1, D_MODEL), lambda i: (l * 8 + _mod_row(i * tr), 0, chunk))

    args, specs, outs, ospecs = [x_ctx, x_lat], [crow, lrow], [], []
    if res is not None:
        u, gk, l, chunk = res
        args += [u, g_norm, mod]
        specs += [row, gspec(gk), mspec(l, chunk)]
        outs += [jax.ShapeDtypeStruct((N_CTX, D_MODEL), F32), jax.ShapeDtypeStruct((N_LAT, D_MODEL), F32)]
        ospecs += [crow, lrow]
    if nxt is not None:
        gk, l, chunk = nxt
        args += [g_norm, mod, mod]
        specs += [gspec(gk), mspec(l, chunk), mspec(l, chunk + 1)]
        outs.append(jax.ShapeDtypeStruct((N_TOK, D_MODEL), BF16))
        ospecs.append(row)
    return pl.pallas_call(
        functools.partial(_resnorm_kernel, has_res=res is not None, has_next=nxt is not None),
        grid=(N_TOK // tr,),
        in_specs=specs, out_specs=ospecs, out_shape=outs,
        compiler_params=_params("arbitrary"),
        name="resnorm",
    )(*args)


def _mm_kernel(x_ref, w_ref, *rest, nk, act, kscale, rope_quarter, w_t):
    o_ref = rest[-1]
    if nk == 1:
        if w_t:
            part = _dot_nt(x_ref[...].astype(BF16), w_ref[0].astype(BF16))
        else:
            part = _dot(x_ref[...].astype(BF16), w_ref[...].astype(BF16))
        if act == "relu2":
            part = jnp.square(jnp.maximum(part, 0.0))
        elif act == "sigmoid":
            part = _sigmoid(part)
        if kscale is not None:
            part = part * jnp.where(pl.program_id(1) >= kscale[0], kscale[1], 1.0)
        if rope_quarter is not None:
            cos_ref, sa_ref, sb_ref = rest[:3]
            part = _rope(part, cos_ref[...], sa_ref[...], sb_ref[...], rope_quarter)
        o_ref[...] = part.astype(o_ref.dtype)
        return

    @pl.when(pl.program_id(2) == 0)
    def _():
        o_ref[...] = jnp.zeros(o_ref.shape, o_ref.dtype)

    o_ref[...] += _dot(x_ref[...].astype(BF16), w_ref[...].astype(BF16))


def _mm(x, w, l, *, col0=0, ncols=None, out_dtype=F32, act=None, tm=2048, tn=512, tk=None,
        m=None, row_map=None, kscale=None, rope=None, w_t=False):
    kdim = w.shape[2] if w_t else w.shape[1]
    ncols = (w.shape[1] if w_t else w.shape[2]) - col0 if ncols is None else ncols
    m = x.shape[0] if m is None else m
    tk = kdim if tk is None else tk
    tm, tn = min(tm, m), min(tn, ncols)
    assert m % tm == 0 and ncols % tn == 0 and kdim % tk == 0
    nk = kdim // tk
    assert nk == 1 or (act is None and out_dtype == F32 and kscale is None and rope is None and not w_t)
    row_map = (lambda i: i) if row_map is None else row_map
    args = [x, w]
    if w_t:
        assert col0 % SUBLANES == 0 and tn % SUBLANES == 0
        wspec = pl.BlockSpec((pl.Element(1), pl.Element(tn), pl.Element(tk)),
                             lambda i, j, k: (l, pl.multiple_of(col0 + j * tn, SUBLANES), 0))
    else:
        assert col0 % tn == 0
        jb = col0 // tn
        wspec = pl.BlockSpec((None, tk, tn), lambda i, j, k: (l, k, jb + j))
    in_specs = [pl.BlockSpec((tm, tk), lambda i, j, k: (row_map(i), k)), wspec]
    if kscale is not None:
        assert kscale[0] % tn == 0
        kscale = (kscale[0] // tn, kscale[1])
    if rope is not None:
        tabs, quarter = rope
        assert tm == DEC_SEQ and tabs[0].shape == (DEC_SEQ, LANES)
        args += list(tabs)
        in_specs += [pl.BlockSpec((tm, LANES), lambda i, j, k: (0, 0))] * 3
    return pl.pallas_call(
        functools.partial(_mm_kernel, nk=nk, act=act, kscale=kscale,
                          rope_quarter=None if rope is None else rope[1], w_t=w_t),
        grid=(m // tm, ncols // tn, nk),
        in_specs=in_specs,
        out_specs=pl.BlockSpec((tm, tn), lambda i, j, k: (i, j)),
        out_shape=jax.ShapeDtypeStruct((m, ncols), out_dtype),
        compiler_params=_params("arbitrary", "arbitrary", "arbitrary"),
        name="mm",
    )(*args)


def _bm_kernel(a0_ref, a1_ref, a2_ref, w0_ref, w1_ref, w2_ref, g0_ref, g1_ref, g2_ref, o_ref):
    acc = g0_ref[...].astype(F32) * _dot(a0_ref[...], w0_ref[...].astype(BF16))
    acc = acc + g1_ref[...].astype(F32) * _dot(a1_ref[...], w1_ref[...].astype(BF16))
    acc = acc + g2_ref[...].astype(F32) * _dot(a2_ref[...], w2_ref[...].astype(BF16))
    o_ref[...] = acc.astype(BF16)


def _branch_merge(acts, ws, gates, l, *, tm=1024, tn=512):
    nj = D_MODEL // tn
    kdim = acts[0].shape[1]
    aspec = pl.BlockSpec((tm, kdim), lambda i, j: (i, 0))
    wspec = pl.BlockSpec((None, kdim, tn), lambda i, j: (l, 0, j))
    gspecs = [pl.BlockSpec((tm, tn), lambda i, j, b=b: (i, b * nj + j)) for b in range(3)]
    return pl.pallas_call(
        _bm_kernel,
        grid=(N_TOK // tm, nj),
        in_specs=[aspec] * 3 + [wspec] * 3 + gspecs,
        out_specs=pl.BlockSpec((tm, tn), lambda i, j: (i, j)),
        out_shape=jax.ShapeDtypeStruct((N_TOK, D_MODEL), BF16),
        compiler_params=_params("arbitrary", "arbitrary"),
        name="branch_merge",
    )(*acts, *ws, gates, gates, gates)


LRU_CT = 512


def _lru_kernel(*refs, T, has_h0, has_state):
    xr_ref, xg_ref, cw_ref, cb_ref, wa_ref, ba_ref, wx_ref, bx_ref, lam_ref = refs[:9]
    refs = refs[9:]
    if has_h0:
        h0_ref = refs[0]
        refs = refs[1:]
    if has_state:
        y_ref, st_ref = refs[2:4]
        refs = refs[4:]
    else:
        y_ref = refs[1]
        refs = refs[2:]
    x3_ref, p_ref, a_ref, b_ref, hl_ref, pc_ref, hn_ref = refs
    nblk = xr_ref.shape[1] // LRU_BS
    S = SUBLANES
    G = T // S
    H = CONV_W - 1

    pitch = G + S
    for n in range(nblk):
        for k in range(S):
            x3_ref[n, k * pitch:k * pitch + G, :] = xr_ref[k * G:(k + 1) * G, n * LRU_BS:(n + 1) * LRU_BS]

    def gather(p, _):
        r0 = pl.multiple_of((p + H) * S, S)
        for n in range(nblk):
            p_ref.at[n][pl.ds(r0, S), :] = x3_ref.at[n][pl.ds(p, S, stride=pitch), :]
        return 0

    lax.fori_loop(0, G, gather, 0, unroll=4)
    row = lax.broadcasted_iota(jnp.int32, (S, LRU_BS), 0)
    for n in range(nblk):
        for i in range(H):
            tail = p_ref[n, (G + i) * S:(G + i + 1) * S, :]
            p_ref[n, i * S:(i + 1) * S, :] = jnp.where(row == 0, 0.0, pltpu.roll(tail, 1, 0))
            head = p_ref[n, (H + i) * S:(H + i + 1) * S, :]
            p_ref[n, (H + G + i) * S:(H + G + i + 1) * S, :] = jnp.where(row == S - 1, 0.0,
                                                                        pltpu.roll(head, S - 1, 0))

    cmats = []
    for d in range(2):
        for n in range(nblk):
            cs = slice(n * LRU_BS, (n + 1) * LRU_BS)
            xc = jnp.zeros((T, LRU_BS), F32) + cb_ref[d, :, cs]
            for j in range(CONV_W):
                off = j if d == 0 else 2 * H - j
                xc = xc + cw_ref[d, j:j + 1, cs] * p_ref[n, off * S:off * S + T, :]
            xcb = xc.astype(BF16)
            t_r = jnp.tanh(_dot(xcb, (0.5 * wa_ref[d, n]).astype(BF16)) + 0.5 * ba_ref[d, :, cs])
            t_i = jnp.tanh(_dot(xcb, (0.5 * wx_ref[d, n]).astype(BF16)) + 0.5 * bx_ref[d, :, cs])
            half_c = (-0.5 * LRU_C) * _softplus(-lam_ref[d, :, cs])
            a = jnp.exp(half_c + half_c * t_r)
            a_ref[n] = a
            b_ref[n] = (0.5 * jnp.sqrt(1.0 - a * a)) * (xc + t_i * xc)

        def body(it, carry, d=d):
            p = it if d == 0 else G - 1 - it
            r0 = pl.multiple_of(p * S, S)
            out = []
            for n in range(nblk):
                h, pc = carry[n]
                av = a_ref.at[n][pl.ds(r0, S), :]
                h = av * h + b_ref.at[n][pl.ds(r0, S), :]
                pc = av * pc
                hl_ref.at[d, n][pl.ds(r0, S), :] = h
                pc_ref.at[d, n][pl.ds(r0, S), :] = pc
                out.append((h, pc))
            return tuple(out)

        init = tuple((jnp.zeros((S, LRU_BS), F32), jnp.ones((S, LRU_BS), F32)) for _ in range(nblk))
        ends = lax.fori_loop(0, G, body, init, unroll=4)

        last_d, cmat_d = [], []
        for n in range(nblk):
            cs = slice(n * LRU_BS, (n + 1) * LRU_BS)
            h_end, pc_end = ends[n]
            c = h0_ref[d, :, cs] if has_h0 else jnp.zeros((1, LRU_BS), F32)
            rows = [None] * S
            for k in (range(S) if d == 0 else range(S - 1, -1, -1)):
                rows[k] = c
                c = h_end[k:k + 1, :] + pc_end[k:k + 1, :] * c
            last_d.append(c)
            cmat_d.append(jnp.concatenate(rows, axis=0))
        cmats.append(cmat_d)
        if has_state:
            for n in range(nblk):
                st_ref[d:d + 1, n * LRU_BS:(n + 1) * LRU_BS] = last_d[n]

    def scatter(p, _):
        r0 = pl.multiple_of(p * S, S)
        for n in range(nblk):
            h = (hl_ref.at[0, n][pl.ds(r0, S), :] + pc_ref.at[0, n][pl.ds(r0, S), :] * cmats[0][n]
                 + hl_ref.at[1, n][pl.ds(r0, S), :] + pc_ref.at[1, n][pl.ds(r0, S), :] * cmats[1][n])
            hn_ref.at[n][pl.ds(p, S, stride=pitch), :] = h
        return 0

    lax.fori_loop(0, G, scatter, 0, unroll=4)

    for n in range(nblk):
        cs = slice(n * LRU_BS, (n + 1) * LRU_BS)
        for k in range(S):
            xg = xg_ref[k * G:(k + 1) * G, cs]
            gelu = 0.5 * xg * (1.0 + jnp.tanh(0.7978845608028654 * (xg + 0.044715 * (xg * xg * xg))))
            y_ref[k * G:(k + 1) * G, cs] = (hn_ref[n, k * pitch:k * pitch + G, :] * gelu).astype(BF16)


def _lru(proj_a, lw, l, *, T, nb, blk0, prev, h0=None, st_prev=None):
    ct = LRU_CT
    ncb = D_RNN // ct
    bpc = ct // LRU_BS
    in_specs = [
        pl.BlockSpec((T, ct), lambda b, c: (blk0 + b, c)),
        pl.BlockSpec((T, ct), lambda b, c: (blk0 + b, ncb + c)),
        pl.BlockSpec((None, 2, CONV_W, ct), lambda b, c: (l, 0, 0, c)),
        pl.BlockSpec((None, 2, 1, ct), lambda b, c: (l, 0, 0, c)),
        pl.BlockSpec((None, 2, bpc, LRU_BS, LRU_BS), lambda b, c: (l, 0, c, 0, 0)),
        pl.BlockSpec((None, 2, 1, ct), lambda b, c: (l, 0, 0, c)),
        pl.BlockSpec((None, 2, bpc, LRU_BS, LRU_BS), lambda b, c: (l, 0, c, 0, 0)),
        pl.BlockSpec((None, 2, 1, ct), lambda b, c: (l, 0, 0, c)),
        pl.BlockSpec((None, 2, 1, ct), lambda b, c: (l, 0, 0, c)),
    ]
    args = [proj_a, proj_a, lw["conv_w"], lw["conv_b"], lw["wa"], lw["ba"], lw["wx"], lw["bx"], lw["lam"]]
    if h0 is not None:
        in_specs.append(pl.BlockSpec((None, None, 2, 1, ct), lambda b, c: (b, l, 0, 0, c)))
        args.append(h0)
    has_state = st_prev is not None
    aliases = {len(args): 0}
    in_specs.append(pl.BlockSpec(memory_space=pl.ANY))
    args.append(prev)
    out_specs = [pl.BlockSpec((T, ct), lambda b, c: (blk0 + b, c))]
    out_shape = [jax.ShapeDtypeStruct(prev.shape, prev.dtype)]
    if has_state:
        aliases[len(args)] = 1
        in_specs.append(pl.BlockSpec(memory_space=pl.ANY))
        args.append(st_prev)
        out_specs.append(pl.BlockSpec((None, None, 2, ct), lambda b, c: (b, l, 0, c)))
        out_shape.append(jax.ShapeDtypeStruct(st_prev.shape, st_prev.dtype))
    return pl.pallas_call(
        functools.partial(_lru_kernel, T=T, has_h0=h0 is not None, has_state=has_state),
        grid=(nb, ncb),
        in_specs=in_specs,
        out_specs=out_specs,
        out_shape=out_shape,
        input_output_aliases=aliases,
        scratch_shapes=[pltpu.VMEM((bpc, T + SUBLANES * SUBLANES, LRU_BS), F32),
                        pltpu.VMEM((bpc, T + 2 * (CONV_W - 1) * SUBLANES, LRU_BS), F32),
                        pltpu.VMEM((bpc, T, LRU_BS), F32), pltpu.VMEM((bpc, T, LRU_BS), F32),
                        pltpu.VMEM((2, bpc, T, LRU_BS), F32), pltpu.VMEM((2, bpc, T, LRU_BS), F32),
                        pltpu.VMEM((bpc, T + SUBLANES * SUBLANES, LRU_BS), F32)],
        compiler_params=_params("arbitrary", "arbitrary"),
        name="lru",
    )(*args)


def _mla_prep_kernel(p_ref, gq_ref, gkv_ref, cos_ref, sa_ref, sb_ref, *rest):
    cq_ref, ckvb_ref, kr_ref, ckv_out_ref, kr_out_ref = rest[-5:]
    i = pl.program_id(0)
    cq_ref[...] = _rms(p_ref[:, 0:MLA_Q_RANK], gq_ref[...]).astype(BF16)
    ckv = _rms(p_ref[:, MLA_Q_RANK:MLA_Q_RANK + MLA_KV_RANK], gkv_ref[...])
    ckvb_ref[...] = ckv.astype(BF16)
    k0 = MLA_Q_RANK + MLA_KV_RANK
    kr = p_ref[:, k0:k0 + LANES]

    @pl.when(i < N_CTX // ROW_TILE)
    def _():
        kr_ref[...] = kr
        ckv_out_ref[...] = ckv
        kr_out_ref[...] = kr[:, 0:MLA_ROPE]

    @pl.when(i >= N_CTX // ROW_TILE)
    def _():
        kr_ref[...] = _rope(kr, cos_ref[...], sa_ref[...], sb_ref[...], MLA_ROPE // 4)


def _lat_tile(i):
    return (jnp.maximum(i - N_CTX // ROW_TILE, 0)) % (DEC_SEQ // ROW_TILE)


def _mla_prep(proj_a, gq, gkv, l, tab, caches):
    tr = ROW_TILE
    assert tr == SEQ
    last_ctx = BATCH - 1
    tspec = pl.BlockSpec((tr, LANES), lambda i: (_lat_tile(i), 0))
    args = [proj_a, gq, gkv, *tab]
    in_specs = [pl.BlockSpec((tr, 1024), lambda i: (i, 2)),
                pl.BlockSpec((None, 1, MLA_Q_RANK), lambda i: (l, 0, 0)),
                pl.BlockSpec((None, 1, MLA_KV_RANK), lambda i: (l, 0, 0)),
                tspec, tspec, tspec]
    aliases = {len(args): 3, len(args) + 1: 4}
    args += list(caches)
    in_specs += [pl.BlockSpec(memory_space=pl.ANY)] * 2
    return pl.pallas_call(
        _mla_prep_kernel,
        grid=(N_TOK // tr,),
        in_specs=in_specs,
        out_specs=[pl.BlockSpec((tr, MLA_Q_RANK), lambda i: (i, 0)),
                   pl.BlockSpec((tr, MLA_KV_RANK), lambda i: (i, 0)),
                   pl.BlockSpec((tr, LANES), lambda i: (i, 0)),
                   pl.BlockSpec((None, None, SEQ, MLA_KV_RANK), lambda i: (jnp.minimum(i, last_ctx), l, 0, 0)),
                   pl.BlockSpec((None, None, SEQ, MLA_ROPE), lambda i: (jnp.minimum(i, last_ctx), l, 0, 0))],
        out_shape=[jax.ShapeDtypeStruct((N_TOK, MLA_Q_RANK), BF16),
                   jax.ShapeDtypeStruct((N_TOK, MLA_KV_RANK), BF16),
                   jax.ShapeDtypeStruct((N_TOK, LANES), F32),
                   jax.ShapeDtypeStruct((BATCH, DEPTH, SEQ, MLA_KV_RANK), F32),
                   jax.ShapeDtypeStruct((BATCH, DEPTH, SEQ, MLA_ROPE), F32)],
        input_output_aliases=aliases,
        compiler_params=_params("arbitrary"),
        name="mla_prep",
    )(*args)


def _attn_kernel(*refs, lat):
    if lat:
        (q_ref, kvn_ref, krn_ref, kvc_ref, krc_ref, cos_ref, sa_ref, sb_ref, _,
         o_ref, kcat_ref, vcat_ref) = refs
    else:
        q_ref, kvn_ref, krn_ref, _, o_ref, kcat_ref = refs
    scale = (MLA_NOPE + MLA_ROPE) ** -0.5
    s_new = kvn_ref.shape[0]
    off = PAST_LEN if lat else 0

    def build():
        lane = lax.broadcasted_iota(jnp.int32, (s_new, LANES), 1)
        kr_new = jnp.where(lane < MLA_ROPE, krn_ref[...], 0.0).astype(BF16)
        for h in range(MLA_HEADS):
            c0 = h * MLA_QH
            kcat_ref[h, off:off + s_new, 0:MLA_NOPE] = kvn_ref[:, c0:c0 + MLA_NOPE]
            kcat_ref[h, off:off + s_new, MLA_NOPE:MLA_QH] = kr_new
            if lat:
                kcat_ref[h, 0:off, 0:MLA_NOPE] = kvc_ref[:, c0:c0 + MLA_NOPE]
                kcat_ref[h, 0:off, MLA_NOPE:MLA_NOPE + MLA_ROPE] = krc_ref[...].astype(BF16)
                kcat_ref[h, 0:off, MLA_NOPE + MLA_ROPE:MLA_QH] = jnp.zeros((off, MLA_QH - MLA_NOPE - MLA_ROPE), BF16)
                vcat_ref[0:off, h * MLA_V:(h + 1) * MLA_V] = kvc_ref[:, c0 + MLA_NOPE:c0 + MLA_QH]
                vcat_ref[off:off + s_new, h * MLA_V:(h + 1) * MLA_V] = kvn_ref[:, c0 + MLA_NOPE:c0 + MLA_QH]

    if lat:
        pl.when(pl.program_id(1) == 0)(build)
    else:
        build()

    for h in range(MLA_HEADS):
        c0 = h * MLA_QH
        qn = q_ref[:, c0:c0 + MLA_NOPE].astype(F32)
        qr = q_ref[:, c0 + MLA_NOPE:c0 + MLA_QH].astype(F32)
        if lat:
            qr = _rope(qr, cos_ref[...], sa_ref[...], sb_ref[...], MLA_ROPE // 4)
        qh = (jnp.concatenate([qn, qr], axis=1) * scale).astype(BF16)
        s = _dot_nt(qh, kcat_ref[h])
        p = jnp.exp(s - jnp.max(s, axis=-1, keepdims=True))
        den = jnp.sum(p, axis=-1, keepdims=True)
        v = vcat_ref[:, h * MLA_V:(h + 1) * MLA_V] if lat else kvn_ref[:, c0 + MLA_NOPE:c0 + MLA_QH]
        o_ref[:, h * MLA_V:(h + 1) * MLA_V] = (_dot(p.astype(BF16), v) / den).astype(BF16)


def _attn_ctx(q, kv, kr, prev):
    t = SEQ
    return pl.pallas_call(
        functools.partial(_attn_kernel, lat=False),
        grid=(BATCH,),
        in_specs=[pl.BlockSpec((t, MLA_HEADS * MLA_QH), lambda b: (b, 0)),
                  pl.BlockSpec((t, MLA_HEADS * MLA_QH), lambda b: (b, 0)),
                  pl.BlockSpec((t, LANES), lambda b: (b, 0)),
                  pl.BlockSpec(memory_space=pl.ANY)],
        out_specs=pl.BlockSpec((t, MLA_HEADS * MLA_V), lambda b: (b, 0)),
        out_shape=jax.ShapeDtypeStruct(prev.shape, prev.dtype),
        input_output_aliases={3: 0},
        scratch_shapes=[pltpu.VMEM((MLA_HEADS, t, MLA_QH), BF16)],
        compiler_params=_params("arbitrary"),
        name="attn_ctx",
    )(q, kv, kr, prev)


def _attn_lat(q, kv, kr, kv_cache, cache_krope, l, tab, prev):
    tq = ROW_TILE
    nq = DEC_SEQ // tq
    blk0 = N_CTX // DEC_SEQ
    qblk0 = N_CTX // tq
    tspec = pl.BlockSpec((tq, LANES), lambda b, i: (i, 0))
    return pl.pallas_call(
        functools.partial(_attn_kernel, lat=True),
        grid=(DEC_BATCH, nq),
        in_specs=[pl.BlockSpec((tq, MLA_HEADS * MLA_QH), lambda b, i: (qblk0 + b * nq + i, 0)),
                  pl.BlockSpec((DEC_SEQ, MLA_HEADS * MLA_QH), lambda b, i: (blk0 + b, 0)),
                  pl.BlockSpec((DEC_SEQ, LANES), lambda b, i: (blk0 + b, 0)),
                  pl.BlockSpec((PAST_LEN, MLA_HEADS * MLA_QH), lambda b, i: (b, 0)),
                  pl.BlockSpec((None, None, PAST_LEN, MLA_ROPE), lambda b, i: (b, l, 0, 0)),
                  tspec, tspec, tspec, pl.BlockSpec(memory_space=pl.ANY)],
        out_specs=pl.BlockSpec((tq, MLA_HEADS * MLA_V), lambda b, i: (qblk0 + b * nq + i, 0)),
        out_shape=jax.ShapeDtypeStruct((N_TOK, MLA_HEADS * MLA_V), BF16),
        input_output_aliases={8: 0},
        scratch_shapes=[pltpu.VMEM((MLA_HEADS, PAST_LEN + DEC_SEQ, MLA_QH), BF16),
                        pltpu.VMEM((PAST_LEN + DEC_SEQ, MLA_HEADS * MLA_V), BF16)],
        compiler_params=_params("arbitrary", "arbitrary"),
        name="attn_lat",
    )(q, kv, kr, kv_cache, cache_krope, *tab, prev)


def _log_sigmoid(x):
    return -_softplus(-x)


def _ret_kernel(*refs, lat, T):
    if lat:
        q_ref, k_ref, v_ref, rg_ref, dec_ref, s0_ref, _, o_ref, decay_ref = refs
    else:
        q_ref, k_ref, v_ref, rg_ref, dec_ref = refs[:5]
        o_ref, st_ref, decay_ref = refs[-3:]
    tq = q_ref.shape[0]
    t0 = pl.program_id(0) * tq if lat else 0
    batch_id = pl.program_id(1) if lat else pl.program_id(0)
    lg = _log_sigmoid(dec_ref[...])
    tcol = (t0 + lax.broadcasted_iota(jnp.int32, (tq, 1), 0)).astype(F32)

    @pl.when(batch_id == 0)
    def _():
        rows = (t0 + lax.broadcasted_iota(jnp.int32, (tq, T), 0)).astype(F32)
        cols = lax.broadcasted_iota(jnp.int32, (tq, T), 1).astype(F32)
        diff = rows - cols
        fwd = diff >= 0.0
        dist = jnp.abs(diff)
        for h in range(RET_HEADS):
            one_sided = jnp.exp(jnp.where(fwd, lg[0:1, h:h + 1], lg[1:2, h:h + 1]) * dist)
            decay_ref[h] = jnp.where(dist == 0.0, 2.0, one_sided)

    for h in range(RET_HEADS):
        cs = slice(h * RET_DK, (h + 1) * RET_DK)
        lgf = lg[0:1, h:h + 1]
        lgb = lg[1:2, h:h + 1]
        q = q_ref[:, cs]
        k = k_ref[:, cs]
        v = v_ref[:, cs]
        sc = _dot_nt(q, k) * decay_ref[h]
        o = _dot(sc.astype(BF16), v)
        if lat:
            qf = q.astype(F32)
            o = o + _dot((qf * jnp.exp(lgf * (tcol + 1.0))).astype(BF16), s0_ref[0, h].astype(BF16))
            o = o + _dot((qf * jnp.exp(lgb * (T - tcol))).astype(BF16), s0_ref[1, h].astype(BF16))
        else:
            kf = k.astype(F32)
            st_ref[0, h] = _dot_tn((kf * jnp.exp(lgf * (T - 1.0 - tcol))).astype(BF16), v)
            st_ref[1, h] = _dot_tn((kf * jnp.exp(lgb * tcol)).astype(BF16), v)
        mu = jnp.mean(o, axis=-1, keepdims=True)
        oc = o - mu
        var = jnp.mean(oc * oc, axis=-1, keepdims=True)
        rg = rg_ref[:, cs].astype(F32)
        o_ref[:, cs] = (oc * lax.rsqrt(var + EPS) * (rg * _sigmoid(rg))).astype(BF16)


def _ret_ctx(qk, v, rg, decay, l, prev, st_prev):
    t = SEQ
    w = RET_HEADS * RET_DK
    return pl.pallas_call(
        functools.partial(_ret_kernel, lat=False, T=t),
        grid=(BATCH,),
        in_specs=[pl.BlockSpec((t, w), lambda b: (b, 0)),
                  pl.BlockSpec((t, w), lambda b: (b, 1)),
                  pl.BlockSpec((t, w), lambda b: (b, 0)),
                  pl.BlockSpec((t, w), lambda b: (b, 0)),
                  pl.BlockSpec((None, 2, RET_HEADS), lambda b: (l, 0, 0)),
                  pl.BlockSpec(memory_space=pl.ANY), pl.BlockSpec(memory_space=pl.ANY)],
        out_specs=[pl.BlockSpec((t, w), lambda b: (b, 0)),
                   pl.BlockSpec((None, None, 2, RET_HEADS, RET_DK, RET_DV), lambda b: (b, l, 0, 0, 0, 0))],
        out_shape=[jax.ShapeDtypeStruct(prev.shape, prev.dtype),
                   jax.ShapeDtypeStruct(st_prev.shape, st_prev.dtype)],
        input_output_aliases={5: 0, 6: 1},
        scratch_shapes=[pltpu.VMEM((RET_HEADS, t, t), F32)],
        compiler_params=_params("arbitrary"),
        name="ret_ctx",
    )(qk, qk, v, rg, decay, prev, st_prev)


def _ret_lat(qk, v, rg, decay, state_ret, l, prev):
    tq = ROW_TILE
    t = DEC_SEQ
    nq = t // tq
    w = RET_HEADS * RET_DK
    blk0 = N_CTX // t
    qblk0 = N_CTX // tq
    return pl.pallas_call(
        functools.partial(_ret_kernel, lat=True, T=t),
        grid=(nq, DEC_BATCH),
        in_specs=[pl.BlockSpec((tq, w), lambda i, b: (b * nq + i, 0)),
                  pl.BlockSpec((t, w), lambda i, b: (b, 1)),
                  pl.BlockSpec((t, w), lambda i, b: (blk0 + b, 0)),
                  pl.BlockSpec((tq, w), lambda i, b: (qblk0 + b * nq + i, 0)),
                  pl.BlockSpec((None, 2, RET_HEADS), lambda i, b: (l, 0, 0)),
                  pl.BlockSpec((None, None, 2, RET_HEADS, RET_DK, RET_DV), lambda i, b: (b, l, 0, 0, 0, 0)),
                  pl.BlockSpec(memory_space=pl.ANY)],
        out_specs=pl.BlockSpec((tq, w), lambda i, b: (qblk0 + b * nq + i, 0)),
        out_shape=jax.ShapeDtypeStruct((N_TOK, w), BF16),
        scratch_shapes=[pltpu.VMEM((RET_HEADS, tq, t), F32)],
        input_output_aliases={6: 0},
        compiler_params=_params("arbitrary", "arbitrary"),
        name="ret_lat",
    )(qk, qk, v, rg, decay, state_ret, prev)


def _rope_tables(dim):
    half = dim // 2
    quarter = half // 2
    t = jnp.arange(DEC_SEQ)
    row = (t // GRID_W).astype(F32)
    col = (t % GRID_W).astype(F32)
    inv = ROPE_BASE ** (-jnp.arange(0, half, 2, dtype=F32) / half)
    zeros = jnp.zeros((DEC_SEQ, quarter), F32)
    cos_parts, sa_parts, sb_parts = [], [], []
    for pos in (row, col):
        ang = pos[:, None] * inv[None, :]
        c, s = jnp.cos(ang), jnp.sin(ang)
        cos_parts += [c, c]
        sa_parts += [zeros, s]
        sb_parts += [-s, zeros]
    padw = LANES - dim

    def build(parts):
        return jnp.concatenate(parts + ([jnp.zeros((DEC_SEQ, padw), F32)] if padw else []), axis=-1)

    return build(cos_parts), build(sa_parts), build(sb_parts)


def kernel(x_prompt, x_sample, c, cache_mla_ckv, cache_mla_krope, state_lru, state_ret, c_ctx, w_mod, b_mod,
           g_norm, w_in, lru_conv_w, lru_conv_b, lru_wa, lru_ba, lru_wx, lru_bx, lru_lam, mla_gq, mla_gkv,
           mla_wuq, mla_wukv, ret_decay, w_br_lru, w_br_mla, w_br_ret, w_out, w_ff1, w_ff2):
    x_ctx = x_prompt.reshape(N_CTX, D_MODEL)
    x_lat = x_sample.reshape(N_LAT, D_MODEL)
    cond8 = jnp.concatenate([c_ctx[None, :], c, jnp.zeros((8 - 1 - DEC_BATCH, D_MODEL), F32)], axis=0)
    mod = _ada(cond8, w_mod, b_mod).reshape(DEPTH * 8, 1, 6 * D_MODEL)
    g_all = g_norm.reshape(DEPTH * 4, 1, D_MODEL)

    w_in_t = jnp.swapaxes(w_in, 1, 2)
    wuq = jnp.pad(mla_wuq.reshape(DEPTH, MLA_Q_RANK, MLA_HEADS, MLA_NOPE + MLA_ROPE),
                  ((0, 0), (0, 0), (0, 0), (0, MLA_QH - MLA_NOPE - MLA_ROPE)))
    wuq = wuq.reshape(DEPTH, MLA_Q_RANK, MLA_HEADS * MLA_QH)
    lw = {"conv_w": lru_conv_w, "conv_b": lru_conv_b.reshape(DEPTH, 2, 1, D_RNN), "wa": lru_wa,
          "ba": lru_ba.reshape(DEPTH, 2, 1, D_RNN), "wx": lru_wx, "bx": lru_bx.reshape(DEPTH, 2, 1, D_RNN),
          "lam": lru_lam.reshape(DEPTH, 2, 1, D_RNN)}
    gq = mla_gq.reshape(DEPTH, 1, MLA_Q_RANK)
    gkv = mla_gkv.reshape(DEPTH, 1, MLA_KV_RANK)
    h0_lat = state_lru.reshape(DEC_BATCH, DEPTH, 2, 1, D_RNN)
    cache_ckv_rows = cache_mla_ckv.reshape(DEC_BATCH * DEPTH * PAST_LEN, MLA_KV_RANK)
    tab_mla = _rope_tables(MLA_ROPE)
    tab_ret = _rope_tables(RET_DK)
    wq = RET_HEADS * RET_DK

    caches = (jnp.zeros((BATCH, DEPTH, SEQ, MLA_KV_RANK), F32), jnp.zeros((BATCH, DEPTH, SEQ, MLA_ROPE), F32))
    st_lru = jnp.zeros((BATCH, DEPTH, 2, D_RNN), F32)
    st_ret = jnp.zeros((BATCH, DEPTH, 2, RET_HEADS, RET_DK, RET_DV), F32)
    y_lru = jnp.zeros((N_TOK, D_RNN), BF16)
    y_mla = jnp.zeros((N_TOK, MLA_HEADS * MLA_V), BF16)
    y_ret = jnp.zeros((N_TOK, RET_HEADS * RET_DV), BF16)

    h, = _resnorm(x_ctx, x_lat, mod, g_all, nxt=(0, 0, 0))
    for l in range(DEPTH):
        proj_a = _mm(h, w_in_t, l, col0=0, ncols=3072, w_t=True)
        kscale = (wq, RET_DK ** -0.5)
        qk_ctx = _mm(h, w_in_t, l, col0=COL_TAIL, ncols=2 * wq, out_dtype=BF16, m=N_CTX, kscale=kscale,
                     w_t=True)
        qk_lat = _mm(h, w_in_t, l, col0=COL_TAIL, ncols=2 * wq, out_dtype=BF16, m=N_LAT, tm=DEC_SEQ,
                     row_map=lambda i: N_CTX // DEC_SEQ + i, kscale=kscale,
                     rope=(tab_ret, RET_DK // 4), w_t=True)
        proj_v = _mm(h, w_in_t, l, col0=COL_TAIL + 2 * wq, ncols=wq, out_dtype=BF16, w_t=True)
        proj_rg = _mm(h, w_in_t, l, col0=COL_TAIL + 3 * wq, ncols=wq, out_dtype=BF16, w_t=True)
        gates = _mm(h, w_in_t, l, col0=COL_TAIL + 4 * wq, ncols=3 * D_MODEL, out_dtype=BF16, act="sigmoid",
                    w_t=True)

        y_lru, st_lru = _lru(proj_a, lw, l, T=SEQ, nb=BATCH, blk0=0, prev=y_lru, st_prev=st_lru)
        y_lru, = _lru(proj_a, lw, l, T=DEC_SEQ, nb=DEC_BATCH, blk0=N_CTX // DEC_SEQ, h0=h0_lat, prev=y_lru)

        cqn, ckvb, kr, *caches = _mla_prep(proj_a, gq, gkv, l, tab_mla, caches)
        q = _mm(cqn, wuq, l, out_dtype=BF16)
        kv = _mm(ckvb, mla_wukv, l, out_dtype=BF16)
        kv_cache = _mm(cache_ckv_rows, mla_wukv, l, out_dtype=BF16, tm=PAST_LEN, m=DEC_BATCH * PAST_LEN,
                       row_map=lambda i, l=l: i * DEPTH + l)
        y_mla = _attn_ctx(q, kv, kr, y_mla)
        y_mla = _attn_lat(q, kv, kr, kv_cache, cache_mla_krope, l, tab_mla, y_mla)

        y_ret, st_ret = _ret_ctx(qk_ctx, proj_v, proj_rg, ret_decay, l, y_ret, st_ret)
        y_ret = _ret_lat(qk_lat, proj_v, proj_rg, ret_decay, state_ret, l, y_ret)

        merged = _branch_merge((y_lru, y_mla, y_ret), (w_br_lru, w_br_mla, w_br_ret), gates, l)
        u = _mm(merged, w_out, l)
        x_ctx, x_lat, h2 = _resnorm(x_ctx, x_lat, mod, g_all, res=(u, l * 4 + 1, l, 2), nxt=(l * 4 + 2, l, 3))
        ff = _mm(h2, w_ff1, l, out_dtype=BF16, act="relu2")
        y = _mm(ff, w_ff2, l, tn=1024, tk=1024)
        if l + 1 < DEPTH:
            x_ctx, x_lat, h = _resnorm(x_ctx, x_lat, mod, g_all, res=(y, l * 4 + 3, l, 5),
                                       nxt=((l + 1) * 4, l + 1, 0))
        else:
            x_ctx, x_lat = _resnorm(x_ctx, x_lat, mod, g_all, res=(y, l * 4 + 3, l, 5))

    new_ckv, new_krope = caches
    return (x_ctx.reshape(BATCH, SEQ, D_MODEL), x_lat.reshape(DEC_BATCH, DEC_SEQ, D_MODEL),
            new_ckv, new_krope, st_lru, st_ret)
```

```python
import functools

import jax
import jax.numpy as jnp
from jax import lax
from jax.experimental import pallas as pl
from jax.experimental.pallas import tpu as pltpu

F32 = jnp.float32
BF16 = jnp.bfloat16

D_MODEL = 2048
BATCH = 16
SEQ = 256
DEPTH = 4
DEC_BATCH = 4
DEC_SEQ = 1024
PAST_LEN = 256
GRID_W = 64
EPS = 1e-6
ROPE_BASE = 10000.0
D_RNN = D_MODEL // 2
LRU_BLOCKS = 8
LRU_BS = D_RNN // LRU_BLOCKS
CONV_W = 4
LRU_C = 8.0
MLA_HEADS = 8
MLA_NOPE = 128
MLA_ROPE = 64
MLA_V = 128
MLA_Q_RANK = D_MODEL // 4
MLA_KV_RANK = D_MODEL // 8
RET_HEADS = 8
RET_DK = 128
RET_DV = 128
D_FF = 4 * D_MODEL

N_CTX = BATCH * SEQ
N_LAT = DEC_BATCH * DEC_SEQ
N_TOK = N_CTX + N_LAT
COL_TAIL = 2 * D_RNN + MLA_Q_RANK + MLA_KV_RANK + MLA_ROPE
MLA_QH = 256
MLA_SLAB = 1024
GELU_TANH_SCALE = 0.7978845608028654
GELU_TANH_CUBIC = 0.044715

LANES = 128
SUBLANES = 8
VMEM_LIMIT = 56 * 1024 * 1024
ROW_TILE = 256
RES_TILE = 512


def _params(*sem):
    return pltpu.CompilerParams(dimension_semantics=sem, vmem_limit_bytes=VMEM_LIMIT)


def _sigmoid(x):
    return 0.5 * (1.0 + jnp.tanh(0.5 * x))


def _softplus(x):
    return jnp.maximum(x, 0.0) + jnp.log(1.0 + jnp.exp(-jnp.abs(x)))


def _rms(x, g):
    return x * lax.rsqrt(jnp.mean(x * x, axis=-1, keepdims=True) + EPS) * g


def _dot(a, b):
    return jnp.dot(a, b, preferred_element_type=F32)


def _dot_nt(a, b):
    return lax.dot_general(a, b, (((1,), (1,)), ((), ())), preferred_element_type=F32)


def _dot_tn(a, b):
    return lax.dot_general(a, b, (((0,), (0,)), ((), ())), preferred_element_type=F32)


def _rope(x, cos, sa, sb, quarter):
    w = x.shape[-1]
    reps = w // cos.shape[-1]
    if reps > 1:
        cos, sa, sb = (jnp.tile(t, (1, reps)) for t in (cos, sa, sb))
    return x * cos + pltpu.roll(x, quarter, 1) * sa + pltpu.roll(x, w - quarter, 1) * sb


def _mod_row(row0):
    return jnp.where(row0 < N_CTX, 0, 1 + (row0 - N_CTX) // DEC_SEQ)


def _ada_kernel(c_ref, w_ref, b_ref, o_ref):
    c = c_ref[...]
    s = c * _sigmoid(c)
    o_ref[...] = _dot(s.astype(BF16), w_ref[...].astype(BF16)) + b_ref[...]


def _ada(cond8, w_mod, b_mod):
    tn = 1024
    n = w_mod.shape[-1]
    return pl.pallas_call(
        _ada_kernel,
        grid=(DEPTH, n // tn),
        in_specs=[pl.BlockSpec((8, D_MODEL), lambda l, j: (0, 0)),
                  pl.BlockSpec((None, D_MODEL, tn), lambda l, j: (l, 0, j)),
                  pl.BlockSpec((None, 1, tn), lambda l, j: (l, 0, j))],
        out_specs=pl.BlockSpec((None, 8, tn), lambda l, j: (l, 0, j)),
        out_shape=jax.ShapeDtypeStruct((DEPTH, 8, n), F32),
        compiler_params=_params("arbitrary", "arbitrary"),
        name="ada",
    )(cond8, w_mod, b_mod.reshape(DEPTH, 1, n))


def _resnorm_kernel(*refs, has_res, has_next):
    refs = list(refs)
    xc_ref, xl_ref = refs[:2]
    refs = refs[2:]
    is_ctx = pl.program_id(0) < N_CTX // RES_TILE
    x = jnp.where(is_ctx, xc_ref[...], xl_ref[...])
    if has_res:
        u_ref, gpost_ref, gate_ref = refs[:3]
        refs = refs[3:]
        x = x + gate_ref[...] * _rms(u_ref[...], gpost_ref[...])
    if has_next:
        gpre_ref, shift_ref, scale_ref = refs[:3]
        refs = refs[3:]
    if has_res:
        xco_ref, xlo_ref = refs[:2]
        refs = refs[2:]

        @pl.when(is_ctx)
        def _():
            xco_ref[...] = x

        @pl.when(jnp.logical_not(is_ctx))
        def _():
            xlo_ref[...] = x
    if has_next:
        ho_ref = refs.pop(0)
        h = _rms(x, gpre_ref[...]) * (1.0 + scale_ref[...]) + shift_ref[...]
        ho_ref[...] = h.astype(BF16)


def _resnorm(x_ctx, x_lat, mod, g_norm, *, res=None, nxt=None):
    tr = RES_TILE
    nctx = N_CTX // tr
    row = pl.BlockSpec((tr, D_MODEL), lambda i: (i, 0))
    crow = pl.BlockSpec((tr, D_MODEL), lambda i: (jnp.minimum(i, nctx - 1), 0))
    lrow = pl.BlockSpec((tr, D_MODEL), lambda i: (jnp.maximum(i - nctx, 0), 0))

    def gspec(k):
        return pl.BlockSpec((None, 1, D_MODEL), lambda i: (k, 0, 0))

    def mspec(l, chunk):
        return pl.BlockSpec((None, 1, D_MODEL), lambda i: (l * 8 + _mod_row(i * tr), 0, chunk))

    args, specs, outs, ospecs = [x_ctx, x_lat], [crow, lrow], [], []
    if res is not None:
        u, gk, l, chunk = res
        args += [u, g_norm, mod]
        specs += [row, gspec(gk), mspec(l, chunk)]
        outs += [jax.ShapeDtypeStruct((N_CTX, D_MODEL), F32), jax.ShapeDtypeStruct((N_LAT, D_MODEL), F32)]
        ospecs += [crow, lrow]
    if nxt is not None:
        gk, l, chunk = nxt
        args += [g_norm, mod, mod]
        specs += [gspec(gk), mspec(l, chunk), mspec(l, chunk + 1)]
        outs.append(jax.ShapeDtypeStruct((N_TOK, D_MODEL), BF16))
        ospecs.append(row)
    return pl.pallas_call(
        functools.partial(_resnorm_kernel, has_res=res is not None, has_next=nxt is not None),
        grid=(N_TOK // tr,),
        in_specs=specs, out_specs=ospecs, out_shape=outs,
        compiler_params=_params("arbitrary"),
        name="resnorm",
    )(*args)


def _mm_kernel(x_ref, w_ref, *rest, nk, act, kscale, rope_quarter, w_t):
    o_ref = rest[-1]
    if nk == 1:
        if w_t:
            part = _dot_nt(x_ref[...].astype(BF16), w_ref[0].astype(BF16))
        else:
            part = _dot(x_ref[...].astype(BF16), w_ref[...].astype(BF16))
        if act == "relu2":
            part = jnp.square(jnp.maximum(part, 0.0))
        elif act == "sigmoid":
            part = _sigmoid(part)
        elif act is not None:
            part = jnp.where(pl.program_id(1) >= act[1], _sigmoid(part), part)
        if kscale is not None:
            part = part * jnp.where(pl.program_id(1) >= kscale[0], kscale[1], 1.0)
        if rope_quarter is not None:
            cos_ref, sa_ref, sb_ref = rest[:3]
            part = _rope(part, cos_ref[...], sa_ref[...], sb_ref[...], rope_quarter)
        o_ref[...] = part.astype(o_ref.dtype)
        return

    @pl.when(pl.program_id(2) == 0)
    def _():
        o_ref[...] = jnp.zeros(o_ref.shape, o_ref.dtype)

    o_ref[...] += _dot(x_ref[...].astype(BF16), w_ref[...].astype(BF16))


def _mm(x, w, l, *, col0=0, ncols=None, out_dtype=F32, act=None, tm=2048, tn=512, tk=None,
        m=None, row_map=None, kscale=None, rope=None, w_t=False):
    kdim = w.shape[2] if w_t else w.shape[1]
    ncols = (w.shape[1] if w_t else w.shape[2]) - col0 if ncols is None else ncols
    m = x.shape[0] if m is None else m
    tk = kdim if tk is None else tk
    tm, tn = min(tm, m), min(tn, ncols)
    assert m % tm == 0 and ncols % tn == 0 and kdim % tk == 0
    nk = kdim // tk
    assert nk == 1 or (act is None and out_dtype == F32 and kscale is None and rope is None and not w_t)
    row_map = (lambda i: i) if row_map is None else row_map
    args = [x, w]
    if w_t:
        assert col0 % SUBLANES == 0 and tn % SUBLANES == 0
        wspec = pl.BlockSpec((pl.Element(1), pl.Element(tn), pl.Element(tk)),
                             lambda i, j, k: (l, pl.multiple_of(col0 + j * tn, SUBLANES), 0))
    else:
        assert col0 % tn == 0
        jb = col0 // tn
        layer = l if callable(l) else (lambda i: l)
        wspec = pl.BlockSpec((None, tk, tn), lambda i, j, k: (layer(i), k, jb + j))
    in_specs = [pl.BlockSpec((tm, tk), lambda i, j, k: (row_map(i), k)), wspec]
    if kscale is not None:
        assert kscale[0] % tn == 0
        kscale = (kscale[0] // tn, kscale[1])
    if isinstance(act, tuple):
        assert act[1] % tn == 0
        act = (act[0], act[1] // tn)
    if rope is not None:
        tabs, quarter = rope
        assert tm == DEC_SEQ and tabs[0].shape == (DEC_SEQ, LANES)
        args += list(tabs)
        in_specs += [pl.BlockSpec((tm, LANES), lambda i, j, k: (0, 0))] * 3
    return pl.pallas_call(
        functools.partial(_mm_kernel, nk=nk, act=act, kscale=kscale,
                          rope_quarter=None if rope is None else rope[1], w_t=w_t),
        grid=(m // tm, ncols // tn, nk),
        in_specs=in_specs,
        out_specs=pl.BlockSpec((tm, tn), lambda i, j, k: (i, j)),
        out_shape=jax.ShapeDtypeStruct((m, ncols), out_dtype),
        compiler_params=_params("arbitrary", "arbitrary", "arbitrary"),
        name="mm",
    )(*args)


def _bm_kernel(a0_ref, a1_ref, a2_ref, w0_ref, w1_ref, w2_ref, g0_ref, g1_ref, g2_ref, o_ref):
    acc = g0_ref[...].astype(F32) * _dot(a0_ref[...], w0_ref[...].astype(BF16))
    acc = acc + g1_ref[...].astype(F32) * _dot(a1_ref[...], w1_ref[...].astype(BF16))
    acc = acc + g2_ref[...].astype(F32) * _dot(a2_ref[...], w2_ref[...].astype(BF16))
    o_ref[...] = acc.astype(BF16)


def _branch_merge(acts, ws, gates, l, *, gate_col0=0, tm=1024, tn=512):
    nj = D_MODEL // tn
    kdim = acts[0].shape[1]
    assert gate_col0 % tn == 0
    g0 = gate_col0 // tn
    aspec = pl.BlockSpec((tm, kdim), lambda i, j: (i, 0))
    wspec = pl.BlockSpec((None, kdim, tn), lambda i, j: (l, 0, j))
    gspecs = [pl.BlockSpec((tm, tn), lambda i, j, b=b: (i, g0 + b * nj + j)) for b in range(3)]
    return pl.pallas_call(
        _bm_kernel,
        grid=(N_TOK // tm, nj),
        in_specs=[aspec] * 3 + [wspec] * 3 + gspecs,
        out_specs=pl.BlockSpec((tm, tn), lambda i, j: (i, j)),
        out_shape=jax.ShapeDtypeStruct((N_TOK, D_MODEL), BF16),
        compiler_params=_params("arbitrary", "arbitrary"),
        name="branch_merge",
    )(*acts, *ws, gates, gates, gates)


LRU_CT = 512


def _lru_kernel(*refs, T, has_h0, has_state):
    xr_ref, xg_ref, cw_ref, cb_ref, wa_ref, ba_ref, wx_ref, bx_ref, lam_ref = refs[:9]
    refs = refs[9:]
    if has_h0:
        h0_ref = refs[0]
        refs = refs[1:]
    if has_state:
        y_ref, st_ref = refs[2:4]
        refs = refs[4:]
    else:
        y_ref = refs[1]
        refs = refs[2:]
    x3_ref, p_ref, a_ref, b_ref, hl_ref, pc_ref, hn_ref = refs
    nblk = xr_ref.shape[1] // LRU_BS
    S = SUBLANES
    G = T // S
    H = CONV_W - 1

    pitch = G + S
    for n in range(nblk):
        for k in range(S):
            x3_ref[n, k * pitch:k * pitch + G, :] = xr_ref[k * G:(k + 1) * G, n * LRU_BS:(n + 1) * LRU_BS]

    def gather(p, _):
        r0 = pl.multiple_of((p + H) * S, S)
        for n in range(nblk):
            p_ref.at[n][pl.ds(r0, S), :] = x3_ref.at[n][pl.ds(p, S, stride=pitch), :]
        return 0

    lax.fori_loop(0, G, gather, 0, unroll=4)
    row = lax.broadcasted_iota(jnp.int32, (S, LRU_BS), 0)
    for n in range(nblk):
        for i in range(H):
            tail = p_ref[n, (G + i) * S:(G + i + 1) * S, :]
            p_ref[n, i * S:(i + 1) * S, :] = jnp.where(row == 0, 0.0, pltpu.roll(tail, 1, 0))
            head = p_ref[n, (H + i) * S:(H + i + 1) * S, :]
            p_ref[n, (H + G + i) * S:(H + G + i + 1) * S, :] = jnp.where(row == S - 1, 0.0,
                                                                        pltpu.roll(head, S - 1, 0))

    cmats = []
    for d in range(2):
        for n in range(nblk):
            cs = slice(n * LRU_BS, (n + 1) * LRU_BS)
            xc = jnp.zeros((T, LRU_BS), F32) + cb_ref[d, :, cs]
            for j in range(CONV_W):
                off = j if d == 0 else 2 * H - j
                xc = xc + cw_ref[d, j:j + 1, cs] * p_ref[n, off * S:off * S + T, :]
            xcb = xc.astype(BF16)
            t_r = jnp.tanh(_dot(xcb, (0.5 * wa_ref[d, n]).astype(BF16)) + 0.5 * ba_ref[d, :, cs])
            t_i = jnp.tanh(_dot(xcb, (0.5 * wx_ref[d, n]).astype(BF16)) + 0.5 * bx_ref[d, :, cs])
            half_c = (-0.5 * LRU_C) * _softplus(-lam_ref[d, :, cs])
            a = jnp.exp(half_c + half_c * t_r)
            a_ref[n] = a
            b_ref[n] = (0.5 * jnp.sqrt(1.0 - a * a)) * (xc + t_i * xc)

        def body(it, carry, d=d):
            p = it if d == 0 else G - 1 - it
            r0 = pl.multiple_of(p * S, S)
            out = []
            for n in range(nblk):
                h, pc = carry[n]
                av = a_ref.at[n][pl.ds(r0, S), :]
                h = av * h + b_ref.at[n][pl.ds(r0, S), :]
                pc = av * pc
                hl_ref.at[d, n][pl.ds(r0, S), :] = h
                pc_ref.at[d, n][pl.ds(r0, S), :] = pc
                out.append((h, pc))
            return tuple(out)

        init = tuple((jnp.zeros((S, LRU_BS), F32), jnp.ones((S, LRU_BS), F32)) for _ in range(nblk))
        ends = lax.fori_loop(0, G, body, init, unroll=4)

        last_d, cmat_d = [], []
        for n in range(nblk):
            cs = slice(n * LRU_BS, (n + 1) * LRU_BS)
            h_end, pc_end = ends[n]
            c = h0_ref[d, :, cs] if has_h0 else jnp.zeros((1, LRU_BS), F32)
            rows = [None] * S
            for k in (range(S) if d == 0 else range(S - 1, -1, -1)):
                rows[k] = c
                c = h_end[k:k + 1, :] + pc_end[k:k + 1, :] * c
            last_d.append(c)
            cmat_d.append(jnp.concatenate(rows, axis=0))
        cmats.append(cmat_d)
        if has_state:
            for n in range(nblk):
                st_ref[d:d + 1, n * LRU_BS:(n + 1) * LRU_BS] = last_d[n]

    def scatter(p, _):
        r0 = pl.multiple_of(p * S, S)
        for n in range(nblk):
            h = (hl_ref.at[0, n][pl.ds(r0, S), :] + pc_ref.at[0, n][pl.ds(r0, S), :] * cmats[0][n]
                 + hl_ref.at[1, n][pl.ds(r0, S), :] + pc_ref.at[1, n][pl.ds(r0, S), :] * cmats[1][n])
            hn_ref.at[n][pl.ds(p, S, stride=pitch), :] = h
        return 0

    lax.fori_loop(0, G, scatter, 0, unroll=4)

    for n in range(nblk):
        cs = slice(n * LRU_BS, (n + 1) * LRU_BS)
        for k in range(S):
            xg = xg_ref[k * G:(k + 1) * G, cs]
            gelu = 0.5 * xg * (1.0 + jnp.tanh(GELU_TANH_SCALE * (xg + GELU_TANH_CUBIC * (xg * xg * xg))))
            y_ref[k * G:(k + 1) * G, cs] = (hn_ref[n, k * pitch:k * pitch + G, :] * gelu).astype(BF16)


def _lru(proj_a, lw, l, *, T, nb, blk0, prev, h0=None, st_prev=None):
    ct = LRU_CT
    ncb = D_RNN // ct
    bpc = ct // LRU_BS
    in_specs = [
        pl.BlockSpec((T, ct), lambda b, c: (blk0 + b, c)),
        pl.BlockSpec((T, ct), lambda b, c: (blk0 + b, ncb + c)),
        pl.BlockSpec((None, 2, CONV_W, ct), lambda b, c: (l, 0, 0, c)),
        pl.BlockSpec((None, 2, 1, ct), lambda b, c: (l, 0, 0, c)),
        pl.BlockSpec((None, 2, bpc, LRU_BS, LRU_BS), lambda b, c: (l, 0, c, 0, 0)),
        pl.BlockSpec((None, 2, 1, ct), lambda b, c: (l, 0, 0, c)),
        pl.BlockSpec((None, 2, bpc, LRU_BS, LRU_BS), lambda b, c: (l, 0, c, 0, 0)),
        pl.BlockSpec((None, 2, 1, ct), lambda b, c: (l, 0, 0, c)),
        pl.BlockSpec((None, 2, 1, ct), lambda b, c: (l, 0, 0, c)),
    ]
    args = [proj_a, proj_a, lw["conv_w"], lw["conv_b"], lw["wa"], lw["ba"], lw["wx"], lw["bx"], lw["lam"]]
    if h0 is not None:
        in_specs.append(pl.BlockSpec((None, None, 2, 1, ct), lambda b, c: (b, l, 0, 0, c)))
        args.append(h0)
    has_state = st_prev is not None
    aliases = {len(args): 0}
    in_specs.append(pl.BlockSpec(memory_space=pl.ANY))
    args.append(prev)
    out_specs = [pl.BlockSpec((T, ct), lambda b, c: (blk0 + b, c))]
    out_shape = [jax.ShapeDtypeStruct(prev.shape, prev.dtype)]
    if has_state:
        aliases[len(args)] = 1
        in_specs.append(pl.BlockSpec(memory_space=pl.ANY))
        args.append(st_prev)
        out_specs.append(pl.BlockSpec((None, None, 2, ct), lambda b, c: (b, l, 0, c)))
        out_shape.append(jax.ShapeDtypeStruct(st_prev.shape, st_prev.dtype))
    return pl.pallas_call(
        functools.partial(_lru_kernel, T=T, has_h0=h0 is not None, has_state=has_state),
        grid=(nb, ncb),
        in_specs=in_specs,
        out_specs=out_specs,
        out_shape=out_shape,
        input_output_aliases=aliases,
        scratch_shapes=[pltpu.VMEM((bpc, T + SUBLANES * SUBLANES, LRU_BS), F32),
                        pltpu.VMEM((bpc, T + 2 * (CONV_W - 1) * SUBLANES, LRU_BS), F32),
                        pltpu.VMEM((bpc, T, LRU_BS), F32), pltpu.VMEM((bpc, T, LRU_BS), F32),
                        pltpu.VMEM((2, bpc, T, LRU_BS), F32), pltpu.VMEM((2, bpc, T, LRU_BS), F32),
                        pltpu.VMEM((bpc, T + SUBLANES * SUBLANES, LRU_BS), F32)],
        compiler_params=_params("arbitrary", "arbitrary"),
        name="lru",
    )(*args)


def _mla_prep_kernel(p_ref, gq_ref, gkv_ref, cos_ref, sa_ref, sb_ref, *rest):
    cq_ref, ckvb_ref, kr_ref, ckv_out_ref, kr_out_ref = rest[-5:]
    i = pl.program_id(0)
    cq_ref[...] = _rms(p_ref[:, 0:MLA_Q_RANK], gq_ref[...]).astype(BF16)
    ckv = _rms(p_ref[:, MLA_Q_RANK:MLA_Q_RANK + MLA_KV_RANK], gkv_ref[...])
    ckvb_ref[...] = ckv.astype(BF16)
    k0 = MLA_Q_RANK + MLA_KV_RANK
    kr = p_ref[:, k0:k0 + LANES]

    @pl.when(i < N_CTX // ROW_TILE)
    def _():
        kr_ref[...] = kr
        ckv_out_ref[...] = ckv
        kr_out_ref[...] = kr[:, 0:MLA_ROPE]

    @pl.when(i >= N_CTX // ROW_TILE)
    def _():
        kr_ref[...] = _rope(kr, cos_ref[...], sa_ref[...], sb_ref[...], MLA_ROPE // 4)


def _lat_tile(i):
    return (jnp.maximum(i - N_CTX // ROW_TILE, 0)) % (DEC_SEQ // ROW_TILE)


def _mla_prep(proj_a, gq, gkv, l, tab, caches):
    tr = ROW_TILE
    assert tr == SEQ
    last_ctx = BATCH - 1
    tspec = pl.BlockSpec((tr, LANES), lambda i: (_lat_tile(i), 0))
    args = [proj_a, gq, gkv, *tab]
    in_specs = [pl.BlockSpec((tr, MLA_SLAB), lambda i: (i, 2 * D_RNN // MLA_SLAB)),
                pl.BlockSpec((None, 1, MLA_Q_RANK), lambda i: (l, 0, 0)),
                pl.BlockSpec((None, 1, MLA_KV_RANK), lambda i: (l, 0, 0)),
                tspec, tspec, tspec]
    aliases = {len(args): 3, len(args) + 1: 4}
    args += list(caches)
    in_specs += [pl.BlockSpec(memory_space=pl.ANY)] * 2
    return pl.pallas_call(
        _mla_prep_kernel,
        grid=(N_TOK // tr,),
        in_specs=in_specs,
        out_specs=[pl.BlockSpec((tr, MLA_Q_RANK), lambda i: (i, 0)),
                   pl.BlockSpec((tr, MLA_KV_RANK), lambda i: (i, 0)),
                   pl.BlockSpec((tr, LANES), lambda i: (i, 0)),
                   pl.BlockSpec((None, None, SEQ, MLA_KV_RANK), lambda i: (jnp.minimum(i, last_ctx), l, 0, 0)),
                   pl.BlockSpec((None, None, SEQ, MLA_ROPE), lambda i: (jnp.minimum(i, last_ctx), l, 0, 0))],
        out_shape=[jax.ShapeDtypeStruct((N_TOK, MLA_Q_RANK), BF16),
                   jax.ShapeDtypeStruct((N_TOK, MLA_KV_RANK), BF16),
                   jax.ShapeDtypeStruct((N_TOK, LANES), F32),
                   jax.ShapeDtypeStruct((BATCH, DEPTH, SEQ, MLA_KV_RANK), F32),
                   jax.ShapeDtypeStruct((BATCH, DEPTH, SEQ, MLA_ROPE), F32)],
        input_output_aliases=aliases,
        compiler_params=_params("arbitrary"),
        name="mla_prep",
    )(*args)


def _attn_kernel(*refs, lat):
    if lat:
        (q_ref, kvn_ref, krn_ref, kvc_ref, krc_ref, cos_ref, sa_ref, sb_ref, _,
         o_ref, kcat_ref, vcat_ref) = refs
    else:
        q_ref, kvn_ref, krn_ref, _, o_ref, kcat_ref = refs
    scale = (MLA_NOPE + MLA_ROPE) ** -0.5
    s_new = kvn_ref.shape[0]
    off = PAST_LEN if lat else 0

    def build():
        lane = lax.broadcasted_iota(jnp.int32, (s_new, LANES), 1)
        kr_new = jnp.where(lane < MLA_ROPE, krn_ref[...], 0.0).astype(BF16)
        for h in range(MLA_HEADS):
            c0 = h * MLA_QH
            kcat_ref[h, off:off + s_new, 0:MLA_NOPE] = kvn_ref[:, c0:c0 + MLA_NOPE]
            kcat_ref[h, off:off + s_new, MLA_NOPE:MLA_QH] = kr_new
            if lat:
                kcat_ref[h, 0:off, 0:MLA_NOPE] = kvc_ref[:, c0:c0 + MLA_NOPE]
                kcat_ref[h, 0:off, MLA_NOPE:MLA_NOPE + MLA_ROPE] = krc_ref[...].astype(BF16)
                kcat_ref[h, 0:off, MLA_NOPE + MLA_ROPE:MLA_QH] = jnp.zeros((off, MLA_QH - MLA_NOPE - MLA_ROPE), BF16)
                vcat_ref[0:off, h * MLA_V:(h + 1) * MLA_V] = kvc_ref[:, c0 + MLA_NOPE:c0 + MLA_QH]
                vcat_ref[off:off + s_new, h * MLA_V:(h + 1) * MLA_V] = kvn_ref[:, c0 + MLA_NOPE:c0 + MLA_QH]

    if lat:
        pl.when(pl.program_id(1) == 0)(build)
    else:
        build()

    for h in range(MLA_HEADS):
        c0 = h * MLA_QH
        qn = q_ref[:, c0:c0 + MLA_NOPE].astype(F32)
        qr = q_ref[:, c0 + MLA_NOPE:c0 + MLA_QH].astype(F32)
        if lat:
            qr = _rope(qr, cos_ref[...], sa_ref[...], sb_ref[...], MLA_ROPE // 4)
        qh = (jnp.concatenate([qn, qr], axis=1) * scale).astype(BF16)
        s = _dot_nt(qh, kcat_ref[h])
        p = jnp.exp(s - jnp.max(s, axis=-1, keepdims=True))
        den = jnp.sum(p, axis=-1, keepdims=True)
        v = vcat_ref[:, h * MLA_V:(h + 1) * MLA_V] if lat else kvn_ref[:, c0 + MLA_NOPE:c0 + MLA_QH]
        o_ref[:, h * MLA_V:(h + 1) * MLA_V] = (_dot(p.astype(BF16), v) / den).astype(BF16)


def _attn_ctx(q, kv, kr, prev):
    t = SEQ
    return pl.pallas_call(
        functools.partial(_attn_kernel, lat=False),
        grid=(BATCH,),
        in_specs=[pl.BlockSpec((t, MLA_HEADS * MLA_QH), lambda b: (b, 0)),
                  pl.BlockSpec((t, MLA_HEADS * MLA_QH), lambda b: (b, 0)),
                  pl.BlockSpec((t, LANES), lambda b: (b, 0)),
                  pl.BlockSpec(memory_space=pl.ANY)],
        out_specs=pl.BlockSpec((t, MLA_HEADS * MLA_V), lambda b: (b, 0)),
        out_shape=jax.ShapeDtypeStruct(prev.shape, prev.dtype),
        input_output_aliases={3: 0},
        scratch_shapes=[pltpu.VMEM((MLA_HEADS, t, MLA_QH), BF16)],
        compiler_params=_params("arbitrary"),
        name="attn_ctx",
    )(q, kv, kr, prev)


def _attn_lat(q, kv, kr, kv_cache, cache_krope, l, tab, prev):
    tq = ROW_TILE
    nq = DEC_SEQ // tq
    blk0 = N_CTX // DEC_SEQ
    qblk0 = N_CTX // tq
    tspec = pl.BlockSpec((tq, LANES), lambda b, i: (i, 0))
    return pl.pallas_call(
        functools.partial(_attn_kernel, lat=True),
        grid=(DEC_BATCH, nq),
        in_specs=[pl.BlockSpec((tq, MLA_HEADS * MLA_QH), lambda b, i: (qblk0 + b * nq + i, 0)),
                  pl.BlockSpec((DEC_SEQ, MLA_HEADS * MLA_QH), lambda b, i: (blk0 + b, 0)),
                  pl.BlockSpec((DEC_SEQ, LANES), lambda b, i: (blk0 + b, 0)),
                  pl.BlockSpec((PAST_LEN, MLA_HEADS * MLA_QH), lambda b, i: (b * DEPTH + l, 0)),
                  pl.BlockSpec((None, None, PAST_LEN, MLA_ROPE), lambda b, i: (b, l, 0, 0)),
                  tspec, tspec, tspec, pl.BlockSpec(memory_space=pl.ANY)],
        out_specs=pl.BlockSpec((tq, MLA_HEADS * MLA_V), lambda b, i: (qblk0 + b * nq + i, 0)),
        out_shape=jax.ShapeDtypeStruct((N_TOK, MLA_HEADS * MLA_V), BF16),
        input_output_aliases={8: 0},
        scratch_shapes=[pltpu.VMEM((MLA_HEADS, PAST_LEN + DEC_SEQ, MLA_QH), BF16),
                        pltpu.VMEM((PAST_LEN + DEC_SEQ, MLA_HEADS * MLA_V), BF16)],
        compiler_params=_params("arbitrary", "arbitrary"),
        name="attn_lat",
    )(q, kv, kr, kv_cache, cache_krope, *tab, prev)


def _log_sigmoid(x):
    return -_softplus(-x)


def _ret_kernel(*refs, lat, T):
    if lat:
        q_ref, k_ref, v_ref, rg_ref, dec_ref, s0_ref, _, o_ref, decay_ref = refs
    else:
        q_ref, k_ref, v_ref, rg_ref, dec_ref = refs[:5]
        o_ref, st_ref, decay_ref = refs[-3:]
    tq = q_ref.shape[0]
    t0 = pl.program_id(0) * tq if lat else 0
    batch_id = pl.program_id(1) if lat else pl.program_id(0)
    lg = _log_sigmoid(dec_ref[...])
    tcol = (t0 + lax.broadcasted_iota(jnp.int32, (tq, 1), 0)).astype(F32)

    @pl.when(batch_id == 0)
    def _():
        rows = (t0 + lax.broadcasted_iota(jnp.int32, (tq, T), 0)).astype(F32)
        cols = lax.broadcasted_iota(jnp.int32, (tq, T), 1).astype(F32)
        diff = rows - cols
        fwd = diff >= 0.0
        dist = jnp.abs(diff)
        for h in range(RET_HEADS):
            one_sided = jnp.exp(jnp.where(fwd, lg[0:1, h:h + 1], lg[1:2, h:h + 1]) * dist)
            decay_ref[h] = jnp.where(dist == 0.0, 2.0, one_sided)

    for h in range(RET_HEADS):
        cs = slice(h * RET_DK, (h + 1) * RET_DK)
        lgf = lg[0:1, h:h + 1]
        lgb = lg[1:2, h:h + 1]
        q = q_ref[:, cs]
        k = k_ref[:, cs]
        v = v_ref[:, cs]
        sc = _dot_nt(q, k) * decay_ref[h]
        o = _dot(sc.astype(BF16), v)
        if lat:
            qf = q.astype(F32)
            o = o + _dot((qf * jnp.exp(lgf * (tcol + 1.0))).astype(BF16), s0_ref[0, h].astype(BF16))
            o = o + _dot((qf * jnp.exp(lgb * (T - tcol))).astype(BF16), s0_ref[1, h].astype(BF16))
        else:
            vf = v.astype(F32)
            st_ref[0, h] = _dot_tn(k, (vf * jnp.exp(lgf * (T - 1.0 - tcol))).astype(BF16))
            st_ref[1, h] = _dot_tn(k, (vf * jnp.exp(lgb * tcol)).astype(BF16))
        mu = jnp.mean(o, axis=-1, keepdims=True)
        oc = o - mu
        var = jnp.mean(oc * oc, axis=-1, keepdims=True)
        rg = rg_ref[:, cs].astype(F32)
        o_ref[:, cs] = (oc * lax.rsqrt(var + EPS) * (rg * _sigmoid(rg))).astype(BF16)


def _ret_ctx(qk, vg, decay, l, prev, st_prev):
    t = SEQ
    w = RET_HEADS * RET_DK
    return pl.pallas_call(
        functools.partial(_ret_kernel, lat=False, T=t),
        grid=(BATCH,),
        in_specs=[pl.BlockSpec((t, w), lambda b: (b, 0)),
                  pl.BlockSpec((t, w), lambda b: (b, 1)),
                  pl.BlockSpec((t, w), lambda b: (b, 0)),
                  pl.BlockSpec((t, w), lambda b: (b, 1)),
                  pl.BlockSpec((None, 2, RET_HEADS), lambda b: (l, 0, 0)),
                  pl.BlockSpec(memory_space=pl.ANY), pl.BlockSpec(memory_space=pl.ANY)],
        out_specs=[pl.BlockSpec((t, w), lambda b: (b, 0)),
                   pl.BlockSpec((None, None, 2, RET_HEADS, RET_DK, RET_DV), lambda b: (b, l, 0, 0, 0, 0))],
        out_shape=[jax.ShapeDtypeStruct(prev.shape, prev.dtype),
                   jax.ShapeDtypeStruct(st_prev.shape, st_prev.dtype)],
        input_output_aliases={5: 0, 6: 1},
        scratch_shapes=[pltpu.VMEM((RET_HEADS, t, t), F32)],
        compiler_params=_params("arbitrary"),
        name="ret_ctx",
    )(qk, qk, vg, vg, decay, prev, st_prev)


def _ret_lat(qk, vg, decay, state_ret, l, prev):
    tq = ROW_TILE
    t = DEC_SEQ
    nq = t // tq
    w = RET_HEADS * RET_DK
    blk0 = N_CTX // t
    qblk0 = N_CTX // tq
    return pl.pallas_call(
        functools.partial(_ret_kernel, lat=True, T=t),
        grid=(nq, DEC_BATCH),
        in_specs=[pl.BlockSpec((tq, w), lambda i, b: (b * nq + i, 0)),
                  pl.BlockSpec((t, w), lambda i, b: (b, 1)),
                  pl.BlockSpec((t, w), lambda i, b: (blk0 + b, 0)),
                  pl.BlockSpec((tq, w), lambda i, b: (qblk0 + b * nq + i, 1)),
                  pl.BlockSpec((None, 2, RET_HEADS), lambda i, b: (l, 0, 0)),
                  pl.BlockSpec((None, None, 2, RET_HEADS, RET_DK, RET_DV), lambda i, b: (b, l, 0, 0, 0, 0)),
                  pl.BlockSpec(memory_space=pl.ANY)],
        out_specs=pl.BlockSpec((tq, w), lambda i, b: (qblk0 + b * nq + i, 0)),
        out_shape=jax.ShapeDtypeStruct((N_TOK, w), BF16),
        scratch_shapes=[pltpu.VMEM((RET_HEADS, tq, t), F32)],
        input_output_aliases={6: 0},
        compiler_params=_params("arbitrary", "arbitrary"),
        name="ret_lat",
    )(qk, qk, vg, vg, decay, state_ret, prev)


def _rope_tables(dim):
    half = dim // 2
    quarter = half // 2
    t = jnp.arange(DEC_SEQ)
    row = (t // GRID_W).astype(F32)
    col = (t % GRID_W).astype(F32)
    inv = ROPE_BASE ** (-jnp.arange(0, half, 2, dtype=F32) / half)
    zeros = jnp.zeros((DEC_SEQ, quarter), F32)
    cos_parts, sa_parts, sb_parts = [], [], []
    for pos in (row, col):
        ang = pos[:, None] * inv[None, :]
        c, s = jnp.cos(ang), jnp.sin(ang)
        cos_parts += [c, c]
        sa_parts += [zeros, s]
        sb_parts += [-s, zeros]
    padw = LANES - dim

    def build(parts):
        return jnp.concatenate(parts + ([jnp.zeros((DEC_SEQ, padw), F32)] if padw else []), axis=-1)

    return build(cos_parts), build(sa_parts), build(sb_parts)


def kernel(x_prompt, x_sample, c, cache_mla_ckv, cache_mla_krope, state_lru, state_ret, c_ctx, w_mod, b_mod,
           g_norm, w_in, lru_conv_w, lru_conv_b, lru_wa, lru_ba, lru_wx, lru_bx, lru_lam, mla_gq, mla_gkv,
           mla_wuq, mla_wukv, ret_decay, w_br_lru, w_br_mla, w_br_ret, w_out, w_ff1, w_ff2):
    x_ctx = x_prompt.reshape(N_CTX, D_MODEL)
    x_lat = x_sample.reshape(N_LAT, D_MODEL)
    cond8 = jnp.concatenate([c_ctx[None, :], c, jnp.zeros((8 - 1 - DEC_BATCH, D_MODEL), F32)], axis=0)
    mod = _ada(cond8, w_mod, b_mod).reshape(DEPTH * 8, 1, 6 * D_MODEL)
    g_all = g_norm.reshape(DEPTH * 4, 1, D_MODEL)

    w_in_t = jnp.swapaxes(w_in, 1, 2)
    wuq = jnp.pad(mla_wuq.reshape(DEPTH, MLA_Q_RANK, MLA_HEADS, MLA_NOPE + MLA_ROPE),
                  ((0, 0), (0, 0), (0, 0), (0, MLA_QH - MLA_NOPE - MLA_ROPE)))
    wuq = wuq.reshape(DEPTH, MLA_Q_RANK, MLA_HEADS * MLA_QH)
    lw = {"conv_w": lru_conv_w, "conv_b": lru_conv_b.reshape(DEPTH, 2, 1, D_RNN), "wa": lru_wa,
          "ba": lru_ba.reshape(DEPTH, 2, 1, D_RNN), "wx": lru_wx, "bx": lru_bx.reshape(DEPTH, 2, 1, D_RNN),
          "lam": lru_lam.reshape(DEPTH, 2, 1, D_RNN)}
    gq = mla_gq.reshape(DEPTH, 1, MLA_Q_RANK)
    gkv = mla_gkv.reshape(DEPTH, 1, MLA_KV_RANK)
    h0_lat = state_lru.reshape(DEC_BATCH, DEPTH, 2, 1, D_RNN)
    cache_ckv_rows = cache_mla_ckv.reshape(DEC_BATCH * DEPTH * PAST_LEN, MLA_KV_RANK)
    tab_mla = _rope_tables(MLA_ROPE)
    tab_ret = _rope_tables(RET_DK)
    wq = RET_HEADS * RET_DK

    caches = (jnp.zeros((BATCH, DEPTH, SEQ, MLA_KV_RANK), F32), jnp.zeros((BATCH, DEPTH, SEQ, MLA_ROPE), F32))
    st_lru = jnp.zeros((BATCH, DEPTH, 2, D_RNN), F32)
    st_ret = jnp.zeros((BATCH, DEPTH, 2, RET_HEADS, RET_DK, RET_DV), F32)
    y_lru = jnp.zeros((N_TOK, D_RNN), BF16)
    y_mla = jnp.zeros((N_TOK, MLA_HEADS * MLA_V), BF16)
    y_ret = jnp.zeros((N_TOK, RET_HEADS * RET_DV), BF16)

    kv_cache = _mm(cache_ckv_rows, mla_wukv, lambda i: i % DEPTH, out_dtype=BF16, tm=PAST_LEN)

    h, = _resnorm(x_ctx, x_lat, mod, g_all, nxt=(0, 0, 0))
    for l in range(DEPTH):
        proj_a = _mm(h, w_in_t, l, col0=0, ncols=2 * D_RNN + MLA_SLAB, w_t=True)
        kscale = (wq, RET_DK ** -0.5)
        qk_ctx = _mm(h, w_in_t, l, col0=COL_TAIL, ncols=2 * wq, out_dtype=BF16, m=N_CTX, kscale=kscale,
                     w_t=True)
        qk_lat = _mm(h, w_in_t, l, col0=COL_TAIL, ncols=2 * wq, out_dtype=BF16, m=N_LAT, tm=DEC_SEQ,
                     row_map=lambda i: N_CTX // DEC_SEQ + i, kscale=kscale,
                     rope=(tab_ret, RET_DK // 4), w_t=True)
        vg = _mm(h, w_in_t, l, col0=COL_TAIL + 2 * wq, ncols=2 * wq + 3 * D_MODEL, out_dtype=BF16,
                 act=("sigmoid_from", 2 * wq), w_t=True)

        y_lru, st_lru = _lru(proj_a, lw, l, T=SEQ, nb=BATCH, blk0=0, prev=y_lru, st_prev=st_lru)
        y_lru, = _lru(proj_a, lw, l, T=DEC_SEQ, nb=DEC_BATCH, blk0=N_CTX // DEC_SEQ, h0=h0_lat, prev=y_lru)

        cqn, ckvb, kr, *caches = _mla_prep(proj_a, gq, gkv, l, tab_mla, caches)
        q = _mm(cqn, wuq, l, out_dtype=BF16)
        kv = _mm(ckvb, mla_wukv, l, out_dtype=BF16)
        y_mla = _attn_ctx(q, kv, kr, y_mla)
        y_mla = _attn_lat(q, kv, kr, kv_cache, cache_mla_krope, l, tab_mla, y_mla)

        y_ret, st_ret = _ret_ctx(qk_ctx, vg, ret_decay, l, y_ret, st_ret)
        y_ret = _ret_lat(qk_lat, vg, ret_decay, state_ret, l, y_ret)

        merged = _branch_merge((y_lru, y_mla, y_ret), (w_br_lru, w_br_mla, w_br_ret), vg, l, gate_col0=2 * wq)
        u = _mm(merged, w_out, l)
        x_ctx, x_lat, h2 = _resnorm(x_ctx, x_lat, mod, g_all, res=(u, l * 4 + 1, l, 2), nxt=(l * 4 + 2, l, 3))
        ff = _mm(h2, w_ff1, l, out_dtype=BF16, act="relu2")
        y = _mm(ff, w_ff2, l, tn=1024, tk=1024)
        if l + 1 < DEPTH:
            x_ctx, x_lat, h = _resnorm(x_ctx, x_lat, mod, g_all, res=(y, l * 4 + 3, l, 5),
                                       nxt=((l + 1) * 4, l + 1, 0))
        else:
            x_ctx, x_lat = _resnorm(x_ctx, x_lat, mod, g_all, res=(y, l * 4 + 3, l, 5))

    new_ckv, new_krope = caches
    return (x_ctx.reshape(BATCH, SEQ, D_MODEL), x_lat.reshape(DEC_BATCH, DEC_SEQ, D_MODEL),
            new_ckv, new_krope, st_lru, st_ret)
```

```python
import functools

import jax
import jax.numpy as jnp
from jax import lax
from jax.experimental import pallas as pl
from jax.experimental.pallas import tpu as pltpu

F32 = jnp.float32
BF16 = jnp.bfloat16

D_MODEL = 2048
BATCH = 16
SEQ = 256
DEPTH = 4
DEC_BATCH = 4
DEC_SEQ = 1024
PAST_LEN = 256
GRID_W = 64
EPS = 1e-6
ROPE_BASE = 10000.0
D_RNN = D_MODEL // 2
LRU_BLOCKS = 8
LRU_BS = D_RNN // LRU_BLOCKS
CONV_W = 4
LRU_C = 8.0
MLA_HEADS = 8
MLA_NOPE = 128
MLA_ROPE = 64
MLA_V = 128
MLA_Q_RANK = D_MODEL // 4
MLA_KV_RANK = D_MODEL // 8
RET_HEADS = 8
RET_DK = 128
RET_DV = 128
D_FF = 4 * D_MODEL

N_CTX = BATCH * SEQ
N_LAT = DEC_BATCH * DEC_SEQ
N_TOK = N_CTX + N_LAT
COL_TAIL = 2 * D_RNN + MLA_Q_RANK + MLA_KV_RANK + MLA_ROPE
MLA_QH = 256
MLA_SLAB = 1024
GELU_TANH_SCALE = 0.7978845608028654
GELU_TANH_CUBIC = 0.044715

LANES = 128
SUBLANES = 8
VMEM_LIMIT = 56 * 1024 * 1024
ROW_TILE = 256
RES_TILE = 512
LAT_Q_TILE = 512


def _params(*sem):
    return pltpu.CompilerParams(dimension_semantics=sem, vmem_limit_bytes=VMEM_LIMIT)


def _sigmoid(x):
    return 0.5 * (1.0 + jnp.tanh(0.5 * x))


def _softplus(x):
    return jnp.maximum(x, 0.0) + jnp.log(1.0 + jnp.exp(-jnp.abs(x)))


def _rms(x, g):
    return x * lax.rsqrt(jnp.mean(x * x, axis=-1, keepdims=True) + EPS) * g


def _dot(a, b):
    return jnp.dot(a, b, preferred_element_type=F32)


def _dot_nt(a, b):
    return lax.dot_general(a, b, (((1,), (1,)), ((), ())), preferred_element_type=F32)


def _dot_tn(a, b):
    return lax.dot_general(a, b, (((0,), (0,)), ((), ())), preferred_element_type=F32)


def _rope(x, cos, sa, sb, quarter):
    w = x.shape[-1]
    reps = w // cos.shape[-1]
    if reps > 1:
        cos, sa, sb = (jnp.tile(t, (1, reps)) for t in (cos, sa, sb))
    return x * cos + pltpu.roll(x, quarter, 1) * sa + pltpu.roll(x, w - quarter, 1) * sb


def _mod_row(row0):
    return jnp.where(row0 < N_CTX, 0, 1 + (row0 - N_CTX) // DEC_SEQ)


def _ada_kernel(c_ref, w_ref, b_ref, o_ref):
    c = c_ref[...]
    s = c * _sigmoid(c)
    o_ref[...] = _dot(s.astype(BF16), w_ref[...].astype(BF16)) + b_ref[...]


def _ada(cond8, w_mod, b_mod):
    tn = 1024
    n = w_mod.shape[-1]
    return pl.pallas_call(
        _ada_kernel,
        grid=(DEPTH, n // tn),
        in_specs=[pl.BlockSpec((8, D_MODEL), lambda l, j: (0, 0)),
                  pl.BlockSpec((None, D_MODEL, tn), lambda l, j: (l, 0, j)),
                  pl.BlockSpec((None, 1, tn), lambda l, j: (l, 0, j))],
        out_specs=pl.BlockSpec((None, 8, tn), lambda l, j: (l, 0, j)),
        out_shape=jax.ShapeDtypeStruct((DEPTH, 8, n), F32),
        compiler_params=_params("arbitrary", "arbitrary"),
        name="ada",
    )(cond8, w_mod, b_mod.reshape(DEPTH, 1, n))


def _resnorm_kernel(*refs, has_res, has_next):
    refs = list(refs)
    xc_ref, xl_ref = refs[:2]
    refs = refs[2:]
    is_ctx = pl.program_id(0) < N_CTX // RES_TILE
    x = jnp.where(is_ctx, xc_ref[...], xl_ref[...])
    if has_res:
        u_ref, gpost_ref, gate_ref = refs[:3]
        refs = refs[3:]
        x = x + gate_ref[...] * _rms(u_ref[...], gpost_ref[...])
    if has_next:
        gpre_ref, shift_ref, scale_ref = refs[:3]
        refs = refs[3:]
    if has_res:
        xco_ref, xlo_ref = refs[:2]
        refs = refs[2:]

        @pl.when(is_ctx)
        def _():
            xco_ref[...] = x

        @pl.when(jnp.logical_not(is_ctx))
        def _():
            xlo_ref[...] = x
    if has_next:
        ho_ref = refs.pop(0)
        h = _rms(x, gpre_ref[...]) * (1.0 + scale_ref[...]) + shift_ref[...]
        ho_ref[...] = h.astype(BF16)


def _resnorm(x_ctx, x_lat, mod, g_norm, *, res=None, nxt=None):
    tr = RES_TILE
    nctx = N_CTX // tr
    row = pl.BlockSpec((tr, D_MODEL), lambda i: (i, 0))
    crow = pl.BlockSpec((tr, D_MODEL), lambda i: (jnp.minimum(i, nctx - 1), 0))
    lrow = pl.BlockSpec((tr, D_MODEL), lambda i: (jnp.maximum(i - nctx, 0), 0))

    def gspec(k):
        return pl.BlockSpec((None, 1, D_MODEL), lambda i: (k, 0, 0))

    def mspec(l, chunk):
        return pl.BlockSpec((None, 1, D_MODEL), lambda i: (l * 8 + _mod_row(i * tr), 0, chunk))

    args, specs, outs, ospecs = [x_ctx, x_lat], [crow, lrow], [], []
    if res is not None:
        u, gk, l, chunk = res
        args += [u, g_norm, mod]
        specs += [row, gspec(gk), mspec(l, chunk)]
        outs += [jax.ShapeDtypeStruct((N_CTX, D_MODEL), F32), jax.ShapeDtypeStruct((N_LAT, D_MODEL), F32)]
        ospecs += [crow, lrow]
    if nxt is not None:
        gk, l, chunk = nxt
        args += [g_norm, mod, mod]
        specs += [gspec(gk), mspec(l, chunk), mspec(l, chunk + 1)]
        outs.append(jax.ShapeDtypeStruct((N_TOK, D_MODEL), BF16))
        ospecs.append(row)
    return pl.pallas_call(
        functools.partial(_resnorm_kernel, has_res=res is not None, has_next=nxt is not None),
        grid=(N_TOK // tr,),
        in_specs=specs, out_specs=ospecs, out_shape=outs,
        compiler_params=_params("arbitrary"),
        name="resnorm",
    )(*args)


def _mm_kernel(x_ref, w_ref, *rest, nk, act, kscale, rope_quarter, w_t):
    o_ref = rest[-1]
    if nk == 1:
        if w_t:
            part = _dot_nt(x_ref[...].astype(BF16), w_ref[0].astype(BF16))
        else:
            part = _dot(x_ref[...].astype(BF16), w_ref[...].astype(BF16))
        if act == "relu2":
            part = jnp.square(jnp.maximum(part, 0.0))
        elif act == "sigmoid":
            part = _sigmoid(part)
        elif act is not None:
            part = jnp.where(pl.program_id(1) >= act[1], _sigmoid(part), part)
        if kscale is not None:
            part = part * jnp.where(pl.program_id(1) >= kscale[0], kscale[1], 1.0)
        if rope_quarter is not None:
            cos_ref, sa_ref, sb_ref = rest[:3]
            part = _rope(part, cos_ref[...], sa_ref[...], sb_ref[...], rope_quarter)
        o_ref[...] = part.astype(o_ref.dtype)
        return

    @pl.when(pl.program_id(2) == 0)
    def _():
        o_ref[...] = jnp.zeros(o_ref.shape, o_ref.dtype)

    o_ref[...] += _dot(x_ref[...].astype(BF16), w_ref[...].astype(BF16))


def _mm(x, w, l, *, col0=0, ncols=None, out_dtype=F32, act=None, tm=2048, tn=512, tk=None,
        m=None, row_map=None, kscale=None, rope=None, w_t=False):
    kdim = w.shape[2] if w_t else w.shape[1]
    ncols = (w.shape[1] if w_t else w.shape[2]) - col0 if ncols is None else ncols
    m = x.shape[0] if m is None else m
    tk = kdim if tk is None else tk
    tm, tn = min(tm, m), min(tn, ncols)
    assert m % tm == 0 and ncols % tn == 0 and kdim % tk == 0
    nk = kdim // tk
    assert nk == 1 or (act is None and out_dtype == F32 and kscale is None and rope is None and not w_t)
    row_map = (lambda i: i) if row_map is None else row_map
    args = [x, w]
    if w_t:
        assert col0 % SUBLANES == 0 and tn % SUBLANES == 0
        wspec = pl.BlockSpec((pl.Element(1), pl.Element(tn), pl.Element(tk)),
                             lambda i, j, k: (l, pl.multiple_of(col0 + j * tn, SUBLANES), 0))
    else:
        assert col0 % tn == 0
        jb = col0 // tn
        layer = l if callable(l) else (lambda i: l)
        wspec = pl.BlockSpec((None, tk, tn), lambda i, j, k: (layer(i), k, jb + j))
    in_specs = [pl.BlockSpec((tm, tk), lambda i, j, k: (row_map(i), k)), wspec]
    if kscale is not None:
        assert kscale[0] % tn == 0
        kscale = (kscale[0] // tn, kscale[1])
    if isinstance(act, tuple):
        assert act[1] % tn == 0
        act = (act[0], act[1] // tn)
    if rope is not None:
        tabs, quarter = rope
        assert tm == DEC_SEQ and tabs[0].shape == (DEC_SEQ, LANES)
        args += list(tabs)
        in_specs += [pl.BlockSpec((tm, LANES), lambda i, j, k: (0, 0))] * 3
    return pl.pallas_call(
        functools.partial(_mm_kernel, nk=nk, act=act, kscale=kscale,
                          rope_quarter=None if rope is None else rope[1], w_t=w_t),
        grid=(m // tm, ncols // tn, nk),
        in_specs=in_specs,
        out_specs=pl.BlockSpec((tm, tn), lambda i, j, k: (i, j)),
        out_shape=jax.ShapeDtypeStruct((m, ncols), out_dtype),
        compiler_params=_params("arbitrary", "arbitrary", "arbitrary"),
        name="mm",
    )(*args)


def _bm_kernel(a0_ref, a1_ref, a2_ref, w0_ref, w1_ref, w2_ref, g0_ref, g1_ref, g2_ref, o_ref):
    acc = g0_ref[...].astype(F32) * _dot(a0_ref[...], w0_ref[...].astype(BF16))
    acc = acc + g1_ref[...].astype(F32) * _dot(a1_ref[...], w1_ref[...].astype(BF16))
    acc = acc + g2_ref[...].astype(F32) * _dot(a2_ref[...], w2_ref[...].astype(BF16))
    o_ref[...] = acc.astype(BF16)


def _branch_merge(acts, ws, gates, l, *, gate_col0=0, tm=1024, tn=512):
    nj = D_MODEL // tn
    kdim = acts[0].shape[1]
    assert gate_col0 % tn == 0
    g0 = gate_col0 // tn
    aspec = pl.BlockSpec((tm, kdim), lambda i, j: (i, 0))
    wspec = pl.BlockSpec((None, kdim, tn), lambda i, j: (l, 0, j))
    gspecs = [pl.BlockSpec((tm, tn), lambda i, j, b=b: (i, g0 + b * nj + j)) for b in range(3)]
    return pl.pallas_call(
        _bm_kernel,
        grid=(N_TOK // tm, nj),
        in_specs=[aspec] * 3 + [wspec] * 3 + gspecs,
        out_specs=pl.BlockSpec((tm, tn), lambda i, j: (i, j)),
        out_shape=jax.ShapeDtypeStruct((N_TOK, D_MODEL), BF16),
        compiler_params=_params("arbitrary", "arbitrary"),
        name="branch_merge",
    )(*acts, *ws, gates, gates, gates)


LRU_CT = 512


def _lru_kernel(*refs, T, has_h0, has_state):
    xr_ref, xg_ref, cw_ref, cb_ref, wa_ref, ba_ref, wx_ref, bx_ref, lam_ref = refs[:9]
    refs = refs[9:]
    if has_h0:
        h0_ref = refs[0]
        refs = refs[1:]
    if has_state:
        y_ref, st_ref = refs[2:4]
        refs = refs[4:]
    else:
        y_ref = refs[1]
        refs = refs[2:]
    x3_ref, p_ref, a_ref, b_ref, hl_ref, pc_ref, hn_ref = refs
    nblk = xr_ref.shape[1] // LRU_BS
    S = SUBLANES
    G = T // S
    H = CONV_W - 1

    pitch = G + S
    for n in range(nblk):
        for k in range(S):
            x3_ref[n, k * pitch:k * pitch + G, :] = xr_ref[k * G:(k + 1) * G, n * LRU_BS:(n + 1) * LRU_BS]

    def gather(p, _):
        r0 = pl.multiple_of((p + H) * S, S)
        for n in range(nblk):
            p_ref.at[n][pl.ds(r0, S), :] = x3_ref.at[n][pl.ds(p, S, stride=pitch), :]
        return 0

    lax.fori_loop(0, G, gather, 0, unroll=4)
    row = lax.broadcasted_iota(jnp.int32, (S, LRU_BS), 0)
    for n in range(nblk):
        for i in range(H):
            tail = p_ref[n, (G + i) * S:(G + i + 1) * S, :]
            p_ref[n, i * S:(i + 1) * S, :] = jnp.where(row == 0, 0.0, pltpu.roll(tail, 1, 0))
            head = p_ref[n, (H + i) * S:(H + i + 1) * S, :]
            p_ref[n, (H + G + i) * S:(H + G + i + 1) * S, :] = jnp.where(row == S - 1, 0.0,
                                                                        pltpu.roll(head, S - 1, 0))

    cmats = []
    for d in range(2):
        for n in range(nblk):
            cs = slice(n * LRU_BS, (n + 1) * LRU_BS)
            xc = jnp.zeros((T, LRU_BS), F32) + cb_ref[d, :, cs]
            for j in range(CONV_W):
                off = j if d == 0 else 2 * H - j
                xc = xc + cw_ref[d, j:j + 1, cs] * p_ref[n, off * S:off * S + T, :]
            xcb = xc.astype(BF16)
            t_r = jnp.tanh(_dot(xcb, (0.5 * wa_ref[d, n]).astype(BF16)) + 0.5 * ba_ref[d, :, cs])
            t_i = jnp.tanh(_dot(xcb, (0.5 * wx_ref[d, n]).astype(BF16)) + 0.5 * bx_ref[d, :, cs])
            half_c = (-0.5 * LRU_C) * _softplus(-lam_ref[d, :, cs])
            a = jnp.exp(half_c + half_c * t_r)
            a_ref[n] = a
            b_ref[n] = (0.5 * jnp.sqrt(1.0 - a * a)) * (xc + t_i * xc)

        def body(it, carry, d=d):
            p = it if d == 0 else G - 1 - it
            r0 = pl.multiple_of(p * S, S)
            out = []
            for n in range(nblk):
                h, pc = carry[n]
                av = a_ref.at[n][pl.ds(r0, S), :]
                h = av * h + b_ref.at[n][pl.ds(r0, S), :]
                pc = av * pc
                hl_ref.at[d, n][pl.ds(r0, S), :] = h
                pc_ref.at[d, n][pl.ds(r0, S), :] = pc
                out.append((h, pc))
            return tuple(out)

        init = tuple((jnp.zeros((S, LRU_BS), F32), jnp.ones((S, LRU_BS), F32)) for _ in range(nblk))
        ends = lax.fori_loop(0, G, body, init, unroll=4)

        last_d, cmat_d = [], []
        for n in range(nblk):
            cs = slice(n * LRU_BS, (n + 1) * LRU_BS)
            h_end, pc_end = ends[n]
            c = h0_ref[d, :, cs] if has_h0 else jnp.zeros((1, LRU_BS), F32)
            rows = [None] * S
            for k in (range(S) if d == 0 else range(S - 1, -1, -1)):
                rows[k] = c
                c = h_end[k:k + 1, :] + pc_end[k:k + 1, :] * c
            last_d.append(c)
            cmat_d.append(jnp.concatenate(rows, axis=0))
        cmats.append(cmat_d)
        if has_state:
            for n in range(nblk):
                st_ref[d:d + 1, n * LRU_BS:(n + 1) * LRU_BS] = last_d[n]

    def scatter(p, _):
        r0 = pl.multiple_of(p * S, S)
        for n in range(nblk):
            h = (hl_ref.at[0, n][pl.ds(r0, S), :] + pc_ref.at[0, n][pl.ds(r0, S), :] * cmats[0][n]
                 + hl_ref.at[1, n][pl.ds(r0, S), :] + pc_ref.at[1, n][pl.ds(r0, S), :] * cmats[1][n])
            hn_ref.at[n][pl.ds(p, S, stride=pitch), :] = h
        return 0

    lax.fori_loop(0, G, scatter, 0, unroll=4)

    for n in range(nblk):
        cs = slice(n * LRU_BS, (n + 1) * LRU_BS)
        for k in range(S):
            xg = xg_ref[k * G:(k + 1) * G, cs]
            gelu = 0.5 * xg * (1.0 + jnp.tanh(GELU_TANH_SCALE * (xg + GELU_TANH_CUBIC * (xg * xg * xg))))
            y_ref[k * G:(k + 1) * G, cs] = (hn_ref[n, k * pitch:k * pitch + G, :] * gelu).astype(BF16)


def _lru(proj_a, lw, l, *, T, nb, blk0, prev, h0=None, st_prev=None):
    ct = LRU_CT
    ncb = D_RNN // ct
    bpc = ct // LRU_BS
    in_specs = [
        pl.BlockSpec((T, ct), lambda b, c: (blk0 + b, c)),
        pl.BlockSpec((T, ct), lambda b, c: (blk0 + b, ncb + c)),
        pl.BlockSpec((None, 2, CONV_W, ct), lambda b, c: (l, 0, 0, c)),
        pl.BlockSpec((None, 2, 1, ct), lambda b, c: (l, 0, 0, c)),
        pl.BlockSpec((None, 2, bpc, LRU_BS, LRU_BS), lambda b, c: (l, 0, c, 0, 0)),
        pl.BlockSpec((None, 2, 1, ct), lambda b, c: (l, 0, 0, c)),
        pl.BlockSpec((None, 2, bpc, LRU_BS, LRU_BS), lambda b, c: (l, 0, c, 0, 0)),
        pl.BlockSpec((None, 2, 1, ct), lambda b, c: (l, 0, 0, c)),
        pl.BlockSpec((None, 2, 1, ct), lambda b, c: (l, 0, 0, c)),
    ]
    args = [proj_a, proj_a, lw["conv_w"], lw["conv_b"], lw["wa"], lw["ba"], lw["wx"], lw["bx"], lw["lam"]]
    if h0 is not None:
        in_specs.append(pl.BlockSpec((None, None, 2, 1, ct), lambda b, c: (b, l, 0, 0, c)))
        args.append(h0)
    has_state = st_prev is not None
    aliases = {len(args): 0}
    in_specs.append(pl.BlockSpec(memory_space=pl.ANY))
    args.append(prev)
    out_specs = [pl.BlockSpec((T, ct), lambda b, c: (blk0 + b, c))]
    out_shape = [jax.ShapeDtypeStruct(prev.shape, prev.dtype)]
    if has_state:
        aliases[len(args)] = 1
        in_specs.append(pl.BlockSpec(memory_space=pl.ANY))
        args.append(st_prev)
        out_specs.append(pl.BlockSpec((None, None, 2, ct), lambda b, c: (b, l, 0, c)))
        out_shape.append(jax.ShapeDtypeStruct(st_prev.shape, st_prev.dtype))
    return pl.pallas_call(
        functools.partial(_lru_kernel, T=T, has_h0=h0 is not None, has_state=has_state),
        grid=(nb, ncb),
        in_specs=in_specs,
        out_specs=out_specs,
        out_shape=out_shape,
        input_output_aliases=aliases,
        scratch_shapes=[pltpu.VMEM((bpc, T + SUBLANES * SUBLANES, LRU_BS), F32),
                        pltpu.VMEM((bpc, T + 2 * (CONV_W - 1) * SUBLANES, LRU_BS), F32),
                        pltpu.VMEM((bpc, T, LRU_BS), F32), pltpu.VMEM((bpc, T, LRU_BS), F32),
                        pltpu.VMEM((2, bpc, T, LRU_BS), F32), pltpu.VMEM((2, bpc, T, LRU_BS), F32),
                        pltpu.VMEM((bpc, T + SUBLANES * SUBLANES, LRU_BS), F32)],
        compiler_params=_params("arbitrary", "arbitrary"),
        name="lru",
    )(*args)


def _mla_prep_kernel(p_ref, gq_ref, gkv_ref, cos_ref, sa_ref, sb_ref, *rest):
    cq_ref, ckvb_ref, kr_ref, ckv_out_ref, kr_out_ref = rest[-5:]
    i = pl.program_id(0)
    cq_ref[...] = _rms(p_ref[:, 0:MLA_Q_RANK], gq_ref[...]).astype(BF16)
    ckv = _rms(p_ref[:, MLA_Q_RANK:MLA_Q_RANK + MLA_KV_RANK], gkv_ref[...])
    ckvb_ref[...] = ckv.astype(BF16)
    k0 = MLA_Q_RANK + MLA_KV_RANK
    kr = p_ref[:, k0:k0 + LANES]

    @pl.when(i < N_CTX // ROW_TILE)
    def _():
        kr_ref[...] = kr
        ckv_out_ref[...] = ckv
        kr_out_ref[...] = kr[:, 0:MLA_ROPE]

    @pl.when(i >= N_CTX // ROW_TILE)
    def _():
        kr_ref[...] = _rope(kr, cos_ref[...], sa_ref[...], sb_ref[...], MLA_ROPE // 4)


def _lat_tile(i):
    return (jnp.maximum(i - N_CTX // ROW_TILE, 0)) % (DEC_SEQ // ROW_TILE)


def _mla_prep(proj_a, gq, gkv, l, tab, caches):
    tr = ROW_TILE
    assert tr == SEQ
    last_ctx = BATCH - 1
    tspec = pl.BlockSpec((tr, LANES), lambda i: (_lat_tile(i), 0))
    args = [proj_a, gq, gkv, *tab]
    in_specs = [pl.BlockSpec((tr, MLA_SLAB), lambda i: (i, 2 * D_RNN // MLA_SLAB)),
                pl.BlockSpec((None, 1, MLA_Q_RANK), lambda i: (l, 0, 0)),
                pl.BlockSpec((None, 1, MLA_KV_RANK), lambda i: (l, 0, 0)),
                tspec, tspec, tspec]
    aliases = {len(args): 3, len(args) + 1: 4}
    args += list(caches)
    in_specs += [pl.BlockSpec(memory_space=pl.ANY)] * 2
    return pl.pallas_call(
        _mla_prep_kernel,
        grid=(N_TOK // tr,),
        in_specs=in_specs,
        out_specs=[pl.BlockSpec((tr, MLA_Q_RANK), lambda i: (i, 0)),
                   pl.BlockSpec((tr, MLA_KV_RANK), lambda i: (i, 0)),
                   pl.BlockSpec((tr, LANES), lambda i: (i, 0)),
                   pl.BlockSpec((None, None, SEQ, MLA_KV_RANK), lambda i: (jnp.minimum(i, last_ctx), l, 0, 0)),
                   pl.BlockSpec((None, None, SEQ, MLA_ROPE), lambda i: (jnp.minimum(i, last_ctx), l, 0, 0))],
        out_shape=[jax.ShapeDtypeStruct((N_TOK, MLA_Q_RANK), BF16),
                   jax.ShapeDtypeStruct((N_TOK, MLA_KV_RANK), BF16),
                   jax.ShapeDtypeStruct((N_TOK, LANES), F32),
                   jax.ShapeDtypeStruct((BATCH, DEPTH, SEQ, MLA_KV_RANK), F32),
                   jax.ShapeDtypeStruct((BATCH, DEPTH, SEQ, MLA_ROPE), F32)],
        input_output_aliases=aliases,
        compiler_params=_params("arbitrary"),
        name="mla_prep",
    )(*args)


def _attn_kernel(*refs, lat):
    if lat:
        (q_ref, kvn_ref, krn_ref, kvc_ref, krc_ref, cos_ref, sa_ref, sb_ref, _,
         o_ref, kcat_ref, vcat_ref) = refs
    else:
        q_ref, kvn_ref, krn_ref, _, o_ref, kcat_ref = refs
    scale = (MLA_NOPE + MLA_ROPE) ** -0.5
    s_new = kvn_ref.shape[0]
    off = PAST_LEN if lat else 0

    def build():
        lane = lax.broadcasted_iota(jnp.int32, (s_new, LANES), 1)
        kr_new = jnp.where(lane < MLA_ROPE, krn_ref[...], 0.0).astype(BF16)
        for h in range(MLA_HEADS):
            c0 = h * MLA_QH
            kcat_ref[h, off:off + s_new, 0:MLA_NOPE] = kvn_ref[:, c0:c0 + MLA_NOPE]
            kcat_ref[h, off:off + s_new, MLA_NOPE:MLA_QH] = kr_new
            if lat:
                kcat_ref[h, 0:off, 0:MLA_NOPE] = kvc_ref[:, c0:c0 + MLA_NOPE]
                kcat_ref[h, 0:off, MLA_NOPE:MLA_NOPE + MLA_ROPE] = krc_ref[...].astype(BF16)
                kcat_ref[h, 0:off, MLA_NOPE + MLA_ROPE:MLA_QH] = jnp.zeros((off, MLA_QH - MLA_NOPE - MLA_ROPE), BF16)
                vcat_ref[0:off, h * MLA_V:(h + 1) * MLA_V] = kvc_ref[:, c0 + MLA_NOPE:c0 + MLA_QH]
                vcat_ref[off:off + s_new, h * MLA_V:(h + 1) * MLA_V] = kvn_ref[:, c0 + MLA_NOPE:c0 + MLA_QH]

    if lat:
        pl.when(pl.program_id(1) == 0)(build)
    else:
        build()

    for h in range(MLA_HEADS):
        c0 = h * MLA_QH
        qn = q_ref[:, c0:c0 + MLA_NOPE].astype(F32)
        qr = q_ref[:, c0 + MLA_NOPE:c0 + MLA_QH].astype(F32)
        if lat:
            qr = _rope(qr, cos_ref[...], sa_ref[...], sb_ref[...], MLA_ROPE // 4)
        qh = (jnp.concatenate([qn, qr], axis=1) * scale).astype(BF16)
        s = _dot_nt(qh, kcat_ref[h])
        p = jnp.exp(s - jnp.max(s, axis=-1, keepdims=True))
        den = jnp.sum(p, axis=-1, keepdims=True)
        v = vcat_ref[:, h * MLA_V:(h + 1) * MLA_V] if lat else kvn_ref[:, c0 + MLA_NOPE:c0 + MLA_QH]
        o_ref[:, h * MLA_V:(h + 1) * MLA_V] = (_dot(p.astype(BF16), v) / den).astype(BF16)


def _attn_ctx(q, kv, kr, prev):
    t = SEQ
    return pl.pallas_call(
        functools.partial(_attn_kernel, lat=False),
        grid=(BATCH,),
        in_specs=[pl.BlockSpec((t, MLA_HEADS * MLA_QH), lambda b: (b, 0)),
                  pl.BlockSpec((t, MLA_HEADS * MLA_QH), lambda b: (b, 0)),
                  pl.BlockSpec((t, LANES), lambda b: (b, 0)),
                  pl.BlockSpec(memory_space=pl.ANY)],
        out_specs=pl.BlockSpec((t, MLA_HEADS * MLA_V), lambda b: (b, 0)),
        out_shape=jax.ShapeDtypeStruct(prev.shape, prev.dtype),
        input_output_aliases={3: 0},
        scratch_shapes=[pltpu.VMEM((MLA_HEADS, t, MLA_QH), BF16)],
        compiler_params=_params("arbitrary"),
        name="attn_ctx",
    )(q, kv, kr, prev)


def _attn_lat(q, kv, kr, kv_cache, cache_krope, l, tab, prev):
    tq = ROW_TILE
    nq = DEC_SEQ // tq
    blk0 = N_CTX // DEC_SEQ
    qblk0 = N_CTX // tq
    tspec = pl.BlockSpec((tq, LANES), lambda b, i: (i, 0))
    return pl.pallas_call(
        functools.partial(_attn_kernel, lat=True),
        grid=(DEC_BATCH, nq),
        in_specs=[pl.BlockSpec((tq, MLA_HEADS * MLA_QH), lambda b, i: (qblk0 + b * nq + i, 0)),
                  pl.BlockSpec((DEC_SEQ, MLA_HEADS * MLA_QH), lambda b, i: (blk0 + b, 0)),
                  pl.BlockSpec((DEC_SEQ, LANES), lambda b, i: (blk0 + b, 0)),
                  pl.BlockSpec((PAST_LEN, MLA_HEADS * MLA_QH), lambda b, i: (b * DEPTH + l, 0)),
                  pl.BlockSpec((None, None, PAST_LEN, MLA_ROPE), lambda b, i: (b, l, 0, 0)),
                  tspec, tspec, tspec, pl.BlockSpec(memory_space=pl.ANY)],
        out_specs=pl.BlockSpec((tq, MLA_HEADS * MLA_V), lambda b, i: (qblk0 + b * nq + i, 0)),
        out_shape=jax.ShapeDtypeStruct((N_TOK, MLA_HEADS * MLA_V), BF16),
        input_output_aliases={8: 0},
        scratch_shapes=[pltpu.VMEM((MLA_HEADS, PAST_LEN + DEC_SEQ, MLA_QH), BF16),
                        pltpu.VMEM((PAST_LEN + DEC_SEQ, MLA_HEADS * MLA_V), BF16)],
        compiler_params=_params("arbitrary", "arbitrary"),
        name="attn_lat",
    )(q, kv, kr, kv_cache, cache_krope, *tab, prev)


def _log_sigmoid(x):
    return -_softplus(-x)


def _ret_kernel(*refs, lat, T):
    if lat:
        q_ref, k_ref, v_ref, rg_ref, dec_ref, s0_ref, _, o_ref, decay_ref = refs
    else:
        q_ref, k_ref, v_ref, rg_ref, dec_ref = refs[:5]
        o_ref, st_ref, decay_ref = refs[-3:]
    tq = q_ref.shape[0]
    t0 = pl.program_id(0) * tq if lat else 0
    batch_id = pl.program_id(1) if lat else pl.program_id(0)
    lg = _log_sigmoid(dec_ref[...])
    tcol = (t0 + lax.broadcasted_iota(jnp.int32, (tq, 1), 0)).astype(F32)

    @pl.when(batch_id == 0)
    def _():
        rows = (t0 + lax.broadcasted_iota(jnp.int32, (tq, T), 0)).astype(F32)
        cols = lax.broadcasted_iota(jnp.int32, (tq, T), 1).astype(F32)
        diff = rows - cols
        fwd = diff >= 0.0
        dist = jnp.abs(diff)
        for h in range(RET_HEADS):
            one_sided = jnp.exp(jnp.where(fwd, lg[0:1, h:h + 1], lg[1:2, h:h + 1]) * dist)
            decay_ref[h] = jnp.where(dist == 0.0, 2.0, one_sided)

    for h in range(RET_HEADS):
        cs = slice(h * RET_DK, (h + 1) * RET_DK)
        lgf = lg[0:1, h:h + 1]
        lgb = lg[1:2, h:h + 1]
        q = q_ref[:, cs]
        k = k_ref[:, cs]
        v = v_ref[:, cs]
        sc = _dot_nt(q, k) * decay_ref[h]
        o = _dot(sc.astype(BF16), v)
        if lat:
            qf = q.astype(F32)
            o = o + _dot((qf * jnp.exp(lgf * (tcol + 1.0))).astype(BF16), s0_ref[0, h].astype(BF16))
            o = o + _dot((qf * jnp.exp(lgb * (T - tcol))).astype(BF16), s0_ref[1, h].astype(BF16))
        else:
            vf = v.astype(F32)
            st_ref[0, h] = _dot_tn(k, (vf * jnp.exp(lgf * (T - 1.0 - tcol))).astype(BF16))
            st_ref[1, h] = _dot_tn(k, (vf * jnp.exp(lgb * tcol)).astype(BF16))
        mu = jnp.mean(o, axis=-1, keepdims=True)
        oc = o - mu
        var = jnp.mean(oc * oc, axis=-1, keepdims=True)
        rg = rg_ref[:, cs].astype(F32)
        o_ref[:, cs] = (oc * lax.rsqrt(var + EPS) * (rg * _sigmoid(rg))).astype(BF16)


def _ret_ctx(qk, vg, decay, l, prev, st_prev):
    t = SEQ
    w = RET_HEADS * RET_DK
    return pl.pallas_call(
        functools.partial(_ret_kernel, lat=False, T=t),
        grid=(BATCH,),
        in_specs=[pl.BlockSpec((t, w), lambda b: (b, 0)),
                  pl.BlockSpec((t, w), lambda b: (b, 1)),
                  pl.BlockSpec((t, w), lambda b: (b, 0)),
                  pl.BlockSpec((t, w), lambda b: (b, 1)),
                  pl.BlockSpec((None, 2, RET_HEADS), lambda b: (l, 0, 0)),
                  pl.BlockSpec(memory_space=pl.ANY), pl.BlockSpec(memory_space=pl.ANY)],
        out_specs=[pl.BlockSpec((t, w), lambda b: (b, 0)),
                   pl.BlockSpec((None, None, 2, RET_HEADS, RET_DK, RET_DV), lambda b: (b, l, 0, 0, 0, 0))],
        out_shape=[jax.ShapeDtypeStruct(prev.shape, prev.dtype),
                   jax.ShapeDtypeStruct(st_prev.shape, st_prev.dtype)],
        input_output_aliases={5: 0, 6: 1},
        scratch_shapes=[pltpu.VMEM((RET_HEADS, t, t), F32)],
        compiler_params=_params("arbitrary"),
        name="ret_ctx",
    )(qk, qk, vg, vg, decay, prev, st_prev)


def _ret_lat(qk, vg, decay, state_ret, l, prev):
    tq = LAT_Q_TILE
    t = DEC_SEQ
    nq = t // tq
    w = RET_HEADS * RET_DK
    blk0 = N_CTX // t
    qblk0 = N_CTX // tq
    return pl.pallas_call(
        functools.partial(_ret_kernel, lat=True, T=t),
        grid=(nq, DEC_BATCH),
        in_specs=[pl.BlockSpec((tq, w), lambda i, b: (b * nq + i, 0)),
                  pl.BlockSpec((t, w), lambda i, b: (b, 1)),
                  pl.BlockSpec((t, w), lambda i, b: (blk0 + b, 0)),
                  pl.BlockSpec((tq, w), lambda i, b: (qblk0 + b * nq + i, 1)),
                  pl.BlockSpec((None, 2, RET_HEADS), lambda i, b: (l, 0, 0)),
                  pl.BlockSpec((None, None, 2, RET_HEADS, RET_DK, RET_DV), lambda i, b: (b, l, 0, 0, 0, 0)),
                  pl.BlockSpec(memory_space=pl.ANY)],
        out_specs=pl.BlockSpec((tq, w), lambda i, b: (qblk0 + b * nq + i, 0)),
        out_shape=jax.ShapeDtypeStruct((N_TOK, w), BF16),
        scratch_shapes=[pltpu.VMEM((RET_HEADS, tq, t), F32)],
        input_output_aliases={6: 0},
        compiler_params=_params("arbitrary", "arbitrary"),
        name="ret_lat",
    )(qk, qk, vg, vg, decay, state_ret, prev)


def _rope_tables(dim):
    half = dim // 2
    quarter = half // 2
    t = jnp.arange(DEC_SEQ)
    row = (t // GRID_W).astype(F32)
    col = (t % GRID_W).astype(F32)
    inv = ROPE_BASE ** (-jnp.arange(0, half, 2, dtype=F32) / half)
    zeros = jnp.zeros((DEC_SEQ, quarter), F32)
    cos_parts, sa_parts, sb_parts = [], [], []
    for pos in (row, col):
        ang = pos[:, None] * inv[None, :]
        c, s = jnp.cos(ang), jnp.sin(ang)
        cos_parts += [c, c]
        sa_parts += [zeros, s]
        sb_parts += [-s, zeros]
    padw = LANES - dim

    def build(parts):
        return jnp.concatenate(parts + ([jnp.zeros((DEC_SEQ, padw), F32)] if padw else []), axis=-1)

    return build(cos_parts), build(sa_parts), build(sb_parts)


def kernel(x_prompt, x_sample, c, cache_mla_ckv, cache_mla_krope, state_lru, state_ret, c_ctx, w_mod, b_mod,
           g_norm, w_in, lru_conv_w, lru_conv_b, lru_wa, lru_ba, lru_wx, lru_bx, lru_lam, mla_gq, mla_gkv,
           mla_wuq, mla_wukv, ret_decay, w_br_lru, w_br_mla, w_br_ret, w_out, w_ff1, w_ff2):
    x_ctx = x_prompt.reshape(N_CTX, D_MODEL)
    x_lat = x_sample.reshape(N_LAT, D_MODEL)
    cond8 = jnp.concatenate([c_ctx[None, :], c, jnp.zeros((8 - 1 - DEC_BATCH, D_MODEL), F32)], axis=0)
    mod = _ada(cond8, w_mod, b_mod).reshape(DEPTH * 8, 1, 6 * D_MODEL)
    g_all = g_norm.reshape(DEPTH * 4, 1, D_MODEL)

    w_in_t = jnp.swapaxes(w_in, 1, 2)
    wuq = jnp.pad(mla_wuq.reshape(DEPTH, MLA_Q_RANK, MLA_HEADS, MLA_NOPE + MLA_ROPE),
                  ((0, 0), (0, 0), (0, 0), (0, MLA_QH - MLA_NOPE - MLA_ROPE)))
    wuq = wuq.reshape(DEPTH, MLA_Q_RANK, MLA_HEADS * MLA_QH)
    lw = {"conv_w": lru_conv_w, "conv_b": lru_conv_b.reshape(DEPTH, 2, 1, D_RNN), "wa": lru_wa,
          "ba": lru_ba.reshape(DEPTH, 2, 1, D_RNN), "wx": lru_wx, "bx": lru_bx.reshape(DEPTH, 2, 1, D_RNN),
          "lam": lru_lam.reshape(DEPTH, 2, 1, D_RNN)}
    gq = mla_gq.reshape(DEPTH, 1, MLA_Q_RANK)
    gkv = mla_gkv.reshape(DEPTH, 1, MLA_KV_RANK)
    h0_lat = state_lru.reshape(DEC_BATCH, DEPTH, 2, 1, D_RNN)
    cache_ckv_rows = cache_mla_ckv.reshape(DEC_BATCH * DEPTH * PAST_LEN, MLA_KV_RANK)
    tab_mla = _rope_tables(MLA_ROPE)
    tab_ret = _rope_tables(RET_DK)
    wq = RET_HEADS * RET_DK

    caches = (jnp.zeros((BATCH, DEPTH, SEQ, MLA_KV_RANK), F32), jnp.zeros((BATCH, DEPTH, SEQ, MLA_ROPE), F32))
    st_lru = jnp.zeros((BATCH, DEPTH, 2, D_RNN), F32)
    st_ret = jnp.zeros((BATCH, DEPTH, 2, RET_HEADS, RET_DK, RET_DV), F32)
    y_lru = jnp.zeros((N_TOK, D_RNN), BF16)
    y_mla = jnp.zeros((N_TOK, MLA_HEADS * MLA_V), BF16)
    y_ret = jnp.zeros((N_TOK, RET_HEADS * RET_DV), BF16)

    kv_cache = _mm(cache_ckv_rows, mla_wukv, lambda i: i % DEPTH, out_dtype=BF16, tm=PAST_LEN)

    h, = _resnorm(x_ctx, x_lat, mod, g_all, nxt=(0, 0, 0))
    for l in range(DEPTH):
        proj_a = _mm(h, w_in_t, l, col0=0, ncols=2 * D_RNN + MLA_SLAB, w_t=True)
        kscale = (wq, RET_DK ** -0.5)
        qk_ctx = _mm(h, w_in_t, l, col0=COL_TAIL, ncols=2 * wq, out_dtype=BF16, m=N_CTX, kscale=kscale,
                     w_t=True)
        qk_lat = _mm(h, w_in_t, l, col0=COL_TAIL, ncols=2 * wq, out_dtype=BF16, m=N_LAT, tm=DEC_SEQ,
                     row_map=lambda i: N_CTX // DEC_SEQ + i, kscale=kscale,
                     rope=(tab_ret, RET_DK // 4), w_t=True)
        vg = _mm(h, w_in_t, l, col0=COL_TAIL + 2 * wq, ncols=2 * wq + 3 * D_MODEL, out_dtype=BF16,
                 act=("sigmoid_from", 2 * wq), w_t=True, tn=1024)

        y_lru, st_lru = _lru(proj_a, lw, l, T=SEQ, nb=BATCH, blk0=0, prev=y_lru, st_prev=st_lru)
        y_lru, = _lru(proj_a, lw, l, T=DEC_SEQ, nb=DEC_BATCH, blk0=N_CTX // DEC_SEQ, h0=h0_lat, prev=y_lru)

        cqn, ckvb, kr, *caches = _mla_prep(proj_a, gq, gkv, l, tab_mla, caches)
        q = _mm(cqn, wuq, l, out_dtype=BF16)
        kv = _mm(ckvb, mla_wukv, l, out_dtype=BF16)
        y_mla = _attn_ctx(q, kv, kr, y_mla)
        y_mla = _attn_lat(q, kv, kr, kv_cache, cache_mla_krope, l, tab_mla, y_mla)

        y_ret, st_ret = _ret_ctx(qk_ctx, vg, ret_decay, l, y_ret, st_ret)
        y_ret = _ret_lat(qk_lat, vg, ret_decay, state_ret, l, y_ret)

        merged = _branch_merge((y_lru, y_mla, y_ret), (w_br_lru, w_br_mla, w_br_ret), vg, l, gate_col0=2 * wq)
        u = _mm(merged, w_out, l)
        x_ctx, x_lat, h2 = _resnorm(x_ctx, x_lat, mod, g_all, res=(u, l * 4 + 1, l, 2), nxt=(l * 4 + 2, l, 3))
        ff = _mm(h2, w_ff1, l, out_dtype=BF16, act="relu2", tn=1024)
        y = _mm(ff, w_ff2, l, tn=512, tk=2048)
        if l + 1 < DEPTH:
            x_ctx, x_lat, h = _resnorm(x_ctx, x_lat, mod, g_all, res=(y, l * 4 + 3, l, 5),
                                       nxt=((l + 1) * 4, l + 1, 0))
        else:
            x_ctx, x_lat = _resnorm(x_ctx, x_lat, mod, g_all, res=(y, l * 4 + 3, l, 5))

    new_ckv, new_krope = caches
    return (x_ctx.reshape(BATCH, SEQ, D_MODEL), x_lat.reshape(DEC_BATCH, DEC_SEQ, D_MODEL),
            new_ckv, new_krope, st_lru, st_ret)
```

```python
import functools

import jax
import jax.numpy as jnp
from jax import lax
from jax.experimental import pallas as pl
from jax.experimental.pallas import tpu as pltpu

F32 = jnp.float32
BF16 = jnp.bfloat16

D_MODEL = 2048
BATCH = 16
SEQ = 256
DEPTH = 4
DEC_BATCH = 4
DEC_SEQ = 1024
PAST_LEN = 256
GRID_W = 64
EPS = 1e-6
ROPE_BASE = 10000.0
D_RNN = D_MODEL // 2
LRU_BLOCKS = 8
LRU_BS = D_RNN // LRU_BLOCKS
CONV_W = 4
LRU_C = 8.0
MLA_HEADS = 8
MLA_NOPE = 128
MLA_ROPE = 64
MLA_V = 128
MLA_Q_RANK = D_MODEL // 4
MLA_KV_RANK = D_MODEL // 8
RET_HEADS = 8
RET_DK = 128
RET_DV = 128
D_FF = 4 * D_MODEL

N_CTX = BATCH * SEQ
N_LAT = DEC_BATCH * DEC_SEQ
N_TOK = N_CTX + N_LAT
COL_TAIL = 2 * D_RNN + MLA_Q_RANK + MLA_KV_RANK + MLA_ROPE
MLA_QH = 256
MLA_SLAB = 1024
GELU_TANH_SCALE = 0.7978845608028654
GELU_TANH_CUBIC = 0.044715

LANES = 128
SUBLANES = 8
VMEM_LIMIT = 56 * 1024 * 1024
ROW_TILE = 256
RES_TILE = 512
LAT_Q_TILE = 512


def _params(*sem):
    return pltpu.CompilerParams(dimension_semantics=sem, vmem_limit_bytes=VMEM_LIMIT)


def _sigmoid(x):
    return 0.5 * (1.0 + jnp.tanh(0.5 * x))


def _softplus(x):
    return jnp.maximum(x, 0.0) + jnp.log(1.0 + jnp.exp(-jnp.abs(x)))


def _rms(x, g):
    return x * lax.rsqrt(jnp.mean(x * x, axis=-1, keepdims=True) + EPS) * g


def _dot(a, b):
    return jnp.dot(a, b, preferred_element_type=F32)


def _dot_nt(a, b):
    return lax.dot_general(a, b, (((1,), (1,)), ((), ())), preferred_element_type=F32)


def _dot_tn(a, b):
    return lax.dot_general(a, b, (((0,), (0,)), ((), ())), preferred_element_type=F32)


def _rope(x, cos, sa, sb, quarter):
    w = x.shape[-1]
    reps = w // cos.shape[-1]
    if reps > 1:
        cos, sa, sb = (jnp.tile(t, (1, reps)) for t in (cos, sa, sb))
    return x * cos + pltpu.roll(x, quarter, 1) * sa + pltpu.roll(x, w - quarter, 1) * sb


def _mod_row(row0):
    return jnp.where(row0 < N_CTX, 0, 1 + (row0 - N_CTX) // DEC_SEQ)


def _ada_kernel(c_ref, w_ref, b_ref, o_ref):
    c = c_ref[...]
    s = c * _sigmoid(c)
    o_ref[...] = _dot(s.astype(BF16), w_ref[...].astype(BF16)) + b_ref[...]


def _ada(cond8, w_mod, b_mod):
    tn = 1024
    n = w_mod.shape[-1]
    return pl.pallas_call(
        _ada_kernel,
        grid=(DEPTH, n // tn),
        in_specs=[pl.BlockSpec((8, D_MODEL), lambda l, j: (0, 0)),
                  pl.BlockSpec((None, D_MODEL, tn), lambda l, j: (l, 0, j)),
                  pl.BlockSpec((None, 1, tn), lambda l, j: (l, 0, j))],
        out_specs=pl.BlockSpec((None, 8, tn), lambda l, j: (l, 0, j)),
        out_shape=jax.ShapeDtypeStruct((DEPTH, 8, n), F32),
        compiler_params=_params("arbitrary", "arbitrary"),
        name="ada",
    )(cond8, w_mod, b_mod.reshape(DEPTH, 1, n))


def _resnorm_kernel(*refs, has_res, has_next):
    refs = list(refs)
    xc_ref, xl_ref = refs[:2]
    refs = refs[2:]
    is_ctx = pl.program_id(0) < N_CTX // RES_TILE
    x = jnp.where(is_ctx, xc_ref[...], xl_ref[...])
    if has_res:
        u_ref, gpost_ref, gate_ref = refs[:3]
        refs = refs[3:]
        x = x + gate_ref[...] * _rms(u_ref[...], gpost_ref[...])
    if has_next:
        gpre_ref, shift_ref, scale_ref = refs[:3]
        refs = refs[3:]
    if has_res:
        xco_ref, xlo_ref = refs[:2]
        refs = refs[2:]

        @pl.when(is_ctx)
        def _():
            xco_ref[...] = x

        @pl.when(jnp.logical_not(is_ctx))
        def _():
            xlo_ref[...] = x
    if has_next:
        ho_ref = refs.pop(0)
        h = _rms(x, gpre_ref[...]) * (1.0 + scale_ref[...]) + shift_ref[...]
        ho_ref[...] = h.astype(BF16)


def _resnorm(x_ctx, x_lat, mod, g_norm, *, res=None, nxt=None):
    tr = RES_TILE
    nctx = N_CTX // tr
    row = pl.BlockSpec((tr, D_MODEL), lambda i: (i, 0))
    crow = pl.BlockSpec((tr, D_MODEL), lambda i: (jnp.minimum(i, nctx - 1), 0))
    lrow = pl.BlockSpec((tr, D_MODEL), lambda i: (jnp.maximum(i - nctx, 0), 0))

    def gspec(k):
        return pl.BlockSpec((None, 1, D_MODEL), lambda i: (k, 0, 0))

    def mspec(l, chunk):
        return pl.BlockSpec((None, 1, D_MODEL), lambda i: (l * 8 + _mod_row(i * tr), 0, chunk))

    args, specs, outs, ospecs = [x_ctx, x_lat], [crow, lrow], [], []
    if res is not None:
        u, gk, l, chunk = res
        args += [u, g_norm, mod]
        specs += [row, gspec(gk), mspec(l, chunk)]
        outs += [jax.ShapeDtypeStruct((N_CTX, D_MODEL), F32), jax.ShapeDtypeStruct((N_LAT, D_MODEL), F32)]
        ospecs += [crow, lrow]
    if nxt is not None:
        gk, l, chunk = nxt
        args += [g_norm, mod, mod]
        specs += [gspec(gk), mspec(l, chunk), mspec(l, chunk + 1)]
        outs.append(jax.ShapeDtypeStruct((N_TOK, D_MODEL), BF16))
        ospecs.append(row)
    return pl.pallas_call(
        functools.partial(_resnorm_kernel, has_res=res is not None, has_next=nxt is not None),
        grid=(N_TOK // tr,),
        in_specs=specs, out_specs=ospecs, out_shape=outs,
        compiler_params=_params("arbitrary"),
        name="resnorm",
    )(*args)


def _mm_kernel(x_ref, w_ref, *rest, nk, act, kscale, rope_quarter, w_t):
    o_ref = rest[-1]
    if nk == 1:
        if w_t:
            part = _dot_nt(x_ref[...].astype(BF16), w_ref[0].astype(BF16))
        else:
            part = _dot(x_ref[...].astype(BF16), w_ref[...].astype(BF16))
        if act == "relu2":
            part = jnp.square(jnp.maximum(part, 0.0))
        elif act == "sigmoid":
            part = _sigmoid(part)
        elif act is not None:
            part = jnp.where(pl.program_id(1) >= act[1], _sigmoid(part), part)
        if kscale is not None:
            part = part * jnp.where(pl.program_id(1) >= kscale[0], kscale[1], 1.0)
        if rope_quarter is not None:
            cos_ref, sa_ref, sb_ref = rest[:3]
            part = _rope(part, cos_ref[...], sa_ref[...], sb_ref[...], rope_quarter)
        o_ref[...] = part.astype(o_ref.dtype)
        return

    @pl.when(pl.program_id(2) == 0)
    def _():
        o_ref[...] = jnp.zeros(o_ref.shape, o_ref.dtype)

    o_ref[...] += _dot(x_ref[...].astype(BF16), w_ref[...].astype(BF16))


def _mm(x, w, l, *, col0=0, ncols=None, out_dtype=F32, act=None, tm=2048, tn=512, tk=None,
        m=None, row_map=None, kscale=None, rope=None, w_t=False):
    kdim = w.shape[2] if w_t else w.shape[1]
    ncols = (w.shape[1] if w_t else w.shape[2]) - col0 if ncols is None else ncols
    m = x.shape[0] if m is None else m
    tk = kdim if tk is None else tk
    tm, tn = min(tm, m), min(tn, ncols)
    assert m % tm == 0 and ncols % tn == 0 and kdim % tk == 0
    nk = kdim // tk
    assert nk == 1 or (act is None and out_dtype == F32 and kscale is None and rope is None and not w_t)
    row_map = (lambda i: i) if row_map is None else row_map
    args = [x, w]
    if w_t:
        assert col0 % SUBLANES == 0 and tn % SUBLANES == 0
        wspec = pl.BlockSpec((pl.Element(1), pl.Element(tn), pl.Element(tk)),
                             lambda i, j, k: (l, pl.multiple_of(col0 + j * tn, SUBLANES), 0))
    else:
        assert col0 % tn == 0
        jb = col0 // tn
        layer = l if callable(l) else (lambda i: l)
        wspec = pl.BlockSpec((None, tk, tn), lambda i, j, k: (layer(i), k, jb + j))
    in_specs = [pl.BlockSpec((tm, tk), lambda i, j, k: (row_map(i), k)), wspec]
    if kscale is not None:
        assert kscale[0] % tn == 0
        kscale = (kscale[0] // tn, kscale[1])
    if isinstance(act, tuple):
        assert act[1] % tn == 0
        act = (act[0], act[1] // tn)
    if rope is not None:
        tabs, quarter = rope
        assert tm == DEC_SEQ and tabs[0].shape == (DEC_SEQ, LANES)
        args += list(tabs)
        in_specs += [pl.BlockSpec((tm, LANES), lambda i, j, k: (0, 0))] * 3
    return pl.pallas_call(
        functools.partial(_mm_kernel, nk=nk, act=act, kscale=kscale,
                          rope_quarter=None if rope is None else rope[1], w_t=w_t),
        grid=(m // tm, ncols // tn, nk),
        in_specs=in_specs,
        out_specs=pl.BlockSpec((tm, tn), lambda i, j, k: (i, j)),
        out_shape=jax.ShapeDtypeStruct((m, ncols), out_dtype),
        compiler_params=_params("arbitrary", "arbitrary", "arbitrary"),
        name="mm",
    )(*args)


def _bm_kernel(a0_ref, a1_ref, a2_ref, w0_ref, w1_ref, w2_ref, g0_ref, g1_ref, g2_ref, o_ref):
    acc = g0_ref[...].astype(F32) * _dot(a0_ref[...], w0_ref[...].astype(BF16))
    acc = acc + g1_ref[...].astype(F32) * _dot(a1_ref[...], w1_ref[...].astype(BF16))
    acc = acc + g2_ref[...].astype(F32) * _dot(a2_ref[...], w2_ref[...].astype(BF16))
    o_ref[...] = acc.astype(BF16)


def _branch_merge(acts, ws, gates, l, *, gate_col0=0, tm=1024, tn=512):
    nj = D_MODEL // tn
    kdim = acts[0].shape[1]
    assert gate_col0 % tn == 0
    g0 = gate_col0 // tn
    aspec = pl.BlockSpec((tm, kdim), lambda i, j: (i, 0))
    wspec = pl.BlockSpec((None, kdim, tn), lambda i, j: (l, 0, j))
    gspecs = [pl.BlockSpec((tm, tn), lambda i, j, b=b: (i, g0 + b * nj + j)) for b in range(3)]
    return pl.pallas_call(
        _bm_kernel,
        grid=(N_TOK // tm, nj),
        in_specs=[aspec] * 3 + [wspec] * 3 + gspecs,
        out_specs=pl.BlockSpec((tm, tn), lambda i, j: (i, j)),
        out_shape=jax.ShapeDtypeStruct((N_TOK, D_MODEL), BF16),
        compiler_params=_params("arbitrary", "arbitrary"),
        name="branch_merge",
    )(*acts, *ws, gates, gates, gates)


LRU_CT = 512


def _lru_kernel(*refs, T, has_h0, has_state):
    xr_ref, xg_ref, cw_ref, cb_ref, wa_ref, ba_ref, wx_ref, bx_ref, lam_ref = refs[:9]
    refs = refs[9:]
    if has_h0:
        h0_ref = refs[0]
        refs = refs[1:]
    if has_state:
        y_ref, st_ref = refs[2:4]
        refs = refs[4:]
    else:
        y_ref = refs[1]
        refs = refs[2:]
    x3_ref, p_ref, a_ref, b_ref, hl_ref, pc_ref, hn_ref = refs
    nblk = xr_ref.shape[1] // LRU_BS
    S = SUBLANES
    G = T // S
    H = CONV_W - 1

    pitch = G + S
    for n in range(nblk):
        for k in range(S):
            x3_ref[n, k * pitch:k * pitch + G, :] = xr_ref[k * G:(k + 1) * G, n * LRU_BS:(n + 1) * LRU_BS]

    def gather(p, _):
        r0 = pl.multiple_of((p + H) * S, S)
        for n in range(nblk):
            p_ref.at[n][pl.ds(r0, S), :] = x3_ref.at[n][pl.ds(p, S, stride=pitch), :]
        return 0

    lax.fori_loop(0, G, gather, 0, unroll=4)
    row = lax.broadcasted_iota(jnp.int32, (S, LRU_BS), 0)
    for n in range(nblk):
        for i in range(H):
            tail = p_ref[n, (G + i) * S:(G + i + 1) * S, :]
            p_ref[n, i * S:(i + 1) * S, :] = jnp.where(row == 0, 0.0, pltpu.roll(tail, 1, 0))
            head = p_ref[n, (H + i) * S:(H + i + 1) * S, :]
            p_ref[n, (H + G + i) * S:(H + G + i + 1) * S, :] = jnp.where(row == S - 1, 0.0,
                                                                        pltpu.roll(head, S - 1, 0))

    cmats = []
    for d in range(2):
        for n in range(nblk):
            cs = slice(n * LRU_BS, (n + 1) * LRU_BS)
            xc = jnp.zeros((T, LRU_BS), F32) + cb_ref[d, :, cs]
            for j in range(CONV_W):
                off = j if d == 0 else 2 * H - j
                xc = xc + cw_ref[d, j:j + 1, cs] * p_ref[n, off * S:off * S + T, :]
            xcb = xc.astype(BF16)
            t_r = jnp.tanh(_dot(xcb, (0.5 * wa_ref[d, n]).astype(BF16)) + 0.5 * ba_ref[d, :, cs])
            t_i = jnp.tanh(_dot(xcb, (0.5 * wx_ref[d, n]).astype(BF16)) + 0.5 * bx_ref[d, :, cs])
            half_c = (-0.5 * LRU_C) * _softplus(-lam_ref[d, :, cs])
            a = jnp.exp(half_c + half_c * t_r)
            a_ref[n] = a
            b_ref[n] = (0.5 * jnp.sqrt(1.0 - a * a)) * (xc + t_i * xc)

        def body(it, carry, d=d):
            p = it if d == 0 else G - 1 - it
            r0 = pl.multiple_of(p * S, S)
            out = []
            for n in range(nblk):
                h, pc = carry[n]
                av = a_ref.at[n][pl.ds(r0, S), :]
                h = av * h + b_ref.at[n][pl.ds(r0, S), :]
                pc = av * pc
                hl_ref.at[d, n][pl.ds(r0, S), :] = h
                pc_ref.at[d, n][pl.ds(r0, S), :] = pc
                out.append((h, pc))
            return tuple(out)

        init = tuple((jnp.zeros((S, LRU_BS), F32), jnp.ones((S, LRU_BS), F32)) for _ in range(nblk))
        ends = lax.fori_loop(0, G, body, init, unroll=4)

        last_d, cmat_d = [], []
        for n in range(nblk):
            cs = slice(n * LRU_BS, (n + 1) * LRU_BS)
            h_end, pc_end = ends[n]
            c = h0_ref[d, :, cs] if has_h0 else jnp.zeros((1, LRU_BS), F32)
            rows = [None] * S
            for k in (range(S) if d == 0 else range(S - 1, -1, -1)):
                rows[k] = c
                c = h_end[k:k + 1, :] + pc_end[k:k + 1, :] * c
            last_d.append(c)
            cmat_d.append(jnp.concatenate(rows, axis=0))
        cmats.append(cmat_d)
        if has_state:
            for n in range(nblk):
                st_ref[d:d + 1, n * LRU_BS:(n + 1) * LRU_BS] = last_d[n]

    def scatter(p, _):
        r0 = pl.multiple_of(p * S, S)
        for n in range(nblk):
            h = (hl_ref.at[0, n][pl.ds(r0, S), :] + pc_ref.at[0, n][pl.ds(r0, S), :] * cmats[0][n]
                 + hl_ref.at[1, n][pl.ds(r0, S), :] + pc_ref.at[1, n][pl.ds(r0, S), :] * cmats[1][n])
            hn_ref.at[n][pl.ds(p, S, stride=pitch), :] = h
        return 0

    lax.fori_loop(0, G, scatter, 0, unroll=4)

    for n in range(nblk):
        cs = slice(n * LRU_BS, (n + 1) * LRU_BS)
        for k in range(S):
            xg = xg_ref[k * G:(k + 1) * G, cs]
            gelu = 0.5 * xg * (1.0 + jnp.tanh(GELU_TANH_SCALE * (xg + GELU_TANH_CUBIC * (xg * xg * xg))))
            y_ref[k * G:(k + 1) * G, cs] = (hn_ref[n, k * pitch:k * pitch + G, :] * gelu).astype(BF16)


def _lru(proj_a, lw, l, *, T, nb, blk0, prev, h0=None, st_prev=None):
    ct = LRU_CT
    ncb = D_RNN // ct
    bpc = ct // LRU_BS
    in_specs = [
        pl.BlockSpec((T, ct), lambda b, c: (blk0 + b, c)),
        pl.BlockSpec((T, ct), lambda b, c: (blk0 + b, ncb + c)),
        pl.BlockSpec((None, 2, CONV_W, ct), lambda b, c: (l, 0, 0, c)),
        pl.BlockSpec((None, 2, 1, ct), lambda b, c: (l, 0, 0, c)),
        pl.BlockSpec((None, 2, bpc, LRU_BS, LRU_BS), lambda b, c: (l, 0, c, 0, 0)),
        pl.BlockSpec((None, 2, 1, ct), lambda b, c: (l, 0, 0, c)),
        pl.BlockSpec((None, 2, bpc, LRU_BS, LRU_BS), lambda b, c: (l, 0, c, 0, 0)),
        pl.BlockSpec((None, 2, 1, ct), lambda b, c: (l, 0, 0, c)),
        pl.BlockSpec((None, 2, 1, ct), lambda b, c: (l, 0, 0, c)),
    ]
    args = [proj_a, proj_a, lw["conv_w"], lw["conv_b"], lw["wa"], lw["ba"], lw["wx"], lw["bx"], lw["lam"]]
    if h0 is not None:
        in_specs.append(pl.BlockSpec((None, None, 2, 1, ct), lambda b, c: (b, l, 0, 0, c)))
        args.append(h0)
    has_state = st_prev is not None
    aliases = {len(args): 0}
    in_specs.append(pl.BlockSpec(memory_space=pl.ANY))
    args.append(prev)
    out_specs = [pl.BlockSpec((T, ct), lambda b, c: (blk0 + b, c))]
    out_shape = [jax.ShapeDtypeStruct(prev.shape, prev.dtype)]
    if has_state:
        aliases[len(args)] = 1
        in_specs.append(pl.BlockSpec(memory_space=pl.ANY))
        args.append(st_prev)
        out_specs.append(pl.BlockSpec((None, None, 2, ct), lambda b, c: (b, l, 0, c)))
        out_shape.append(jax.ShapeDtypeStruct(st_prev.shape, st_prev.dtype))
    return pl.pallas_call(
        functools.partial(_lru_kernel, T=T, has_h0=h0 is not None, has_state=has_state),
        grid=(nb, ncb),
        in_specs=in_specs,
        out_specs=out_specs,
        out_shape=out_shape,
        input_output_aliases=aliases,
        scratch_shapes=[pltpu.VMEM((bpc, T + SUBLANES * SUBLANES, LRU_BS), F32),
                        pltpu.VMEM((bpc, T + 2 * (CONV_W - 1) * SUBLANES, LRU_BS), F32),
                        pltpu.VMEM((bpc, T, LRU_BS), F32), pltpu.VMEM((bpc, T, LRU_BS), F32),
                        pltpu.VMEM((2, bpc, T, LRU_BS), F32), pltpu.VMEM((2, bpc, T, LRU_BS), F32),
                        pltpu.VMEM((bpc, T + SUBLANES * SUBLANES, LRU_BS), F32)],
        compiler_params=_params("arbitrary", "arbitrary"),
        name="lru",
    )(*args)


MLA_TILE = DEC_SEQ
MLA_TILE_SEQS = MLA_TILE // SEQ


def _mla_front_kernel(p_ref, gq_ref, gkv_ref, wuq_ref, wukv_ref, cos_ref, sa_ref, sb_ref, *rest):
    q_ref, kv_ref, kr_ref, ckv_out_ref, kr_out_ref = rest[-5:]
    i = pl.program_id(0)
    cq = _rms(p_ref[:, 0:MLA_Q_RANK], gq_ref[...]).astype(BF16)
    q_ref[...] = _dot(cq, wuq_ref[...].astype(BF16)).astype(BF16)
    ckv = _rms(p_ref[:, MLA_Q_RANK:MLA_Q_RANK + MLA_KV_RANK], gkv_ref[...])
    kv_ref[...] = _dot(ckv.astype(BF16), wukv_ref[...].astype(BF16)).astype(BF16)
    k0 = MLA_Q_RANK + MLA_KV_RANK
    kr = p_ref[:, k0:k0 + LANES]

    @pl.when(i < N_CTX // MLA_TILE)
    def _():
        kr_ref[...] = kr
        for s in range(MLA_TILE_SEQS):
            ckv_out_ref[s] = ckv[s * SEQ:(s + 1) * SEQ, :]
            kr_out_ref[s] = kr[s * SEQ:(s + 1) * SEQ, 0:MLA_ROPE]

    @pl.when(i >= N_CTX // MLA_TILE)
    def _():
        kr_ref[...] = _rope(kr, cos_ref[...], sa_ref[...], sb_ref[...], MLA_ROPE // 4)


def _mla_front(proj_a, gq, gkv, wuq, wukv, l, tab, caches):
    tr = MLA_TILE
    last_ctx = N_CTX // tr - 1
    tspec = pl.BlockSpec((tr, LANES), lambda i: (0, 0))
    kvw = MLA_HEADS * MLA_QH
    args = [proj_a, gq, gkv, wuq, wukv, *tab]
    in_specs = [pl.BlockSpec((tr, MLA_SLAB), lambda i: (i, 2 * D_RNN // MLA_SLAB)),
                pl.BlockSpec((None, 1, MLA_Q_RANK), lambda i: (l, 0, 0)),
                pl.BlockSpec((None, 1, MLA_KV_RANK), lambda i: (l, 0, 0)),
                pl.BlockSpec((None, MLA_Q_RANK, kvw), lambda i: (l, 0, 0)),
                pl.BlockSpec((None, MLA_KV_RANK, kvw), lambda i: (l, 0, 0)),
                tspec, tspec, tspec]
    aliases = {len(args): 3, len(args) + 1: 4}
    args += list(caches)
    in_specs += [pl.BlockSpec(memory_space=pl.ANY)] * 2
    return pl.pallas_call(
        _mla_front_kernel,
        grid=(N_TOK // tr,),
        in_specs=in_specs,
        out_specs=[pl.BlockSpec((tr, kvw), lambda i: (i, 0)),
                   pl.BlockSpec((tr, kvw), lambda i: (i, 0)),
                   pl.BlockSpec((tr, LANES), lambda i: (i, 0)),
                   pl.BlockSpec((MLA_TILE_SEQS, None, SEQ, MLA_KV_RANK),
                                lambda i: (jnp.minimum(i, last_ctx), l, 0, 0)),
                   pl.BlockSpec((MLA_TILE_SEQS, None, SEQ, MLA_ROPE),
                                lambda i: (jnp.minimum(i, last_ctx), l, 0, 0))],
        out_shape=[jax.ShapeDtypeStruct((N_TOK, kvw), BF16),
                   jax.ShapeDtypeStruct((N_TOK, kvw), BF16),
                   jax.ShapeDtypeStruct((N_TOK, LANES), F32),
                   jax.ShapeDtypeStruct((BATCH, DEPTH, SEQ, MLA_KV_RANK), F32),
                   jax.ShapeDtypeStruct((BATCH, DEPTH, SEQ, MLA_ROPE), F32)],
        input_output_aliases=aliases,
        compiler_params=_params("arbitrary"),
        name="mla_front",
    )(*args)


def _attn_kernel(*refs, lat):
    if lat:
        (q_ref, kvn_ref, krn_ref, kvc_ref, krc_ref, cos_ref, sa_ref, sb_ref, _,
         o_ref, kcat_ref, vcat_ref) = refs
    else:
        q_ref, kvn_ref, krn_ref, _, o_ref, kcat_ref = refs
    scale = (MLA_NOPE + MLA_ROPE) ** -0.5
    s_new = kvn_ref.shape[0]
    off = PAST_LEN if lat else 0

    def build():
        lane = lax.broadcasted_iota(jnp.int32, (s_new, LANES), 1)
        kr_new = jnp.where(lane < MLA_ROPE, krn_ref[...], 0.0).astype(BF16)
        for h in range(MLA_HEADS):
            c0 = h * MLA_QH
            kcat_ref[h, off:off + s_new, 0:MLA_NOPE] = kvn_ref[:, c0:c0 + MLA_NOPE]
            kcat_ref[h, off:off + s_new, MLA_NOPE:MLA_QH] = kr_new
            if lat:
                kcat_ref[h, 0:off, 0:MLA_NOPE] = kvc_ref[:, c0:c0 + MLA_NOPE]
                kcat_ref[h, 0:off, MLA_NOPE:MLA_NOPE + MLA_ROPE] = krc_ref[...].astype(BF16)
                kcat_ref[h, 0:off, MLA_NOPE + MLA_ROPE:MLA_QH] = jnp.zeros((off, MLA_QH - MLA_NOPE - MLA_ROPE), BF16)
                vcat_ref[0:off, h * MLA_V:(h + 1) * MLA_V] = kvc_ref[:, c0 + MLA_NOPE:c0 + MLA_QH]
                vcat_ref[off:off + s_new, h * MLA_V:(h + 1) * MLA_V] = kvn_ref[:, c0 + MLA_NOPE:c0 + MLA_QH]

    if lat:
        pl.when(pl.program_id(1) == 0)(build)
    else:
        build()

    for h in range(MLA_HEADS):
        c0 = h * MLA_QH
        qn = q_ref[:, c0:c0 + MLA_NOPE].astype(F32)
        qr = q_ref[:, c0 + MLA_NOPE:c0 + MLA_QH].astype(F32)
        if lat:
            qr = _rope(qr, cos_ref[...], sa_ref[...], sb_ref[...], MLA_ROPE // 4)
        qh = (jnp.concatenate([qn, qr], axis=1) * scale).astype(BF16)
        s = _dot_nt(qh, kcat_ref[h])
        p = jnp.exp(s - jnp.max(s, axis=-1, keepdims=True))
        den = jnp.sum(p, axis=-1, keepdims=True)
        v = vcat_ref[:, h * MLA_V:(h + 1) * MLA_V] if lat else kvn_ref[:, c0 + MLA_NOPE:c0 + MLA_QH]
        o_ref[:, h * MLA_V:(h + 1) * MLA_V] = (_dot(p.astype(BF16), v) / den).astype(BF16)


def _attn_ctx(q, kv, kr, prev):
    t = SEQ
    return pl.pallas_call(
        functools.partial(_attn_kernel, lat=False),
        grid=(BATCH,),
        in_specs=[pl.BlockSpec((t, MLA_HEADS * MLA_QH), lambda b: (b, 0)),
                  pl.BlockSpec((t, MLA_HEADS * MLA_QH), lambda b: (b, 0)),
                  pl.BlockSpec((t, LANES), lambda b: (b, 0)),
                  pl.BlockSpec(memory_space=pl.ANY)],
        out_specs=pl.BlockSpec((t, MLA_HEADS * MLA_V), lambda b: (b, 0)),
        out_shape=jax.ShapeDtypeStruct(prev.shape, prev.dtype),
        input_output_aliases={3: 0},
        scratch_shapes=[pltpu.VMEM((MLA_HEADS, t, MLA_QH), BF16)],
        compiler_params=_params("arbitrary"),
        name="attn_ctx",
    )(q, kv, kr, prev)


def _attn_lat(q, kv, kr, kv_cache, cache_krope, l, tab, prev):
    tq = ROW_TILE
    nq = DEC_SEQ // tq
    blk0 = N_CTX // DEC_SEQ
    qblk0 = N_CTX // tq
    tspec = pl.BlockSpec((tq, LANES), lambda b, i: (i, 0))
    return pl.pallas_call(
        functools.partial(_attn_kernel, lat=True),
        grid=(DEC_BATCH, nq),
        in_specs=[pl.BlockSpec((tq, MLA_HEADS * MLA_QH), lambda b, i: (qblk0 + b * nq + i, 0)),
                  pl.BlockSpec((DEC_SEQ, MLA_HEADS * MLA_QH), lambda b, i: (blk0 + b, 0)),
                  pl.BlockSpec((DEC_SEQ, LANES), lambda b, i: (blk0 + b, 0)),
                  pl.BlockSpec((PAST_LEN, MLA_HEADS * MLA_QH), lambda b, i: (b * DEPTH + l, 0)),
                  pl.BlockSpec((None, None, PAST_LEN, MLA_ROPE), lambda b, i: (b, l, 0, 0)),
                  tspec, tspec, tspec, pl.BlockSpec(memory_space=pl.ANY)],
        out_specs=pl.BlockSpec((tq, MLA_HEADS * MLA_V), lambda b, i: (qblk0 + b * nq + i, 0)),
        out_shape=jax.ShapeDtypeStruct((N_TOK, MLA_HEADS * MLA_V), BF16),
        input_output_aliases={8: 0},
        scratch_shapes=[pltpu.VMEM((MLA_HEADS, PAST_LEN + DEC_SEQ, MLA_QH), BF16),
                        pltpu.VMEM((PAST_LEN + DEC_SEQ, MLA_HEADS * MLA_V), BF16)],
        compiler_params=_params("arbitrary", "arbitrary"),
        name="attn_lat",
    )(q, kv, kr, kv_cache, cache_krope, *tab, prev)


def _log_sigmoid(x):
    return -_softplus(-x)


def _ret_kernel(*refs, lat, T):
    if lat:
        q_ref, k_ref, v_ref, rg_ref, dec_ref, s0_ref, _, o_ref, decay_ref = refs
    else:
        q_ref, k_ref, v_ref, rg_ref, dec_ref = refs[:5]
        o_ref, st_ref, decay_ref = refs[-3:]
    tq = q_ref.shape[0]
    t0 = pl.program_id(0) * tq if lat else 0
    batch_id = pl.program_id(1) if lat else pl.program_id(0)
    lg = _log_sigmoid(dec_ref[...])
    tcol = (t0 + lax.broadcasted_iota(jnp.int32, (tq, 1), 0)).astype(F32)

    @pl.when(batch_id == 0)
    def _():
        rows = (t0 + lax.broadcasted_iota(jnp.int32, (tq, T), 0)).astype(F32)
        cols = lax.broadcasted_iota(jnp.int32, (tq, T), 1).astype(F32)
        diff = rows - cols
        fwd = diff >= 0.0
        dist = jnp.abs(diff)
        for h in range(RET_HEADS):
            one_sided = jnp.exp(jnp.where(fwd, lg[0:1, h:h + 1], lg[1:2, h:h + 1]) * dist)
            decay_ref[h] = jnp.where(dist == 0.0, 2.0, one_sided)

    for h in range(RET_HEADS):
        cs = slice(h * RET_DK, (h + 1) * RET_DK)
        lgf = lg[0:1, h:h + 1]
        lgb = lg[1:2, h:h + 1]
        q = q_ref[:, cs]
        k = k_ref[:, cs]
        v = v_ref[:, cs]
        sc = _dot_nt(q, k) * decay_ref[h]
        o = _dot(sc.astype(BF16), v)
        if lat:
            qf = q.astype(F32)
            o = o + _dot((qf * jnp.exp(lgf * (tcol + 1.0))).astype(BF16), s0_ref[0, h].astype(BF16))
            o = o + _dot((qf * jnp.exp(lgb * (T - tcol))).astype(BF16), s0_ref[1, h].astype(BF16))
        else:
            vf = v.astype(F32)
            st_ref[0, h] = _dot_tn(k, (vf * jnp.exp(lgf * (T - 1.0 - tcol))).astype(BF16))
            st_ref[1, h] = _dot_tn(k, (vf * jnp.exp(lgb * tcol)).astype(BF16))
        mu = jnp.mean(o, axis=-1, keepdims=True)
        oc = o - mu
        var = jnp.mean(oc * oc, axis=-1, keepdims=True)
        rg = rg_ref[:, cs].astype(F32)
        o_ref[:, cs] = (oc * lax.rsqrt(var + EPS) * (rg * _sigmoid(rg))).astype(BF16)


def _ret_ctx(qk, vg, decay, l, prev, st_prev):
    t = SEQ
    w = RET_HEADS * RET_DK
    return pl.pallas_call(
        functools.partial(_ret_kernel, lat=False, T=t),
        grid=(BATCH,),
        in_specs=[pl.BlockSpec((t, w), lambda b: (b, 0)),
                  pl.BlockSpec((t, w), lambda b: (b, 1)),
                  pl.BlockSpec((t, w), lambda b: (b, 0)),
                  pl.BlockSpec((t, w), lambda b: (b, 1)),
                  pl.BlockSpec((None, 2, RET_HEADS), lambda b: (l, 0, 0)),
                  pl.BlockSpec(memory_space=pl.ANY), pl.BlockSpec(memory_space=pl.ANY)],
        out_specs=[pl.BlockSpec((t, w), lambda b: (b, 0)),
                   pl.BlockSpec((None, None, 2, RET_HEADS, RET_DK, RET_DV), lambda b: (b, l, 0, 0, 0, 0))],
        out_shape=[jax.ShapeDtypeStruct(prev.shape, prev.dtype),
                   jax.ShapeDtypeStruct(st_prev.shape, st_prev.dtype)],
        input_output_aliases={5: 0, 6: 1},
        scratch_shapes=[pltpu.VMEM((RET_HEADS, t, t), F32)],
        compiler_params=_params("arbitrary"),
        name="ret_ctx",
    )(qk, qk, vg, vg, decay, prev, st_prev)


def _ret_lat(qk, vg, decay, state_ret, l, prev):
    tq = LAT_Q_TILE
    t = DEC_SEQ
    nq = t // tq
    w = RET_HEADS * RET_DK
    blk0 = N_CTX // t
    qblk0 = N_CTX // tq
    return pl.pallas_call(
        functools.partial(_ret_kernel, lat=True, T=t),
        grid=(nq, DEC_BATCH),
        in_specs=[pl.BlockSpec((tq, w), lambda i, b: (b * nq + i, 0)),
                  pl.BlockSpec((t, w), lambda i, b: (b, 1)),
                  pl.BlockSpec((t, w), lambda i, b: (blk0 + b, 0)),
                  pl.BlockSpec((tq, w), lambda i, b: (qblk0 + b * nq + i, 1)),
                  pl.BlockSpec((None, 2, RET_HEADS), lambda i, b: (l, 0, 0)),
                  pl.BlockSpec((None, None, 2, RET_HEADS, RET_DK, RET_DV), lambda i, b: (b, l, 0, 0, 0, 0)),
                  pl.BlockSpec(memory_space=pl.ANY)],
        out_specs=pl.BlockSpec((tq, w), lambda i, b: (qblk0 + b * nq + i, 0)),
        out_shape=jax.ShapeDtypeStruct((N_TOK, w), BF16),
        scratch_shapes=[pltpu.VMEM((RET_HEADS, tq, t), F32)],
        input_output_aliases={6: 0},
        compiler_params=_params("arbitrary", "arbitrary"),
        name="ret_lat",
    )(qk, qk, vg, vg, decay, state_ret, prev)


def _rope_tables(dim):
    half = dim // 2
    quarter = half // 2
    t = jnp.arange(DEC_SEQ)
    row = (t // GRID_W).astype(F32)
    col = (t % GRID_W).astype(F32)
    inv = ROPE_BASE ** (-jnp.arange(0, half, 2, dtype=F32) / half)
    zeros = jnp.zeros((DEC_SEQ, quarter), F32)
    cos_parts, sa_parts, sb_parts = [], [], []
    for pos in (row, col):
        ang = pos[:, None] * inv[None, :]
        c, s = jnp.cos(ang), jnp.sin(ang)
        cos_parts += [c, c]
        sa_parts += [zeros, s]
        sb_parts += [-s, zeros]
    padw = LANES - dim

    def build(parts):
        return jnp.concatenate(parts + ([jnp.zeros((DEC_SEQ, padw), F32)] if padw else []), axis=-1)

    return build(cos_parts), build(sa_parts), build(sb_parts)


def kernel(x_prompt, x_sample, c, cache_mla_ckv, cache_mla_krope, state_lru, state_ret, c_ctx, w_mod, b_mod,
           g_norm, w_in, lru_conv_w, lru_conv_b, lru_wa, lru_ba, lru_wx, lru_bx, lru_lam, mla_gq, mla_gkv,
           mla_wuq, mla_wukv, ret_decay, w_br_lru, w_br_mla, w_br_ret, w_out, w_ff1, w_ff2):
    x_ctx = x_prompt.reshape(N_CTX, D_MODEL)
    x_lat = x_sample.reshape(N_LAT, D_MODEL)
    cond8 = jnp.concatenate([c_ctx[None, :], c, jnp.zeros((8 - 1 - DEC_BATCH, D_MODEL), F32)], axis=0)
    mod = _ada(cond8, w_mod, b_mod).reshape(DEPTH * 8, 1, 6 * D_MODEL)
    g_all = g_norm.reshape(DEPTH * 4, 1, D_MODEL)

    w_in_t = jnp.swapaxes(w_in, 1, 2)
    wuq = jnp.pad(mla_wuq.reshape(DEPTH, MLA_Q_RANK, MLA_HEADS, MLA_NOPE + MLA_ROPE),
                  ((0, 0), (0, 0), (0, 0), (0, MLA_QH - MLA_NOPE - MLA_ROPE)))
    wuq = wuq.reshape(DEPTH, MLA_Q_RANK, MLA_HEADS * MLA_QH)
    lw = {"conv_w": lru_conv_w, "conv_b": lru_conv_b.reshape(DEPTH, 2, 1, D_RNN), "wa": lru_wa,
          "ba": lru_ba.reshape(DEPTH, 2, 1, D_RNN), "wx": lru_wx, "bx": lru_bx.reshape(DEPTH, 2, 1, D_RNN),
          "lam": lru_lam.reshape(DEPTH, 2, 1, D_RNN)}
    gq = mla_gq.reshape(DEPTH, 1, MLA_Q_RANK)
    gkv = mla_gkv.reshape(DEPTH, 1, MLA_KV_RANK)
    h0_lat = state_lru.reshape(DEC_BATCH, DEPTH, 2, 1, D_RNN)
    cache_ckv_rows = cache_mla_ckv.reshape(DEC_BATCH * DEPTH * PAST_LEN, MLA_KV_RANK)
    tab_mla = _rope_tables(MLA_ROPE)
    tab_ret = _rope_tables(RET_DK)
    wq = RET_HEADS * RET_DK

    caches = (jnp.zeros((BATCH, DEPTH, SEQ, MLA_KV_RANK), F32), jnp.zeros((BATCH, DEPTH, SEQ, MLA_ROPE), F32))
    st_lru = jnp.zeros((BATCH, DEPTH, 2, D_RNN), F32)
    st_ret = jnp.zeros((BATCH, DEPTH, 2, RET_HEADS, RET_DK, RET_DV), F32)
    y_lru = jnp.zeros((N_TOK, D_RNN), BF16)
    y_mla = jnp.zeros((N_TOK, MLA_HEADS * MLA_V), BF16)
    y_ret = jnp.zeros((N_TOK, RET_HEADS * RET_DV), BF16)

    kv_cache = _mm(cache_ckv_rows, mla_wukv, lambda i: i % DEPTH, out_dtype=BF16, tm=PAST_LEN)

    h, = _resnorm(x_ctx, x_lat, mod, g_all, nxt=(0, 0, 0))
    for l in range(DEPTH):
        proj_a = _mm(h, w_in_t, l, col0=0, ncols=2 * D_RNN + MLA_SLAB, w_t=True)
        kscale = (wq, RET_DK ** -0.5)
        qk_ctx = _mm(h, w_in_t, l, col0=COL_TAIL, ncols=2 * wq, out_dtype=BF16, m=N_CTX, kscale=kscale,
                     w_t=True)
        qk_lat = _mm(h, w_in_t, l, col0=COL_TAIL, ncols=2 * wq, out_dtype=BF16, m=N_LAT, tm=DEC_SEQ,
                     row_map=lambda i: N_CTX // DEC_SEQ + i, kscale=kscale,
                     rope=(tab_ret, RET_DK // 4), w_t=True)
        vg = _mm(h, w_in_t, l, col0=COL_TAIL + 2 * wq, ncols=2 * wq + 3 * D_MODEL, out_dtype=BF16,
                 act=("sigmoid_from", 2 * wq), w_t=True, tn=1024)

        y_lru, st_lru = _lru(proj_a, lw, l, T=SEQ, nb=BATCH, blk0=0, prev=y_lru, st_prev=st_lru)
        y_lru, = _lru(proj_a, lw, l, T=DEC_SEQ, nb=DEC_BATCH, blk0=N_CTX // DEC_SEQ, h0=h0_lat, prev=y_lru)

        q, kv, kr, *caches = _mla_front(proj_a, gq, gkv, wuq, mla_wukv, l, tab_mla, caches)
        y_mla = _attn_ctx(q, kv, kr, y_mla)
        y_mla = _attn_lat(q, kv, kr, kv_cache, cache_mla_krope, l, tab_mla, y_mla)

        y_ret, st_ret = _ret_ctx(qk_ctx, vg, ret_decay, l, y_ret, st_ret)
        y_ret = _ret_lat(qk_lat, vg, ret_decay, state_ret, l, y_ret)

        merged = _branch_merge((y_lru, y_mla, y_ret), (w_br_lru, w_br_mla, w_br_ret), vg, l, gate_col0=2 * wq)
        u = _mm(merged, w_out, l)
        x_ctx, x_lat, h2 = _resnorm(x_ctx, x_lat, mod, g_all, res=(u, l * 4 + 1, l, 2), nxt=(l * 4 + 2, l, 3))
        ff = _mm(h2, w_ff1, l, out_dtype=BF16, act="relu2", tn=1024)
        y = _mm(ff, w_ff2, l, tn=1024, tk=1024)
        if l + 1 < DEPTH:
            x_ctx, x_lat, h = _resnorm(x_ctx, x_lat, mod, g_all, res=(y, l * 4 + 3, l, 5),
                                       nxt=((l + 1) * 4, l + 1, 0))
        else:
            x_ctx, x_lat = _resnorm(x_ctx, x_lat, mod, g_all, res=(y, l * 4 + 3, l, 5))

    new_ckv, new_krope = caches
    return (x_ctx.reshape(BATCH, SEQ, D_MODEL), x_lat.reshape(DEC_BATCH, DEC_SEQ, D_MODEL),
            new_ckv, new_krope, st_lru, st_ret)
```

```python
import functools

import jax
import jax.numpy as jnp
from jax import lax
from jax.experimental import pallas as pl
from jax.experimental.pallas import tpu as pltpu

F32 = jnp.float32
BF16 = jnp.bfloat16

D_MODEL = 2048
BATCH = 16
SEQ = 256
DEPTH = 4
DEC_BATCH = 4
DEC_SEQ = 1024
PAST_LEN = 256
GRID_W = 64
EPS = 1e-6
ROPE_BASE = 10000.0
D_RNN = D_MODEL // 2
LRU_BLOCKS = 8
LRU_BS = D_RNN // LRU_BLOCKS
CONV_W = 4
LRU_C = 8.0
MLA_HEADS = 8
MLA_NOPE = 128
MLA_ROPE = 64
MLA_V = 128
MLA_Q_RANK = D_MODEL // 4
MLA_KV_RANK = D_MODEL // 8
RET_HEADS = 8
RET_DK = 128
RET_DV = 128
D_FF = 4 * D_MODEL

N_CTX = BATCH * SEQ
N_LAT = DEC_BATCH * DEC_SEQ
N_TOK = N_CTX + N_LAT
COL_TAIL = 2 * D_RNN + MLA_Q_RANK + MLA_KV_RANK + MLA_ROPE
MLA_QH = 256
MLA_SLAB = 1024
GELU_TANH_SCALE = 0.7978845608028654
GELU_TANH_CUBIC = 0.044715

LANES = 128
SUBLANES = 8
VMEM_LIMIT = 56 * 1024 * 1024
ROW_TILE = 256
RES_TILE = 512
LAT_Q_TILE = 512


def _params(*sem):
    return pltpu.CompilerParams(dimension_semantics=sem, vmem_limit_bytes=VMEM_LIMIT)


def _sigmoid(x):
    return 0.5 * (1.0 + jnp.tanh(0.5 * x))


def _softplus(x):
    return jnp.maximum(x, 0.0) + jnp.log(1.0 + jnp.exp(-jnp.abs(x)))


def _rms(x, g):
    return x * lax.rsqrt(jnp.mean(x * x, axis=-1, keepdims=True) + EPS) * g


def _dot(a, b):
    return jnp.dot(a, b, preferred_element_type=F32)


def _dot_nt(a, b):
    return lax.dot_general(a, b, (((1,), (1,)), ((), ())), preferred_element_type=F32)


def _dot_tn(a, b):
    return lax.dot_general(a, b, (((0,), (0,)), ((), ())), preferred_element_type=F32)


def _rope(x, cos, sa, sb, quarter):
    w = x.shape[-1]
    reps = w // cos.shape[-1]
    if reps > 1:
        cos, sa, sb = (jnp.tile(t, (1, reps)) for t in (cos, sa, sb))
    return x * cos + pltpu.roll(x, quarter, 1) * sa + pltpu.roll(x, w - quarter, 1) * sb


def _mod_row(row0):
    return jnp.where(row0 < N_CTX, 0, 1 + (row0 - N_CTX) // DEC_SEQ)


def _ada_kernel(c_ref, w_ref, b_ref, o_ref):
    c = c_ref[...]
    s = c * _sigmoid(c)
    o_ref[...] = _dot(s.astype(BF16), w_ref[...].astype(BF16)) + b_ref[...]


def _ada(cond8, w_mod, b_mod):
    tn = 1024
    n = w_mod.shape[-1]
    return pl.pallas_call(
        _ada_kernel,
        grid=(DEPTH, n // tn),
        in_specs=[pl.BlockSpec((8, D_MODEL), lambda l, j: (0, 0)),
                  pl.BlockSpec((None, D_MODEL, tn), lambda l, j: (l, 0, j)),
                  pl.BlockSpec((None, 1, tn), lambda l, j: (l, 0, j))],
        out_specs=pl.BlockSpec((None, 8, tn), lambda l, j: (l, 0, j)),
        out_shape=jax.ShapeDtypeStruct((DEPTH, 8, n), F32),
        compiler_params=_params("arbitrary", "arbitrary"),
        name="ada",
    )(cond8, w_mod, b_mod.reshape(DEPTH, 1, n))


def _resnorm_kernel(*refs, has_res, has_next):
    refs = list(refs)
    xc_ref, xl_ref = refs[:2]
    refs = refs[2:]
    is_ctx = pl.program_id(0) < N_CTX // RES_TILE
    x = jnp.where(is_ctx, xc_ref[...], xl_ref[...])
    if has_res:
        u_ref, gpost_ref, gate_ref = refs[:3]
        refs = refs[3:]
        x = x + gate_ref[...] * _rms(u_ref[...], gpost_ref[...])
    if has_next:
        gpre_ref, shift_ref, scale_ref = refs[:3]
        refs = refs[3:]
    if has_res:
        xco_ref, xlo_ref = refs[:2]
        refs = refs[2:]

        @pl.when(is_ctx)
        def _():
            xco_ref[...] = x

        @pl.when(jnp.logical_not(is_ctx))
        def _():
            xlo_ref[...] = x
    if has_next:
        ho_ref = refs.pop(0)
        h = _rms(x, gpre_ref[...]) * (1.0 + scale_ref[...]) + shift_ref[...]
        ho_ref[...] = h.astype(BF16)


def _resnorm(x_ctx, x_lat, mod, g_norm, *, res=None, nxt=None):
    tr = RES_TILE
    nctx = N_CTX // tr
    row = pl.BlockSpec((tr, D_MODEL), lambda i: (i, 0))
    crow = pl.BlockSpec((tr, D_MODEL), lambda i: (jnp.minimum(i, nctx - 1), 0))
    lrow = pl.BlockSpec((tr, D_MODEL), lambda i: (jnp.maximum(i - nctx, 0), 0))

    def gspec(k):
        return pl.BlockSpec((None, 1, D_MODEL), lambda i: (k, 0, 0))

    def mspec(l, chunk):
        return pl.BlockSpec((None, 1, D_MODEL), lambda i: (l * 8 + _mod_row(i * tr), 0, chunk))

    args, specs, outs, ospecs = [x_ctx, x_lat], [crow, lrow], [], []
    if res is not None:
        u, gk, l, chunk = res
        args += [u, g_norm, mod]
        specs += [row, gspec(gk), mspec(l, chunk)]
        outs += [jax.ShapeDtypeStruct((N_CTX, D_MODEL), F32), jax.ShapeDtypeStruct((N_LAT, D_MODEL), F32)]
        ospecs += [crow, lrow]
    if nxt is not None:
        gk, l, chunk = nxt
        args += [g_norm, mod, mod]
        specs += [gspec(gk), mspec(l, chunk), mspec(l, chunk + 1)]
        outs.append(jax.ShapeDtypeStruct((N_TOK, D_MODEL), BF16))
        ospecs.append(row)
    return pl.pallas_call(
        functools.partial(_resnorm_kernel, has_res=res is not None, has_next=nxt is not None),
        grid=(N_TOK // tr,),
        in_specs=specs, out_specs=ospecs, out_shape=outs,
        compiler_params=_params("arbitrary"),
        name="resnorm",
    )(*args)


def _mm_kernel(x_ref, w_ref, *rest, nk, act, kscale, rope_quarter, w_t):
    o_ref = rest[-1]
    if nk == 1:
        if w_t:
            part = _dot_nt(x_ref[...].astype(BF16), w_ref[0].astype(BF16))
        else:
            part = _dot(x_ref[...].astype(BF16), w_ref[...].astype(BF16))
        if act == "relu2":
            part = jnp.square(jnp.maximum(part, 0.0))
        elif act == "sigmoid":
            part = _sigmoid(part)
        elif act is not None:
            part = jnp.where(pl.program_id(1) >= act[1], _sigmoid(part), part)
        if kscale is not None:
            part = part * jnp.where(pl.program_id(1) >= kscale[0], kscale[1], 1.0)
        if rope_quarter is not None:
            cos_ref, sa_ref, sb_ref = rest[:3]
            part = _rope(part, cos_ref[...], sa_ref[...], sb_ref[...], rope_quarter)
        o_ref[...] = part.astype(o_ref.dtype)
        return

    @pl.when(pl.program_id(2) == 0)
    def _():
        o_ref[...] = jnp.zeros(o_ref.shape, o_ref.dtype)

    o_ref[...] += _dot(x_ref[...].astype(BF16), w_ref[...].astype(BF16))


def _mm(x, w, l, *, col0=0, ncols=None, out_dtype=F32, act=None, tm=2048, tn=512, tk=None,
        m=None, row_map=None, kscale=None, rope=None, w_t=False):
    kdim = w.shape[2] if w_t else w.shape[1]
    ncols = (w.shape[1] if w_t else w.shape[2]) - col0 if ncols is None else ncols
    m = x.shape[0] if m is None else m
    tk = kdim if tk is None else tk
    tm, tn = min(tm, m), min(tn, ncols)
    assert m % tm == 0 and ncols % tn == 0 and kdim % tk == 0
    nk = kdim // tk
    assert nk == 1 or (act is None and out_dtype == F32 and kscale is None and rope is None and not w_t)
    row_map = (lambda i: i) if row_map is None else row_map
    args = [x, w]
    if w_t:
        assert col0 % SUBLANES == 0 and tn % SUBLANES == 0
        wspec = pl.BlockSpec((pl.Element(1), pl.Element(tn), pl.Element(tk)),
                             lambda i, j, k: (l, pl.multiple_of(col0 + j * tn, SUBLANES), 0))
    else:
        assert col0 % tn == 0
        jb = col0 // tn
        layer = l if callable(l) else (lambda i: l)
        wspec = pl.BlockSpec((None, tk, tn), lambda i, j, k: (layer(i), k, jb + j))
    in_specs = [pl.BlockSpec((tm, tk), lambda i, j, k: (row_map(i), k)), wspec]
    if kscale is not None:
        assert kscale[0] % tn == 0
        kscale = (kscale[0] // tn, kscale[1])
    if isinstance(act, tuple):
        assert act[1] % tn == 0
        act = (act[0], act[1] // tn)
    if rope is not None:
        tabs, quarter = rope
        assert tm == DEC_SEQ and tabs[0].shape == (DEC_SEQ, LANES)
        args += list(tabs)
        in_specs += [pl.BlockSpec((tm, LANES), lambda i, j, k: (0, 0))] * 3
    return pl.pallas_call(
        functools.partial(_mm_kernel, nk=nk, act=act, kscale=kscale,
                          rope_quarter=None if rope is None else rope[1], w_t=w_t),
        grid=(m // tm, ncols // tn, nk),
        in_specs=in_specs,
        out_specs=pl.BlockSpec((tm, tn), lambda i, j, k: (i, j)),
        out_shape=jax.ShapeDtypeStruct((m, ncols), out_dtype),
        compiler_params=_params("arbitrary", "arbitrary", "arbitrary"),
        name="mm",
    )(*args)


def _bm_kernel(a0_ref, a1_ref, a2_ref, w0_ref, w1_ref, w2_ref, g0_ref, g1_ref, g2_ref, o_ref):
    acc = g0_ref[...].astype(F32) * _dot(a0_ref[...], w0_ref[...].astype(BF16))
    acc = acc + g1_ref[...].astype(F32) * _dot(a1_ref[...], w1_ref[...].astype(BF16))
    acc = acc + g2_ref[...].astype(F32) * _dot(a2_ref[...], w2_ref[...].astype(BF16))
    o_ref[...] = acc.astype(BF16)


def _branch_merge(acts, ws, gates, l, *, gate_col0=0, tm=2048, tn=256):
    nj = D_MODEL // tn
    kdim = acts[0].shape[1]
    assert gate_col0 % tn == 0
    g0 = gate_col0 // tn
    aspec = pl.BlockSpec((tm, kdim), lambda i, j: (i, 0))
    wspec = pl.BlockSpec((None, kdim, tn), lambda i, j: (l, 0, j))
    gspecs = [pl.BlockSpec((tm, tn), lambda i, j, b=b: (i, g0 + b * nj + j)) for b in range(3)]
    return pl.pallas_call(
        _bm_kernel,
        grid=(N_TOK // tm, nj),
        in_specs=[aspec] * 3 + [wspec] * 3 + gspecs,
        out_specs=pl.BlockSpec((tm, tn), lambda i, j: (i, j)),
        out_shape=jax.ShapeDtypeStruct((N_TOK, D_MODEL), BF16),
        compiler_params=_params("arbitrary", "arbitrary"),
        name="branch_merge",
    )(*acts, *ws, gates, gates, gates)


LRU_CT = 512


def _lru_kernel(*refs, T, has_h0, has_state):
    xr_ref, xg_ref, cw_ref, cb_ref, wa_ref, ba_ref, wx_ref, bx_ref, lam_ref = refs[:9]
    refs = refs[9:]
    if has_h0:
        h0_ref = refs[0]
        refs = refs[1:]
    if has_state:
        y_ref, st_ref = refs[2:4]
        refs = refs[4:]
    else:
        y_ref = refs[1]
        refs = refs[2:]
    x3_ref, p_ref, a_ref, b_ref, hl_ref, pc_ref, hn_ref = refs
    nblk = xr_ref.shape[1] // LRU_BS
    S = SUBLANES
    G = T // S
    H = CONV_W - 1

    pitch = G + S
    for n in range(nblk):
        for k in range(S):
            x3_ref[n, k * pitch:k * pitch + G, :] = xr_ref[k * G:(k + 1) * G, n * LRU_BS:(n + 1) * LRU_BS]

    def gather(p, _):
        r0 = pl.multiple_of((p + H) * S, S)
        for n in range(nblk):
            p_ref.at[n][pl.ds(r0, S), :] = x3_ref.at[n][pl.ds(p, S, stride=pitch), :]
        return 0

    lax.fori_loop(0, G, gather, 0, unroll=4)
    row = lax.broadcasted_iota(jnp.int32, (S, LRU_BS), 0)
    for n in range(nblk):
        for i in range(H):
            tail = p_ref[n, (G + i) * S:(G + i + 1) * S, :]
            p_ref[n, i * S:(i + 1) * S, :] = jnp.where(row == 0, 0.0, pltpu.roll(tail, 1, 0))
            head = p_ref[n, (H + i) * S:(H + i + 1) * S, :]
            p_ref[n, (H + G + i) * S:(H + G + i + 1) * S, :] = jnp.where(row == S - 1, 0.0,
                                                                        pltpu.roll(head, S - 1, 0))

    cmats = []
    for d in range(2):
        for n in range(nblk):
            cs = slice(n * LRU_BS, (n + 1) * LRU_BS)
            xc = jnp.zeros((T, LRU_BS), F32) + cb_ref[d, :, cs]
            for j in range(CONV_W):
                off = j if d == 0 else 2 * H - j
                xc = xc + cw_ref[d, j:j + 1, cs] * p_ref[n, off * S:off * S + T, :]
            xcb = xc.astype(BF16)
            t_r = jnp.tanh(_dot(xcb, (0.5 * wa_ref[d, n]).astype(BF16)) + 0.5 * ba_ref[d, :, cs])
            t_i = jnp.tanh(_dot(xcb, (0.5 * wx_ref[d, n]).astype(BF16)) + 0.5 * bx_ref[d, :, cs])
            half_c = (-0.5 * LRU_C) * _softplus(-lam_ref[d, :, cs])
            a = jnp.exp(half_c + half_c * t_r)
            a_ref[n] = a
            b_ref[n] = (0.5 * jnp.sqrt(1.0 - a * a)) * (xc + t_i * xc)

        def body(it, carry, d=d):
            p = it if d == 0 else G - 1 - it
            r0 = pl.multiple_of(p * S, S)
            out = []
            for n in range(nblk):
                h, pc = carry[n]
                av = a_ref.at[n][pl.ds(r0, S), :]
                h = av * h + b_ref.at[n][pl.ds(r0, S), :]
                pc = av * pc
                hl_ref.at[d, n][pl.ds(r0, S), :] = h
                pc_ref.at[d, n][pl.ds(r0, S), :] = pc
                out.append((h, pc))
            return tuple(out)

        init = tuple((jnp.zeros((S, LRU_BS), F32), jnp.ones((S, LRU_BS), F32)) for _ in range(nblk))
        ends = lax.fori_loop(0, G, body, init, unroll=4)

        last_d, cmat_d = [], []
        for n in range(nblk):
            cs = slice(n * LRU_BS, (n + 1) * LRU_BS)
            h_end, pc_end = ends[n]
            c = h0_ref[d, :, cs] if has_h0 else jnp.zeros((1, LRU_BS), F32)
            rows = [None] * S
            for k in (range(S) if d == 0 else range(S - 1, -1, -1)):
                rows[k] = c
                c = h_end[k:k + 1, :] + pc_end[k:k + 1, :] * c
            last_d.append(c)
            cmat_d.append(jnp.concatenate(rows, axis=0))
        cmats.append(cmat_d)
        if has_state:
            for n in range(nblk):
                st_ref[d:d + 1, n * LRU_BS:(n + 1) * LRU_BS] = last_d[n]

    def scatter(p, _):
        r0 = pl.multiple_of(p * S, S)
        for n in range(nblk):
            h = (hl_ref.at[0, n][pl.ds(r0, S), :] + pc_ref.at[0, n][pl.ds(r0, S), :] * cmats[0][n]
                 + hl_ref.at[1, n][pl.ds(r0, S), :] + pc_ref.at[1, n][pl.ds(r0, S), :] * cmats[1][n])
            hn_ref.at[n][pl.ds(p, S, stride=pitch), :] = h
        return 0

    lax.fori_loop(0, G, scatter, 0, unroll=4)

    for n in range(nblk):
        cs = slice(n * LRU_BS, (n + 1) * LRU_BS)
        for k in range(S):
            xg = xg_ref[k * G:(k + 1) * G, cs]
            gelu = 0.5 * xg * (1.0 + jnp.tanh(GELU_TANH_SCALE * (xg + GELU_TANH_CUBIC * (xg * xg * xg))))
            y_ref[k * G:(k + 1) * G, cs] = (hn_ref[n, k * pitch:k * pitch + G, :] * gelu).astype(BF16)


def _lru(proj_a, lw, l, *, T, nb, blk0, prev, h0=None, st_prev=None):
    ct = LRU_CT
    ncb = D_RNN // ct
    bpc = ct // LRU_BS
    in_specs = [
        pl.BlockSpec((T, ct), lambda b, c: (blk0 + b, c)),
        pl.BlockSpec((T, ct), lambda b, c: (blk0 + b, ncb + c)),
        pl.BlockSpec((None, 2, CONV_W, ct), lambda b, c: (l, 0, 0, c)),
        pl.BlockSpec((None, 2, 1, ct), lambda b, c: (l, 0, 0, c)),
        pl.BlockSpec((None, 2, bpc, LRU_BS, LRU_BS), lambda b, c: (l, 0, c, 0, 0)),
        pl.BlockSpec((None, 2, 1, ct), lambda b, c: (l, 0, 0, c)),
        pl.BlockSpec((None, 2, bpc, LRU_BS, LRU_BS), lambda b, c: (l, 0, c, 0, 0)),
        pl.BlockSpec((None, 2, 1, ct), lambda b, c: (l, 0, 0, c)),
        pl.BlockSpec((None, 2, 1, ct), lambda b, c: (l, 0, 0, c)),
    ]
    args = [proj_a, proj_a, lw["conv_w"], lw["conv_b"], lw["wa"], lw["ba"], lw["wx"], lw["bx"], lw["lam"]]
    if h0 is not None:
        in_specs.append(pl.BlockSpec((None, None, 2, 1, ct), lambda b, c: (b, l, 0, 0, c)))
        args.append(h0)
    has_state = st_prev is not None
    aliases = {len(args): 0}
    in_specs.append(pl.BlockSpec(memory_space=pl.ANY))
    args.append(prev)
    out_specs = [pl.BlockSpec((T, ct), lambda b, c: (blk0 + b, c))]
    out_shape = [jax.ShapeDtypeStruct(prev.shape, prev.dtype)]
    if has_state:
        aliases[len(args)] = 1
        in_specs.append(pl.BlockSpec(memory_space=pl.ANY))
        args.append(st_prev)
        out_specs.append(pl.BlockSpec((None, None, 2, ct), lambda b, c: (b, l, 0, c)))
        out_shape.append(jax.ShapeDtypeStruct(st_prev.shape, st_prev.dtype))
    return pl.pallas_call(
        functools.partial(_lru_kernel, T=T, has_h0=h0 is not None, has_state=has_state),
        grid=(nb, ncb),
        in_specs=in_specs,
        out_specs=out_specs,
        out_shape=out_shape,
        input_output_aliases=aliases,
        scratch_shapes=[pltpu.VMEM((bpc, T + SUBLANES * SUBLANES, LRU_BS), F32),
                        pltpu.VMEM((bpc, T + 2 * (CONV_W - 1) * SUBLANES, LRU_BS), F32),
                        pltpu.VMEM((bpc, T, LRU_BS), F32), pltpu.VMEM((bpc, T, LRU_BS), F32),
                        pltpu.VMEM((2, bpc, T, LRU_BS), F32), pltpu.VMEM((2, bpc, T, LRU_BS), F32),
                        pltpu.VMEM((bpc, T + SUBLANES * SUBLANES, LRU_BS), F32)],
        compiler_params=_params("arbitrary", "arbitrary"),
        name="lru",
    )(*args)


MLA_TILE = DEC_SEQ
MLA_TILE_SEQS = MLA_TILE // SEQ


def _mla_front_kernel(p_ref, gq_ref, gkv_ref, wuq_ref, wukv_ref, cos_ref, sa_ref, sb_ref, *rest):
    q_ref, kv_ref, kr_ref, ckv_out_ref, kr_out_ref = rest[-5:]
    i = pl.program_id(0)
    cq = _rms(p_ref[:, 0:MLA_Q_RANK], gq_ref[...]).astype(BF16)
    q_ref[...] = _dot(cq, wuq_ref[...].astype(BF16)).astype(BF16)
    ckv = _rms(p_ref[:, MLA_Q_RANK:MLA_Q_RANK + MLA_KV_RANK], gkv_ref[...])
    kv_ref[...] = _dot(ckv.astype(BF16), wukv_ref[...].astype(BF16)).astype(BF16)
    k0 = MLA_Q_RANK + MLA_KV_RANK
    kr = p_ref[:, k0:k0 + LANES]

    @pl.when(i < N_CTX // MLA_TILE)
    def _():
        kr_ref[...] = kr
        for s in range(MLA_TILE_SEQS):
            ckv_out_ref[s] = ckv[s * SEQ:(s + 1) * SEQ, :]
            kr_out_ref[s] = kr[s * SEQ:(s + 1) * SEQ, 0:MLA_ROPE]

    @pl.when(i >= N_CTX // MLA_TILE)
    def _():
        kr_ref[...] = _rope(kr, cos_ref[...], sa_ref[...], sb_ref[...], MLA_ROPE // 4)


def _mla_front(proj_a, gq, gkv, wuq, wukv, l, tab, caches):
    tr = MLA_TILE
    last_ctx = N_CTX // tr - 1
    tspec = pl.BlockSpec((tr, LANES), lambda i: (0, 0))
    kvw = MLA_HEADS * MLA_QH
    args = [proj_a, gq, gkv, wuq, wukv, *tab]
    in_specs = [pl.BlockSpec((tr, MLA_SLAB), lambda i: (i, 2 * D_RNN // MLA_SLAB)),
                pl.BlockSpec((None, 1, MLA_Q_RANK), lambda i: (l, 0, 0)),
                pl.BlockSpec((None, 1, MLA_KV_RANK), lambda i: (l, 0, 0)),
                pl.BlockSpec((None, MLA_Q_RANK, kvw), lambda i: (l, 0, 0)),
                pl.BlockSpec((None, MLA_KV_RANK, kvw), lambda i: (l, 0, 0)),
                tspec, tspec, tspec]
    aliases = {len(args): 3, len(args) + 1: 4}
    args += list(caches)
    in_specs += [pl.BlockSpec(memory_space=pl.ANY)] * 2
    return pl.pallas_call(
        _mla_front_kernel,
        grid=(N_TOK // tr,),
        in_specs=in_specs,
        out_specs=[pl.BlockSpec((tr, kvw), lambda i: (i, 0)),
                   pl.BlockSpec((tr, kvw), lambda i: (i, 0)),
                   pl.BlockSpec((tr, LANES), lambda i: (i, 0)),
                   pl.BlockSpec((MLA_TILE_SEQS, None, SEQ, MLA_KV_RANK),
                                lambda i: (jnp.minimum(i, last_ctx), l, 0, 0)),
                   pl.BlockSpec((MLA_TILE_SEQS, None, SEQ, MLA_ROPE),
                                lambda i: (jnp.minimum(i, last_ctx), l, 0, 0))],
        out_shape=[jax.ShapeDtypeStruct((N_TOK, kvw), BF16),
                   jax.ShapeDtypeStruct((N_TOK, kvw), BF16),
                   jax.ShapeDtypeStruct((N_TOK, LANES), F32),
                   jax.ShapeDtypeStruct((BATCH, DEPTH, SEQ, MLA_KV_RANK), F32),
                   jax.ShapeDtypeStruct((BATCH, DEPTH, SEQ, MLA_ROPE), F32)],
        input_output_aliases=aliases,
        compiler_params=_params("arbitrary"),
        name="mla_front",
    )(*args)


def _attn_kernel(*refs, lat):
    if lat:
        (q_ref, kvn_ref, krn_ref, kvc_ref, krc_ref, cos_ref, sa_ref, sb_ref, _,
         o_ref, kcat_ref, vcat_ref) = refs
    else:
        q_ref, kvn_ref, krn_ref, _, o_ref, kcat_ref = refs
    scale = (MLA_NOPE + MLA_ROPE) ** -0.5
    s_new = kvn_ref.shape[0]
    off = PAST_LEN if lat else 0

    def build():
        lane = lax.broadcasted_iota(jnp.int32, (s_new, LANES), 1)
        kr_new = jnp.where(lane < MLA_ROPE, krn_ref[...], 0.0).astype(BF16)
        for h in range(MLA_HEADS):
            c0 = h * MLA_QH
            kcat_ref[h, off:off + s_new, 0:MLA_NOPE] = kvn_ref[:, c0:c0 + MLA_NOPE]
            kcat_ref[h, off:off + s_new, MLA_NOPE:MLA_QH] = kr_new
            if lat:
                kcat_ref[h, 0:off, 0:MLA_NOPE] = kvc_ref[:, c0:c0 + MLA_NOPE]
                kcat_ref[h, 0:off, MLA_NOPE:MLA_NOPE + MLA_ROPE] = krc_ref[...].astype(BF16)
                kcat_ref[h, 0:off, MLA_NOPE + MLA_ROPE:MLA_QH] = jnp.zeros((off, MLA_QH - MLA_NOPE - MLA_ROPE), BF16)
                vcat_ref[0:off, h * MLA_V:(h + 1) * MLA_V] = kvc_ref[:, c0 + MLA_NOPE:c0 + MLA_QH]
                vcat_ref[off:off + s_new, h * MLA_V:(h + 1) * MLA_V] = kvn_ref[:, c0 + MLA_NOPE:c0 + MLA_QH]

    if lat:
        pl.when(pl.program_id(1) == 0)(build)
    else:
        build()

    def scores(h):
        c0 = h * MLA_QH
        qn = q_ref[:, c0:c0 + MLA_NOPE].astype(F32)
        qr = q_ref[:, c0 + MLA_NOPE:c0 + MLA_QH].astype(F32)
        if lat:
            qr = _rope(qr, cos_ref[...], sa_ref[...], sb_ref[...], MLA_ROPE // 4)
        qh = (jnp.concatenate([qn, qr], axis=1) * scale).astype(BF16)
        return _dot_nt(qh, kcat_ref[h])

    def attend(h, s):
        p = jnp.exp(s - jnp.max(s, axis=-1, keepdims=True))
        den = jnp.sum(p, axis=-1, keepdims=True)
        v = (vcat_ref[:, h * MLA_V:(h + 1) * MLA_V] if lat
             else kvn_ref[:, h * MLA_QH + MLA_NOPE:(h + 1) * MLA_QH])
        o_ref[:, h * MLA_V:(h + 1) * MLA_V] = (_dot(p.astype(BF16), v) / den).astype(BF16)

    if lat:
        for h in range(MLA_HEADS):
            attend(h, scores(h))
    else:
        all_scores = [scores(h) for h in range(MLA_HEADS)]
        for h in range(MLA_HEADS):
            attend(h, all_scores[h])


def _attn_ctx(q, kv, kr, prev):
    t = SEQ
    return pl.pallas_call(
        functools.partial(_attn_kernel, lat=False),
        grid=(BATCH,),
        in_specs=[pl.BlockSpec((t, MLA_HEADS * MLA_QH), lambda b: (b, 0)),
                  pl.BlockSpec((t, MLA_HEADS * MLA_QH), lambda b: (b, 0)),
                  pl.BlockSpec((t, LANES), lambda b: (b, 0)),
                  pl.BlockSpec(memory_space=pl.ANY)],
        out_specs=pl.BlockSpec((t, MLA_HEADS * MLA_V), lambda b: (b, 0)),
        out_shape=jax.ShapeDtypeStruct(prev.shape, prev.dtype),
        input_output_aliases={3: 0},
        scratch_shapes=[pltpu.VMEM((MLA_HEADS, t, MLA_QH), BF16)],
        compiler_params=_params("arbitrary"),
        name="attn_ctx",
    )(q, kv, kr, prev)


def _attn_lat(q, kv, kr, kv_cache, cache_krope, l, tab, prev):
    tq = ROW_TILE
    nq = DEC_SEQ // tq
    blk0 = N_CTX // DEC_SEQ
    qblk0 = N_CTX // tq
    tspec = pl.BlockSpec((tq, LANES), lambda b, i: (i, 0))
    return pl.pallas_call(
        functools.partial(_attn_kernel, lat=True),
        grid=(DEC_BATCH, nq),
        in_specs=[pl.BlockSpec((tq, MLA_HEADS * MLA_QH), lambda b, i: (qblk0 + b * nq + i, 0)),
                  pl.BlockSpec((DEC_SEQ, MLA_HEADS * MLA_QH), lambda b, i: (blk0 + b, 0)),
                  pl.BlockSpec((DEC_SEQ, LANES), lambda b, i: (blk0 + b, 0)),
                  pl.BlockSpec((PAST_LEN, MLA_HEADS * MLA_QH), lambda b, i: (b * DEPTH + l, 0)),
                  pl.BlockSpec((None, None, PAST_LEN, MLA_ROPE), lambda b, i: (b, l, 0, 0)),
                  tspec, tspec, tspec, pl.BlockSpec(memory_space=pl.ANY)],
        out_specs=pl.BlockSpec((tq, MLA_HEADS * MLA_V), lambda b, i: (qblk0 + b * nq + i, 0)),
        out_shape=jax.ShapeDtypeStruct((N_TOK, MLA_HEADS * MLA_V), BF16),
        input_output_aliases={8: 0},
        scratch_shapes=[pltpu.VMEM((MLA_HEADS, PAST_LEN + DEC_SEQ, MLA_QH), BF16),
                        pltpu.VMEM((PAST_LEN + DEC_SEQ, MLA_HEADS * MLA_V), BF16)],
        compiler_params=_params("arbitrary", "arbitrary"),
        name="attn_lat",
    )(q, kv, kr, kv_cache, cache_krope, *tab, prev)


def _log_sigmoid(x):
    return -_softplus(-x)


def _ret_kernel(*refs, lat, T):
    if lat:
        q_ref, k_ref, v_ref, rg_ref, dec_ref, s0_ref, _, o_ref, decay_ref = refs
    else:
        q_ref, k_ref, v_ref, rg_ref, dec_ref = refs[:5]
        o_ref, st_ref, decay_ref = refs[-3:]
    tq = q_ref.shape[0]
    t0 = pl.program_id(0) * tq if lat else 0
    batch_id = pl.program_id(1) if lat else pl.program_id(0)
    lg = _log_sigmoid(dec_ref[...])
    tcol = (t0 + lax.broadcasted_iota(jnp.int32, (tq, 1), 0)).astype(F32)

    @pl.when(batch_id == 0)
    def _():
        rows = (t0 + lax.broadcasted_iota(jnp.int32, (tq, T), 0)).astype(F32)
        cols = lax.broadcasted_iota(jnp.int32, (tq, T), 1).astype(F32)
        diff = rows - cols
        fwd = diff >= 0.0
        dist = jnp.abs(diff)
        for h in range(RET_HEADS):
            one_sided = jnp.exp(jnp.where(fwd, lg[0:1, h:h + 1], lg[1:2, h:h + 1]) * dist)
            decay_ref[h] = jnp.where(dist == 0.0, 2.0, one_sided)

    heads = [slice(h * RET_DK, (h + 1) * RET_DK) for h in range(RET_HEADS)]
    scores = [_dot_nt(q_ref[:, cs], k_ref[:, cs]) for cs in heads]
    probs = [(scores[h] * decay_ref[h]).astype(BF16) for h in range(RET_HEADS)]
    outs = [_dot(probs[h], v_ref[:, cs]) for h, cs in enumerate(heads)]
    for h, cs in enumerate(heads):
        lgf = lg[0:1, h:h + 1]
        lgb = lg[1:2, h:h + 1]
        if lat:
            qf = q_ref[:, cs].astype(F32)
            outs[h] = outs[h] + _dot((qf * jnp.exp(lgf * (tcol + 1.0))).astype(BF16), s0_ref[0, h].astype(BF16))
            outs[h] = outs[h] + _dot((qf * jnp.exp(lgb * (T - tcol))).astype(BF16), s0_ref[1, h].astype(BF16))
        else:
            vf = v_ref[:, cs].astype(F32)
            k = k_ref[:, cs]
            st_ref[0, h] = _dot_tn(k, (vf * jnp.exp(lgf * (T - 1.0 - tcol))).astype(BF16))
            st_ref[1, h] = _dot_tn(k, (vf * jnp.exp(lgb * tcol)).astype(BF16))
    for h, cs in enumerate(heads):
        o = outs[h]
        mu = jnp.mean(o, axis=-1, keepdims=True)
        oc = o - mu
        var = jnp.mean(oc * oc, axis=-1, keepdims=True)
        rg = rg_ref[:, cs].astype(F32)
        o_ref[:, cs] = (oc * lax.rsqrt(var + EPS) * (rg * _sigmoid(rg))).astype(BF16)


def _ret_ctx(qk, vg, decay, l, prev, st_prev):
    t = SEQ
    w = RET_HEADS * RET_DK
    return pl.pallas_call(
        functools.partial(_ret_kernel, lat=False, T=t),
        grid=(BATCH,),
        in_specs=[pl.BlockSpec((t, w), lambda b: (b, 0)),
                  pl.BlockSpec((t, w), lambda b: (b, 1)),
                  pl.BlockSpec((t, w), lambda b: (b, 0)),
                  pl.BlockSpec((t, w), lambda b: (b, 1)),
                  pl.BlockSpec((None, 2, RET_HEADS), lambda b: (l, 0, 0)),
                  pl.BlockSpec(memory_space=pl.ANY), pl.BlockSpec(memory_space=pl.ANY)],
        out_specs=[pl.BlockSpec((t, w), lambda b: (b, 0)),
                   pl.BlockSpec((None, None, 2, RET_HEADS, RET_DK, RET_DV), lambda b: (b, l, 0, 0, 0, 0))],
        out_shape=[jax.ShapeDtypeStruct(prev.shape, prev.dtype),
                   jax.ShapeDtypeStruct(st_prev.shape, st_prev.dtype)],
        input_output_aliases={5: 0, 6: 1},
        scratch_shapes=[pltpu.VMEM((RET_HEADS, t, t), F32)],
        compiler_params=_params("arbitrary"),
        name="ret_ctx",
    )(qk, qk, vg, vg, decay, prev, st_prev)


def _ret_lat(qk, vg, decay, state_ret, l, prev):
    tq = LAT_Q_TILE
    t = DEC_SEQ
    nq = t // tq
    w = RET_HEADS * RET_DK
    blk0 = N_CTX // t
    qblk0 = N_CTX // tq
    return pl.pallas_call(
        functools.partial(_ret_kernel, lat=True, T=t),
        grid=(nq, DEC_BATCH),
        in_specs=[pl.BlockSpec((tq, w), lambda i, b: (b * nq + i, 0)),
                  pl.BlockSpec((t, w), lambda i, b: (b, 1)),
                  pl.BlockSpec((t, w), lambda i, b: (blk0 + b, 0)),
                  pl.BlockSpec((tq, w), lambda i, b: (qblk0 + b * nq + i, 1)),
                  pl.BlockSpec((None, 2, RET_HEADS), lambda i, b: (l, 0, 0)),
                  pl.BlockSpec((None, None, 2, RET_HEADS, RET_DK, RET_DV), lambda i, b: (b, l, 0, 0, 0, 0)),
                  pl.BlockSpec(memory_space=pl.ANY)],
        out_specs=pl.BlockSpec((tq, w), lambda i, b: (qblk0 + b * nq + i, 0)),
        out_shape=jax.ShapeDtypeStruct((N_TOK, w), BF16),
        scratch_shapes=[pltpu.VMEM((RET_HEADS, tq, t), F32)],
        input_output_aliases={6: 0},
        compiler_params=_params("arbitrary", "arbitrary"),
        name="ret_lat",
    )(qk, qk, vg, vg, decay, state_ret, prev)


def _rope_tables(dim):
    half = dim // 2
    quarter = half // 2
    t = jnp.arange(DEC_SEQ)
    row = (t // GRID_W).astype(F32)
    col = (t % GRID_W).astype(F32)
    inv = ROPE_BASE ** (-jnp.arange(0, half, 2, dtype=F32) / half)
    zeros = jnp.zeros((DEC_SEQ, quarter), F32)
    cos_parts, sa_parts, sb_parts = [], [], []
    for pos in (row, col):
        ang = pos[:, None] * inv[None, :]
        c, s = jnp.cos(ang), jnp.sin(ang)
        cos_parts += [c, c]
        sa_parts += [zeros, s]
        sb_parts += [-s, zeros]
    padw = LANES - dim

    def build(parts):
        return jnp.concatenate(parts + ([jnp.zeros((DEC_SEQ, padw), F32)] if padw else []), axis=-1)

    return build(cos_parts), build(sa_parts), build(sb_parts)


def kernel(x_prompt, x_sample, c, cache_mla_ckv, cache_mla_krope, state_lru, state_ret, c_ctx, w_mod, b_mod,
           g_norm, w_in, lru_conv_w, lru_conv_b, lru_wa, lru_ba, lru_wx, lru_bx, lru_lam, mla_gq, mla_gkv,
           mla_wuq, mla_wukv, ret_decay, w_br_lru, w_br_mla, w_br_ret, w_out, w_ff1, w_ff2):
    x_ctx = x_prompt.reshape(N_CTX, D_MODEL)
    x_lat = x_sample.reshape(N_LAT, D_MODEL)
    cond8 = jnp.concatenate([c_ctx[None, :], c, jnp.zeros((8 - 1 - DEC_BATCH, D_MODEL), F32)], axis=0)
    mod = _ada(cond8, w_mod, b_mod).reshape(DEPTH * 8, 1, 6 * D_MODEL)
    g_all = g_norm.reshape(DEPTH * 4, 1, D_MODEL)

    w_in_t = jnp.swapaxes(w_in, 1, 2)
    wuq = jnp.pad(mla_wuq.reshape(DEPTH, MLA_Q_RANK, MLA_HEADS, MLA_NOPE + MLA_ROPE),
                  ((0, 0), (0, 0), (0, 0), (0, MLA_QH - MLA_NOPE - MLA_ROPE)))
    wuq = wuq.reshape(DEPTH, MLA_Q_RANK, MLA_HEADS * MLA_QH)
    lw = {"conv_w": lru_conv_w, "conv_b": lru_conv_b.reshape(DEPTH, 2, 1, D_RNN), "wa": lru_wa,
          "ba": lru_ba.reshape(DEPTH, 2, 1, D_RNN), "wx": lru_wx, "bx": lru_bx.reshape(DEPTH, 2, 1, D_RNN),
          "lam": lru_lam.reshape(DEPTH, 2, 1, D_RNN)}
    gq = mla_gq.reshape(DEPTH, 1, MLA_Q_RANK)
    gkv = mla_gkv.reshape(DEPTH, 1, MLA_KV_RANK)
    h0_lat = state_lru.reshape(DEC_BATCH, DEPTH, 2, 1, D_RNN)
    cache_ckv_rows = cache_mla_ckv.reshape(DEC_BATCH * DEPTH * PAST_LEN, MLA_KV_RANK)
    tab_mla = _rope_tables(MLA_ROPE)
    tab_ret = _rope_tables(RET_DK)
    wq = RET_HEADS * RET_DK

    caches = (jnp.zeros((BATCH, DEPTH, SEQ, MLA_KV_RANK), F32), jnp.zeros((BATCH, DEPTH, SEQ, MLA_ROPE), F32))
    st_lru = jnp.zeros((BATCH, DEPTH, 2, D_RNN), F32)
    st_ret = jnp.zeros((BATCH, DEPTH, 2, RET_HEADS, RET_DK, RET_DV), F32)
    y_lru = jnp.zeros((N_TOK, D_RNN), BF16)
    y_mla = jnp.zeros((N_TOK, MLA_HEADS * MLA_V), BF16)
    y_ret = jnp.zeros((N_TOK, RET_HEADS * RET_DV), BF16)

    kv_cache = _mm(cache_ckv_rows, mla_wukv, lambda i: i % DEPTH, out_dtype=BF16, tm=PAST_LEN)

    h, = _resnorm(x_ctx, x_lat, mod, g_all, nxt=(0, 0, 0))
    for l in range(DEPTH):
        proj_a = _mm(h, w_in_t, l, col0=0, ncols=2 * D_RNN + MLA_SLAB, w_t=True)
        kscale = (wq, RET_DK ** -0.5)
        qk_ctx = _mm(h, w_in_t, l, col0=COL_TAIL, ncols=2 * wq, out_dtype=BF16, m=N_CTX, kscale=kscale,
                     w_t=True)
        qk_lat = _mm(h, w_in_t, l, col0=COL_TAIL, ncols=2 * wq, out_dtype=BF16, m=N_LAT, tm=DEC_SEQ,
                     row_map=lambda i: N_CTX // DEC_SEQ + i, kscale=kscale,
                     rope=(tab_ret, RET_DK // 4), w_t=True)
        vg = _mm(h, w_in_t, l, col0=COL_TAIL + 2 * wq, ncols=2 * wq + 3 * D_MODEL, out_dtype=BF16,
                 act=("sigmoid_from", 2 * wq), w_t=True, tn=1024)

        y_lru, st_lru = _lru(proj_a, lw, l, T=SEQ, nb=BATCH, blk0=0, prev=y_lru, st_prev=st_lru)
        y_lru, = _lru(proj_a, lw, l, T=DEC_SEQ, nb=DEC_BATCH, blk0=N_CTX // DEC_SEQ, h0=h0_lat, prev=y_lru)

        q, kv, kr, *caches = _mla_front(proj_a, gq, gkv, wuq, mla_wukv, l, tab_mla, caches)
        y_mla = _attn_ctx(q, kv, kr, y_mla)
        y_mla = _attn_lat(q, kv, kr, kv_cache, cache_mla_krope, l, tab_mla, y_mla)

        y_ret, st_ret = _ret_ctx(qk_ctx, vg, ret_decay, l, y_ret, st_ret)
        y_ret = _ret_lat(qk_lat, vg, ret_decay, state_ret, l, y_ret)

        merged = _branch_merge((y_lru, y_mla, y_ret), (w_br_lru, w_br_mla, w_br_ret), vg, l, gate_col0=2 * wq)
        u = _mm(merged, w_out, l)
        x_ctx, x_lat, h2 = _resnorm(x_ctx, x_lat, mod, g_all, res=(u, l * 4 + 1, l, 2), nxt=(l * 4 + 2, l, 3))
        ff = _mm(h2, w_ff1, l, out_dtype=BF16, act="relu2", tn=1024)
        y = _mm(ff, w_ff2, l, tn=1024, tk=1024)
        if l + 1 < DEPTH:
            x_ctx, x_lat, h = _resnorm(x_ctx, x_lat, mod, g_all, res=(y, l * 4 + 3, l, 5),
                                       nxt=((l + 1) * 4, l + 1, 0))
        else:
            x_ctx, x_lat = _resnorm(x_ctx, x_lat, mod, g_all, res=(y, l * 4 + 3, l, 5))

    new_ckv, new_krope = caches
    return (x_ctx.reshape(BATCH, SEQ, D_MODEL), x_lat.reshape(DEC_BATCH, DEC_SEQ, D_MODEL),
            new_ckv, new_krope, st_lru, st_ret)
```

```python
import functools

import jax
import jax.numpy as jnp
from jax import lax
from jax.experimental import pallas as pl
from jax.experimental.pallas import tpu as pltpu

F32 = jnp.float32
BF16 = jnp.bfloat16

D_MODEL = 2048
BATCH = 16
SEQ = 256
DEPTH = 4
DEC_BATCH = 4
DEC_SEQ = 1024
PAST_LEN = 256
GRID_W = 64
EPS = 1e-6
ROPE_BASE = 10000.0
D_RNN = D_MODEL // 2
LRU_BLOCKS = 8
LRU_BS = D_RNN // LRU_BLOCKS
CONV_W = 4
LRU_C = 8.0
MLA_HEADS = 8
MLA_NOPE = 128
MLA_ROPE = 64
MLA_V = 128
MLA_Q_RANK = D_MODEL // 4
MLA_KV_RANK = D_MODEL // 8
RET_HEADS = 8
RET_DK = 128
RET_DV = 128
D_FF = 4 * D_MODEL

N_CTX = BATCH * SEQ
N_LAT = DEC_BATCH * DEC_SEQ
N_TOK = N_CTX + N_LAT
COL_TAIL = 2 * D_RNN + MLA_Q_RANK + MLA_KV_RANK + MLA_ROPE
MLA_QH = 256
MLA_SLAB = 1024
GELU_TANH_SCALE = 0.7978845608028654
GELU_TANH_CUBIC = 0.044715

LANES = 128
SUBLANES = 8
VMEM_LIMIT = 56 * 1024 * 1024
ROW_TILE = 256
RES_TILE = 512
LAT_Q_TILE = 512


def _params(*sem):
    return pltpu.CompilerParams(dimension_semantics=sem, vmem_limit_bytes=VMEM_LIMIT)


def _sigmoid(x):
    return 0.5 * (1.0 + jnp.tanh(0.5 * x))


def _softplus(x):
    return jnp.maximum(x, 0.0) + jnp.log(1.0 + jnp.exp(-jnp.abs(x)))


def _rms(x, g):
    return x * lax.rsqrt(jnp.mean(x * x, axis=-1, keepdims=True) + EPS) * g


def _dot(a, b):
    return jnp.dot(a, b, preferred_element_type=F32)


def _dot_nt(a, b):
    return lax.dot_general(a, b, (((1,), (1,)), ((), ())), preferred_element_type=F32)


def _dot_tn(a, b):
    return lax.dot_general(a, b, (((0,), (0,)), ((), ())), preferred_element_type=F32)


def _rope(x, cos, sa, sb, quarter):
    w = x.shape[-1]
    reps = w // cos.shape[-1]
    if reps > 1:
        cos, sa, sb = (jnp.tile(t, (1, reps)) for t in (cos, sa, sb))
    return x * cos + pltpu.roll(x, quarter, 1) * sa + pltpu.roll(x, w - quarter, 1) * sb


def _mod_row(row0):
    return jnp.where(row0 < N_CTX, 0, 1 + (row0 - N_CTX) // DEC_SEQ)


def _ada_kernel(c_ref, w_ref, b_ref, o_ref):
    c = c_ref[...]
    s = c * _sigmoid(c)
    o_ref[...] = _dot(s.astype(BF16), w_ref[...].astype(BF16)) + b_ref[...]


def _ada(cond8, w_mod, b_mod):
    tn = 1024
    n = w_mod.shape[-1]
    return pl.pallas_call(
        _ada_kernel,
        grid=(DEPTH, n // tn),
        in_specs=[pl.BlockSpec((8, D_MODEL), lambda l, j: (0, 0)),
                  pl.BlockSpec((None, D_MODEL, tn), lambda l, j: (l, 0, j)),
                  pl.BlockSpec((None, 1, tn), lambda l, j: (l, 0, j))],
        out_specs=pl.BlockSpec((None, 8, tn), lambda l, j: (l, 0, j)),
        out_shape=jax.ShapeDtypeStruct((DEPTH, 8, n), F32),
        compiler_params=_params("arbitrary", "arbitrary"),
        name="ada",
    )(cond8, w_mod, b_mod.reshape(DEPTH, 1, n))


def _resnorm_kernel(*refs, has_res, has_next):
    refs = list(refs)
    xc_ref, xl_ref = refs[:2]
    refs = refs[2:]
    is_ctx = pl.program_id(0) < N_CTX // RES_TILE
    x = jnp.where(is_ctx, xc_ref[...], xl_ref[...])
    if has_res:
        u_ref, gpost_ref, gate_ref = refs[:3]
        refs = refs[3:]
        x = x + gate_ref[...] * _rms(u_ref[...], gpost_ref[...])
    if has_next:
        gpre_ref, shift_ref, scale_ref = refs[:3]
        refs = refs[3:]
    if has_res:
        xco_ref, xlo_ref = refs[:2]
        refs = refs[2:]

        @pl.when(is_ctx)
        def _():
            xco_ref[...] = x

        @pl.when(jnp.logical_not(is_ctx))
        def _():
            xlo_ref[...] = x
    if has_next:
        ho_ref = refs.pop(0)
        h = _rms(x, gpre_ref[...]) * (1.0 + scale_ref[...]) + shift_ref[...]
        ho_ref[...] = h.astype(BF16)


def _resnorm(x_ctx, x_lat, mod, g_norm, *, res=None, nxt=None):
    tr = RES_TILE
    nctx = N_CTX // tr
    row = pl.BlockSpec((tr, D_MODEL), lambda i: (i, 0))
    crow = pl.BlockSpec((tr, D_MODEL), lambda i: (jnp.minimum(i, nctx - 1), 0))
    lrow = pl.BlockSpec((tr, D_MODEL), lambda i: (jnp.maximum(i - nctx, 0), 0))

    def gspec(k):
        return pl.BlockSpec((None, 1, D_MODEL), lambda i: (k, 0, 0))

    def mspec(l, chunk):
        return pl.BlockSpec((None, 1, D_MODEL), lambda i: (l * 8 + _mod_row(i * tr), 0, chunk))

    args, specs, outs, ospecs = [x_ctx, x_lat], [crow, lrow], [], []
    if res is not None:
        u, gk, l, chunk = res
        args += [u, g_norm, mod]
        specs += [row, gspec(gk), mspec(l, chunk)]
        outs += [jax.ShapeDtypeStruct((N_CTX, D_MODEL), F32), jax.ShapeDtypeStruct((N_LAT, D_MODEL), F32)]
        ospecs += [crow, lrow]
    if nxt is not None:
        gk, l, chunk = nxt
        args += [g_norm, mod, mod]
        specs += [gspec(gk), mspec(l, chunk), mspec(l, chunk + 1)]
        outs.append(jax.ShapeDtypeStruct((N_TOK, D_MODEL), BF16))
        ospecs.append(row)
    return pl.pallas_call(
        functools.partial(_resnorm_kernel, has_res=res is not None, has_next=nxt is not None),
        grid=(N_TOK // tr,),
        in_specs=specs, out_specs=ospecs, out_shape=outs,
        compiler_params=_params("arbitrary"),
        name="resnorm",
    )(*args)


def _mm_kernel(x_ref, w_ref, *rest, nk, act, kscale, rope_quarter, w_t):
    o_ref = rest[-1]
    if nk == 1:
        if w_t:
            part = _dot_nt(x_ref[...].astype(BF16), w_ref[0].astype(BF16))
        else:
            part = _dot(x_ref[...].astype(BF16), w_ref[...].astype(BF16))
        if act == "relu2":
            part = jnp.square(jnp.maximum(part, 0.0))
        elif act == "sigmoid":
            part = _sigmoid(part)
        elif act is not None:
            part = jnp.where(pl.program_id(1) >= act[1], _sigmoid(part), part)
        if kscale is not None:
            part = part * jnp.where(pl.program_id(1) >= kscale[0], kscale[1], 1.0)
        if rope_quarter is not None:
            cos_ref, sa_ref, sb_ref = rest[:3]
            part = _rope(part, cos_ref[...], sa_ref[...], sb_ref[...], rope_quarter)
        o_ref[...] = part.astype(o_ref.dtype)
        return

    @pl.when(pl.program_id(2) == 0)
    def _():
        o_ref[...] = jnp.zeros(o_ref.shape, o_ref.dtype)

    o_ref[...] += _dot(x_ref[...].astype(BF16), w_ref[...].astype(BF16))


def _mm(x, w, l, *, col0=0, ncols=None, out_dtype=F32, act=None, tm=2048, tn=512, tk=None,
        m=None, row_map=None, kscale=None, rope=None, w_t=False):
    kdim = w.shape[2] if w_t else w.shape[1]
    ncols = (w.shape[1] if w_t else w.shape[2]) - col0 if ncols is None else ncols
    m = x.shape[0] if m is None else m
    tk = kdim if tk is None else tk
    tm, tn = min(tm, m), min(tn, ncols)
    assert m % tm == 0 and ncols % tn == 0 and kdim % tk == 0
    nk = kdim // tk
    assert nk == 1 or (act is None and out_dtype == F32 and kscale is None and rope is None and not w_t)
    row_map = (lambda i: i) if row_map is None else row_map
    args = [x, w]
    if w_t:
        assert col0 % SUBLANES == 0 and tn % SUBLANES == 0
        wspec = pl.BlockSpec((pl.Element(1), pl.Element(tn), pl.Element(tk)),
                             lambda i, j, k: (l, pl.multiple_of(col0 + j * tn, SUBLANES), 0))
    else:
        assert col0 % tn == 0
        jb = col0 // tn
        layer = l if callable(l) else (lambda i: l)
        wspec = pl.BlockSpec((None, tk, tn), lambda i, j, k: (layer(i), k, jb + j))
    in_specs = [pl.BlockSpec((tm, tk), lambda i, j, k: (row_map(i), k)), wspec]
    if kscale is not None:
        assert kscale[0] % tn == 0
        kscale = (kscale[0] // tn, kscale[1])
    if isinstance(act, tuple):
        assert act[1] % tn == 0
        act = (act[0], act[1] // tn)
    if rope is not None:
        tabs, quarter = rope
        assert tm == DEC_SEQ and tabs[0].shape == (DEC_SEQ, LANES)
        args += list(tabs)
        in_specs += [pl.BlockSpec((tm, LANES), lambda i, j, k: (0, 0))] * 3
    return pl.pallas_call(
        functools.partial(_mm_kernel, nk=nk, act=act, kscale=kscale,
                          rope_quarter=None if rope is None else rope[1], w_t=w_t),
        grid=(m // tm, ncols // tn, nk),
        in_specs=in_specs,
        out_specs=pl.BlockSpec((tm, tn), lambda i, j, k: (i, j)),
        out_shape=jax.ShapeDtypeStruct((m, ncols), out_dtype),
        compiler_params=_params("arbitrary", "arbitrary", "arbitrary"),
        name="mm",
    )(*args)


def _bm_kernel(a0_ref, a1_ref, a2_ref, w0_ref, w1_ref, w2_ref, g0_ref, g1_ref, g2_ref, o_ref):
    acc = g0_ref[...].astype(F32) * _dot(a0_ref[...], w0_ref[...].astype(BF16))
    acc = acc + g1_ref[...].astype(F32) * _dot(a1_ref[...], w1_ref[...].astype(BF16))
    acc = acc + g2_ref[...].astype(F32) * _dot(a2_ref[...], w2_ref[...].astype(BF16))
    o_ref[...] = acc.astype(BF16)


def _branch_merge(acts, ws, gates, l, *, gate_col0=0, tm=2048, tn=256):
    nj = D_MODEL // tn
    kdim = acts[0].shape[1]
    assert gate_col0 % tn == 0
    g0 = gate_col0 // tn
    aspec = pl.BlockSpec((tm, kdim), lambda i, j: (i, 0))
    wspec = pl.BlockSpec((None, kdim, tn), lambda i, j: (l, 0, j))
    gspecs = [pl.BlockSpec((tm, tn), lambda i, j, b=b: (i, g0 + b * nj + j)) for b in range(3)]
    return pl.pallas_call(
        _bm_kernel,
        grid=(N_TOK // tm, nj),
        in_specs=[aspec] * 3 + [wspec] * 3 + gspecs,
        out_specs=pl.BlockSpec((tm, tn), lambda i, j: (i, j)),
        out_shape=jax.ShapeDtypeStruct((N_TOK, D_MODEL), BF16),
        compiler_params=_params("arbitrary", "arbitrary"),
        name="branch_merge",
    )(*acts, *ws, gates, gates, gates)


LRU_CT = 512


def _lru_kernel(*refs, T, has_h0, has_state):
    xr_ref, xg_ref, cw_ref, cb_ref, wa_ref, ba_ref, wx_ref, bx_ref, lam_ref = refs[:9]
    refs = refs[9:]
    if has_h0:
        h0_ref = refs[0]
        refs = refs[1:]
    if has_state:
        y_ref, st_ref = refs[2:4]
        refs = refs[4:]
    else:
        y_ref = refs[1]
        refs = refs[2:]
    x3_ref, p_ref, a_ref, b_ref, hl_ref, pc_ref, hn_ref = refs
    nblk = xr_ref.shape[1] // LRU_BS
    S = SUBLANES
    G = T // S
    H = CONV_W - 1

    pitch = G + S
    for n in range(nblk):
        for k in range(S):
            x3_ref[n, k * pitch:k * pitch + G, :] = xr_ref[k * G:(k + 1) * G,
                                                           n * LRU_BS:(n + 1) * LRU_BS].astype(F32)

    def gather(p, _):
        r0 = pl.multiple_of((p + H) * S, S)
        for n in range(nblk):
            p_ref.at[n][pl.ds(r0, S), :] = x3_ref.at[n][pl.ds(p, S, stride=pitch), :]
        return 0

    lax.fori_loop(0, G, gather, 0, unroll=4)
    row = lax.broadcasted_iota(jnp.int32, (S, LRU_BS), 0)
    for n in range(nblk):
        for i in range(H):
            tail = p_ref[n, (G + i) * S:(G + i + 1) * S, :]
            p_ref[n, i * S:(i + 1) * S, :] = jnp.where(row == 0, 0.0, pltpu.roll(tail, 1, 0))
            head = p_ref[n, (H + i) * S:(H + i + 1) * S, :]
            p_ref[n, (H + G + i) * S:(H + G + i + 1) * S, :] = jnp.where(row == S - 1, 0.0,
                                                                        pltpu.roll(head, S - 1, 0))

    cmats = []
    for d in range(2):
        for n in range(nblk):
            cs = slice(n * LRU_BS, (n + 1) * LRU_BS)
            xc = jnp.zeros((T, LRU_BS), F32) + cb_ref[d, :, cs]
            for j in range(CONV_W):
                off = j if d == 0 else 2 * H - j
                xc = xc + cw_ref[d, j:j + 1, cs] * p_ref[n, off * S:off * S + T, :]
            xcb = xc.astype(BF16)
            t_r = jnp.tanh(_dot(xcb, (0.5 * wa_ref[d, n]).astype(BF16)) + 0.5 * ba_ref[d, :, cs])
            t_i = jnp.tanh(_dot(xcb, (0.5 * wx_ref[d, n]).astype(BF16)) + 0.5 * bx_ref[d, :, cs])
            half_c = (-0.5 * LRU_C) * _softplus(-lam_ref[d, :, cs])
            a = jnp.exp(half_c + half_c * t_r)
            a_ref[n] = a
            b_ref[n] = (0.5 * jnp.sqrt(1.0 - a * a)) * (xc + t_i * xc)

        def body(it, carry, d=d):
            p = it if d == 0 else G - 1 - it
            r0 = pl.multiple_of(p * S, S)
            out = []
            for n in range(nblk):
                h, pc = carry[n]
                av = a_ref.at[n][pl.ds(r0, S), :]
                h = av * h + b_ref.at[n][pl.ds(r0, S), :]
                pc = av * pc
                hl_ref.at[d, n][pl.ds(r0, S), :] = h
                pc_ref.at[d, n][pl.ds(r0, S), :] = pc
                out.append((h, pc))
            return tuple(out)

        init = tuple((jnp.zeros((S, LRU_BS), F32), jnp.ones((S, LRU_BS), F32)) for _ in range(nblk))
        ends = lax.fori_loop(0, G, body, init, unroll=4)

        last_d, cmat_d = [], []
        for n in range(nblk):
            cs = slice(n * LRU_BS, (n + 1) * LRU_BS)
            h_end, pc_end = ends[n]
            c = h0_ref[d, :, cs] if has_h0 else jnp.zeros((1, LRU_BS), F32)
            rows = [None] * S
            for k in (range(S) if d == 0 else range(S - 1, -1, -1)):
                rows[k] = c
                c = h_end[k:k + 1, :] + pc_end[k:k + 1, :] * c
            last_d.append(c)
            cmat_d.append(jnp.concatenate(rows, axis=0))
        cmats.append(cmat_d)
        if has_state:
            for n in range(nblk):
                st_ref[d:d + 1, n * LRU_BS:(n + 1) * LRU_BS] = last_d[n]

    def scatter(p, _):
        r0 = pl.multiple_of(p * S, S)
        for n in range(nblk):
            h = (hl_ref.at[0, n][pl.ds(r0, S), :] + pc_ref.at[0, n][pl.ds(r0, S), :] * cmats[0][n]
                 + hl_ref.at[1, n][pl.ds(r0, S), :] + pc_ref.at[1, n][pl.ds(r0, S), :] * cmats[1][n])
            hn_ref.at[n][pl.ds(p, S, stride=pitch), :] = h
        return 0

    lax.fori_loop(0, G, scatter, 0, unroll=4)

    for n in range(nblk):
        cs = slice(n * LRU_BS, (n + 1) * LRU_BS)
        for k in range(S):
            xg = xg_ref[k * G:(k + 1) * G, cs].astype(F32)
            gelu = 0.5 * xg * (1.0 + jnp.tanh(GELU_TANH_SCALE * (xg + GELU_TANH_CUBIC * (xg * xg * xg))))
            y_ref[k * G:(k + 1) * G, cs] = (hn_ref[n, k * pitch:k * pitch + G, :] * gelu).astype(BF16)


def _lru(proj_x, lw, l, *, T, nb, blk0, prev, h0=None, st_prev=None):
    ct = LRU_CT
    ncb = D_RNN // ct
    bpc = ct // LRU_BS
    in_specs = [
        pl.BlockSpec((T, ct), lambda b, c: (blk0 + b, c)),
        pl.BlockSpec((T, ct), lambda b, c: (blk0 + b, ncb + c)),
        pl.BlockSpec((None, 2, CONV_W, ct), lambda b, c: (l, 0, 0, c)),
        pl.BlockSpec((None, 2, 1, ct), lambda b, c: (l, 0, 0, c)),
        pl.BlockSpec((None, 2, bpc, LRU_BS, LRU_BS), lambda b, c: (l, 0, c, 0, 0)),
        pl.BlockSpec((None, 2, 1, ct), lambda b, c: (l, 0, 0, c)),
        pl.BlockSpec((None, 2, bpc, LRU_BS, LRU_BS), lambda b, c: (l, 0, c, 0, 0)),
        pl.BlockSpec((None, 2, 1, ct), lambda b, c: (l, 0, 0, c)),
        pl.BlockSpec((None, 2, 1, ct), lambda b, c: (l, 0, 0, c)),
    ]
    args = [proj_x, proj_x, lw["conv_w"], lw["conv_b"], lw["wa"], lw["ba"], lw["wx"], lw["bx"], lw["lam"]]
    if h0 is not None:
        in_specs.append(pl.BlockSpec((None, None, 2, 1, ct), lambda b, c: (b, l, 0, 0, c)))
        args.append(h0)
    has_state = st_prev is not None
    aliases = {len(args): 0}
    in_specs.append(pl.BlockSpec(memory_space=pl.ANY))
    args.append(prev)
    out_specs = [pl.BlockSpec((T, ct), lambda b, c: (blk0 + b, c))]
    out_shape = [jax.ShapeDtypeStruct(prev.shape, prev.dtype)]
    if has_state:
        aliases[len(args)] = 1
        in_specs.append(pl.BlockSpec(memory_space=pl.ANY))
        args.append(st_prev)
        out_specs.append(pl.BlockSpec((None, None, 2, ct), lambda b, c: (b, l, 0, c)))
        out_shape.append(jax.ShapeDtypeStruct(st_prev.shape, st_prev.dtype))
    return pl.pallas_call(
        functools.partial(_lru_kernel, T=T, has_h0=h0 is not None, has_state=has_state),
        grid=(nb, ncb),
        in_specs=in_specs,
        out_specs=out_specs,
        out_shape=out_shape,
        input_output_aliases=aliases,
        scratch_shapes=[pltpu.VMEM((bpc, T + SUBLANES * SUBLANES, LRU_BS), F32),
                        pltpu.VMEM((bpc, T + 2 * (CONV_W - 1) * SUBLANES, LRU_BS), F32),
                        pltpu.VMEM((bpc, T, LRU_BS), F32), pltpu.VMEM((bpc, T, LRU_BS), F32),
                        pltpu.VMEM((2, bpc, T, LRU_BS), F32), pltpu.VMEM((2, bpc, T, LRU_BS), F32),
                        pltpu.VMEM((bpc, T + SUBLANES * SUBLANES, LRU_BS), F32)],
        compiler_params=_params("arbitrary", "arbitrary"),
        name="lru",
    )(*args)


MLA_TILE = DEC_SEQ
MLA_TILE_SEQS = MLA_TILE // SEQ


def _mla_front_kernel(p_ref, gq_ref, gkv_ref, wuq_ref, wukv_ref, cos_ref, sa_ref, sb_ref, *rest):
    q_ref, kv_ref, kr_ref, ckv_out_ref, kr_out_ref = rest[-5:]
    i = pl.program_id(0)
    cq = _rms(p_ref[:, 0:MLA_Q_RANK], gq_ref[...]).astype(BF16)
    q_ref[...] = _dot(cq, wuq_ref[...].astype(BF16)).astype(BF16)
    ckv = _rms(p_ref[:, MLA_Q_RANK:MLA_Q_RANK + MLA_KV_RANK], gkv_ref[...])
    kv_ref[...] = _dot(ckv.astype(BF16), wukv_ref[...].astype(BF16)).astype(BF16)
    k0 = MLA_Q_RANK + MLA_KV_RANK
    kr = p_ref[:, k0:k0 + LANES]

    @pl.when(i < N_CTX // MLA_TILE)
    def _():
        kr_ref[...] = kr
        for s in range(MLA_TILE_SEQS):
            ckv_out_ref[s] = ckv[s * SEQ:(s + 1) * SEQ, :]
            kr_out_ref[s] = kr[s * SEQ:(s + 1) * SEQ, 0:MLA_ROPE]

    @pl.when(i >= N_CTX // MLA_TILE)
    def _():
        kr_ref[...] = _rope(kr, cos_ref[...], sa_ref[...], sb_ref[...], MLA_ROPE // 4)


def _mla_front(proj_m, gq, gkv, wuq, wukv, l, tab, caches):
    tr = MLA_TILE
    last_ctx = N_CTX // tr - 1
    tspec = pl.BlockSpec((tr, LANES), lambda i: (0, 0))
    kvw = MLA_HEADS * MLA_QH
    args = [proj_m, gq, gkv, wuq, wukv, *tab]
    in_specs = [pl.BlockSpec((tr, MLA_SLAB), lambda i: (i, 0)),
                pl.BlockSpec((None, 1, MLA_Q_RANK), lambda i: (l, 0, 0)),
                pl.BlockSpec((None, 1, MLA_KV_RANK), lambda i: (l, 0, 0)),
                pl.BlockSpec((None, MLA_Q_RANK, kvw), lambda i: (l, 0, 0)),
                pl.BlockSpec((None, MLA_KV_RANK, kvw), lambda i: (l, 0, 0)),
                tspec, tspec, tspec]
    aliases = {len(args): 3, len(args) + 1: 4}
    args += list(caches)
    in_specs += [pl.BlockSpec(memory_space=pl.ANY)] * 2
    return pl.pallas_call(
        _mla_front_kernel,
        grid=(N_TOK // tr,),
        in_specs=in_specs,
        out_specs=[pl.BlockSpec((tr, kvw), lambda i: (i, 0)),
                   pl.BlockSpec((tr, kvw), lambda i: (i, 0)),
                   pl.BlockSpec((tr, LANES), lambda i: (i, 0)),
                   pl.BlockSpec((MLA_TILE_SEQS, None, SEQ, MLA_KV_RANK),
                                lambda i: (jnp.minimum(i, last_ctx), l, 0, 0)),
                   pl.BlockSpec((MLA_TILE_SEQS, None, SEQ, MLA_ROPE),
                                lambda i: (jnp.minimum(i, last_ctx), l, 0, 0))],
        out_shape=[jax.ShapeDtypeStruct((N_TOK, kvw), BF16),
                   jax.ShapeDtypeStruct((N_TOK, kvw), BF16),
                   jax.ShapeDtypeStruct((N_TOK, LANES), F32),
                   jax.ShapeDtypeStruct((BATCH, DEPTH, SEQ, MLA_KV_RANK), F32),
                   jax.ShapeDtypeStruct((BATCH, DEPTH, SEQ, MLA_ROPE), F32)],
        input_output_aliases=aliases,
        compiler_params=_params("arbitrary"),
        name="mla_front",
    )(*args)


def _attn_kernel(*refs, lat):
    if lat:
        (q_ref, kvn_ref, krn_ref, kvc_ref, krc_ref, cos_ref, sa_ref, sb_ref, _,
         o_ref, kcat_ref, vcat_ref) = refs
    else:
        q_ref, kvn_ref, krn_ref, _, o_ref, kcat_ref = refs
    scale = (MLA_NOPE + MLA_ROPE) ** -0.5
    s_new = kvn_ref.shape[0]
    off = PAST_LEN if lat else 0

    def build():
        lane = lax.broadcasted_iota(jnp.int32, (s_new, LANES), 1)
        kr_new = jnp.where(lane < MLA_ROPE, krn_ref[...], 0.0).astype(BF16)
        for h in range(MLA_HEADS):
            c0 = h * MLA_QH
            kcat_ref[h, off:off + s_new, 0:MLA_NOPE] = kvn_ref[:, c0:c0 + MLA_NOPE]
            kcat_ref[h, off:off + s_new, MLA_NOPE:MLA_QH] = kr_new
            if lat:
                kcat_ref[h, 0:off, 0:MLA_NOPE] = kvc_ref[:, c0:c0 + MLA_NOPE]
                kcat_ref[h, 0:off, MLA_NOPE:MLA_NOPE + MLA_ROPE] = krc_ref[...].astype(BF16)
                kcat_ref[h, 0:off, MLA_NOPE + MLA_ROPE:MLA_QH] = jnp.zeros((off, MLA_QH - MLA_NOPE - MLA_ROPE), BF16)
                vcat_ref[0:off, h * MLA_V:(h + 1) * MLA_V] = kvc_ref[:, c0 + MLA_NOPE:c0 + MLA_QH]
                vcat_ref[off:off + s_new, h * MLA_V:(h + 1) * MLA_V] = kvn_ref[:, c0 + MLA_NOPE:c0 + MLA_QH]

    if lat:
        pl.when(pl.program_id(1) == 0)(build)
    else:
        build()

    def scores(h):
        c0 = h * MLA_QH
        qn = q_ref[:, c0:c0 + MLA_NOPE].astype(F32)
        qr = q_ref[:, c0 + MLA_NOPE:c0 + MLA_QH].astype(F32)
        if lat:
            qr = _rope(qr, cos_ref[...], sa_ref[...], sb_ref[...], MLA_ROPE // 4)
        qh = (jnp.concatenate([qn, qr], axis=1) * scale).astype(BF16)
        return _dot_nt(qh, kcat_ref[h])

    def attend(h, s):
        p = jnp.exp(s - jnp.max(s, axis=-1, keepdims=True))
        den = jnp.sum(p, axis=-1, keepdims=True)
        v = (vcat_ref[:, h * MLA_V:(h + 1) * MLA_V] if lat
             else kvn_ref[:, h * MLA_QH + MLA_NOPE:(h + 1) * MLA_QH])
        o_ref[:, h * MLA_V:(h + 1) * MLA_V] = (_dot(p.astype(BF16), v) / den).astype(BF16)

    if lat:
        s_next = scores(0)
        for h in range(MLA_HEADS):
            s_cur = s_next
            if h + 1 < MLA_HEADS:
                s_next = scores(h + 1)
            attend(h, s_cur)
    else:
        all_scores = [scores(h) for h in range(MLA_HEADS)]
        for h in range(MLA_HEADS):
            attend(h, all_scores[h])


def _attn_ctx(q, kv, kr, prev):
    t = SEQ
    return pl.pallas_call(
        functools.partial(_attn_kernel, lat=False),
        grid=(BATCH,),
        in_specs=[pl.BlockSpec((t, MLA_HEADS * MLA_QH), lambda b: (b, 0)),
                  pl.BlockSpec((t, MLA_HEADS * MLA_QH), lambda b: (b, 0)),
                  pl.BlockSpec((t, LANES), lambda b: (b, 0)),
                  pl.BlockSpec(memory_space=pl.ANY)],
        out_specs=pl.BlockSpec((t, MLA_HEADS * MLA_V), lambda b: (b, 0)),
        out_shape=jax.ShapeDtypeStruct(prev.shape, prev.dtype),
        input_output_aliases={3: 0},
        scratch_shapes=[pltpu.VMEM((MLA_HEADS, t, MLA_QH), BF16)],
        compiler_params=_params("arbitrary"),
        name="attn_ctx",
    )(q, kv, kr, prev)


def _attn_lat(q, kv, kr, kv_cache, cache_krope, l, tab, prev):
    tq = ROW_TILE
    nq = DEC_SEQ // tq
    blk0 = N_CTX // DEC_SEQ
    qblk0 = N_CTX // tq
    tspec = pl.BlockSpec((tq, LANES), lambda b, i: (i, 0))
    return pl.pallas_call(
        functools.partial(_attn_kernel, lat=True),
        grid=(DEC_BATCH, nq),
        in_specs=[pl.BlockSpec((tq, MLA_HEADS * MLA_QH), lambda b, i: (qblk0 + b * nq + i, 0)),
                  pl.BlockSpec((DEC_SEQ, MLA_HEADS * MLA_QH), lambda b, i: (blk0 + b, 0)),
                  pl.BlockSpec((DEC_SEQ, LANES), lambda b, i: (blk0 + b, 0)),
                  pl.BlockSpec((PAST_LEN, MLA_HEADS * MLA_QH), lambda b, i: (b * DEPTH + l, 0)),
                  pl.BlockSpec((None, None, PAST_LEN, MLA_ROPE), lambda b, i: (b, l, 0, 0)),
                  tspec, tspec, tspec, pl.BlockSpec(memory_space=pl.ANY)],
        out_specs=pl.BlockSpec((tq, MLA_HEADS * MLA_V), lambda b, i: (qblk0 + b * nq + i, 0)),
        out_shape=jax.ShapeDtypeStruct((N_TOK, MLA_HEADS * MLA_V), BF16),
        input_output_aliases={8: 0},
        scratch_shapes=[pltpu.VMEM((MLA_HEADS, PAST_LEN + DEC_SEQ, MLA_QH), BF16),
                        pltpu.VMEM((PAST_LEN + DEC_SEQ, MLA_HEADS * MLA_V), BF16)],
        compiler_params=_params("arbitrary", "arbitrary"),
        name="attn_lat",
    )(q, kv, kr, kv_cache, cache_krope, *tab, prev)


def _log_sigmoid(x):
    return -_softplus(-x)


def _ret_kernel(*refs, lat, T):
    if lat:
        q_ref, k_ref, v_ref, rg_ref, dec_ref, s0_ref, _, o_ref, decay_ref = refs
    else:
        q_ref, k_ref, v_ref, rg_ref, dec_ref = refs[:5]
        o_ref, st_ref, decay_ref = refs[-3:]
    tq = q_ref.shape[0]
    t0 = pl.program_id(0) * tq if lat else 0
    batch_id = pl.program_id(1) if lat else pl.program_id(0)
    lg = _log_sigmoid(dec_ref[...])
    tcol = (t0 + lax.broadcasted_iota(jnp.int32, (tq, 1), 0)).astype(F32)

    @pl.when(batch_id == 0)
    def _():
        rows = (t0 + lax.broadcasted_iota(jnp.int32, (tq, T), 0)).astype(F32)
        cols = lax.broadcasted_iota(jnp.int32, (tq, T), 1).astype(F32)
        diff = rows - cols
        fwd = diff >= 0.0
        dist = jnp.abs(diff)
        for h in range(RET_HEADS):
            one_sided = jnp.exp(jnp.where(fwd, lg[0:1, h:h + 1], lg[1:2, h:h + 1]) * dist)
            decay_ref[h] = jnp.where(dist == 0.0, 2.0, one_sided)

    heads = [slice(h * RET_DK, (h + 1) * RET_DK) for h in range(RET_HEADS)]
    if lat:
        outs = []
        s_next = _dot_nt(q_ref[:, heads[0]], k_ref[:, heads[0]])
        for h, cs in enumerate(heads):
            s_cur = s_next
            if h + 1 < RET_HEADS:
                s_next = _dot_nt(q_ref[:, heads[h + 1]], k_ref[:, heads[h + 1]])
            outs.append(_dot((s_cur * decay_ref[h]).astype(BF16), v_ref[:, cs]))
    else:
        scores = [_dot_nt(q_ref[:, cs], k_ref[:, cs]) for cs in heads]
        probs = [(scores[h] * decay_ref[h]).astype(BF16) for h in range(RET_HEADS)]
        outs = [_dot(probs[h], v_ref[:, cs]) for h, cs in enumerate(heads)]
    for h, cs in enumerate(heads):
        lgf = lg[0:1, h:h + 1]
        lgb = lg[1:2, h:h + 1]
        if lat:
            qf = q_ref[:, cs].astype(F32)
            outs[h] = outs[h] + _dot((qf * jnp.exp(lgf * (tcol + 1.0))).astype(BF16), s0_ref[0, h].astype(BF16))
            outs[h] = outs[h] + _dot((qf * jnp.exp(lgb * (T - tcol))).astype(BF16), s0_ref[1, h].astype(BF16))
        else:
            vf = v_ref[:, cs].astype(F32)
            k = k_ref[:, cs]
            st_ref[0, h] = _dot_tn(k, (vf * jnp.exp(lgf * (T - 1.0 - tcol))).astype(BF16))
            st_ref[1, h] = _dot_tn(k, (vf * jnp.exp(lgb * tcol)).astype(BF16))
    for h, cs in enumerate(heads):
        o = outs[h]
        mu = jnp.mean(o, axis=-1, keepdims=True)
        oc = o - mu
        var = jnp.mean(oc * oc, axis=-1, keepdims=True)
        rg = rg_ref[:, cs].astype(F32)
        o_ref[:, cs] = (oc * lax.rsqrt(var + EPS) * (rg * _sigmoid(rg))).astype(BF16)


def _ret_ctx(qk, vg, decay, l, prev, st_prev):
    t = SEQ
    w = RET_HEADS * RET_DK
    return pl.pallas_call(
        functools.partial(_ret_kernel, lat=False, T=t),
        grid=(BATCH,),
        in_specs=[pl.BlockSpec((t, w), lambda b: (b, 0)),
                  pl.BlockSpec((t, w), lambda b: (b, 1)),
                  pl.BlockSpec((t, w), lambda b: (b, 0)),
                  pl.BlockSpec((t, w), lambda b: (b, 1)),
                  pl.BlockSpec((None, 2, RET_HEADS), lambda b: (l, 0, 0)),
                  pl.BlockSpec(memory_space=pl.ANY), pl.BlockSpec(memory_space=pl.ANY)],
        out_specs=[pl.BlockSpec((t, w), lambda b: (b, 0)),
                   pl.BlockSpec((None, None, 2, RET_HEADS, RET_DK, RET_DV), lambda b: (b, l, 0, 0, 0, 0))],
        out_shape=[jax.ShapeDtypeStruct(prev.shape, prev.dtype),
                   jax.ShapeDtypeStruct(st_prev.shape, st_prev.dtype)],
        input_output_aliases={5: 0, 6: 1},
        scratch_shapes=[pltpu.VMEM((RET_HEADS, t, t), F32)],
        compiler_params=_params("arbitrary"),
        name="ret_ctx",
    )(qk, qk, vg, vg, decay, prev, st_prev)


def _ret_lat(qk, vg, decay, state_ret, l, prev):
    tq = LAT_Q_TILE
    t = DEC_SEQ
    nq = t // tq
    w = RET_HEADS * RET_DK
    blk0 = N_CTX // t
    qblk0 = N_CTX // tq
    return pl.pallas_call(
        functools.partial(_ret_kernel, lat=True, T=t),
        grid=(nq, DEC_BATCH),
        in_specs=[pl.BlockSpec((tq, w), lambda i, b: (b * nq + i, 0)),
                  pl.BlockSpec((t, w), lambda i, b: (b, 1)),
                  pl.BlockSpec((t, w), lambda i, b: (blk0 + b, 0)),
                  pl.BlockSpec((tq, w), lambda i, b: (qblk0 + b * nq + i, 1)),
                  pl.BlockSpec((None, 2, RET_HEADS), lambda i, b: (l, 0, 0)),
                  pl.BlockSpec((None, None, 2, RET_HEADS, RET_DK, RET_DV), lambda i, b: (b, l, 0, 0, 0, 0)),
                  pl.BlockSpec(memory_space=pl.ANY)],
        out_specs=pl.BlockSpec((tq, w), lambda i, b: (qblk0 + b * nq + i, 0)),
        out_shape=jax.ShapeDtypeStruct((N_TOK, w), BF16),
        scratch_shapes=[pltpu.VMEM((RET_HEADS, tq, t), F32)],
        input_output_aliases={6: 0},
        compiler_params=_params("arbitrary", "arbitrary"),
        name="ret_lat",
    )(qk, qk, vg, vg, decay, state_ret, prev)


def _rope_tables(dim):
    half = dim // 2
    quarter = half // 2
    t = jnp.arange(DEC_SEQ)
    row = (t // GRID_W).astype(F32)
    col = (t % GRID_W).astype(F32)
    inv = ROPE_BASE ** (-jnp.arange(0, half, 2, dtype=F32) / half)
    zeros = jnp.zeros((DEC_SEQ, quarter), F32)
    cos_parts, sa_parts, sb_parts = [], [], []
    for pos in (row, col):
        ang = pos[:, None] * inv[None, :]
        c, s = jnp.cos(ang), jnp.sin(ang)
        cos_parts += [c, c]
        sa_parts += [zeros, s]
        sb_parts += [-s, zeros]
    padw = LANES - dim

    def build(parts):
        return jnp.concatenate(parts + ([jnp.zeros((DEC_SEQ, padw), F32)] if padw else []), axis=-1)

    return build(cos_parts), build(sa_parts), build(sb_parts)


def kernel(x_prompt, x_sample, c, cache_mla_ckv, cache_mla_krope, state_lru, state_ret, c_ctx, w_mod, b_mod,
           g_norm, w_in, lru_conv_w, lru_conv_b, lru_wa, lru_ba, lru_wx, lru_bx, lru_lam, mla_gq, mla_gkv,
           mla_wuq, mla_wukv, ret_decay, w_br_lru, w_br_mla, w_br_ret, w_out, w_ff1, w_ff2):
    x_ctx = x_prompt.reshape(N_CTX, D_MODEL)
    x_lat = x_sample.reshape(N_LAT, D_MODEL)
    cond8 = jnp.concatenate([c_ctx[None, :], c, jnp.zeros((8 - 1 - DEC_BATCH, D_MODEL), F32)], axis=0)
    mod = _ada(cond8, w_mod, b_mod).reshape(DEPTH * 8, 1, 6 * D_MODEL)
    g_all = g_norm.reshape(DEPTH * 4, 1, D_MODEL)

    w_in_t = jnp.swapaxes(w_in, 1, 2)
    wuq = jnp.pad(mla_wuq.reshape(DEPTH, MLA_Q_RANK, MLA_HEADS, MLA_NOPE + MLA_ROPE),
                  ((0, 0), (0, 0), (0, 0), (0, MLA_QH - MLA_NOPE - MLA_ROPE)))
    wuq = wuq.reshape(DEPTH, MLA_Q_RANK, MLA_HEADS * MLA_QH)
    lw = {"conv_w": lru_conv_w, "conv_b": lru_conv_b.reshape(DEPTH, 2, 1, D_RNN), "wa": lru_wa,
          "ba": lru_ba.reshape(DEPTH, 2, 1, D_RNN), "wx": lru_wx, "bx": lru_bx.reshape(DEPTH, 2, 1, D_RNN),
          "lam": lru_lam.reshape(DEPTH, 2, 1, D_RNN)}
    gq = mla_gq.reshape(DEPTH, 1, MLA_Q_RANK)
    gkv = mla_gkv.reshape(DEPTH, 1, MLA_KV_RANK)
    h0_lat = state_lru.reshape(DEC_BATCH, DEPTH, 2, 1, D_RNN)
    cache_ckv_rows = cache_mla_ckv.reshape(DEC_BATCH * DEPTH * PAST_LEN, MLA_KV_RANK)
    tab_mla = _rope_tables(MLA_ROPE)
    tab_ret = _rope_tables(RET_DK)
    wq = RET_HEADS * RET_DK

    caches = (jnp.zeros((BATCH, DEPTH, SEQ, MLA_KV_RANK), F32), jnp.zeros((BATCH, DEPTH, SEQ, MLA_ROPE), F32))
    st_lru = jnp.zeros((BATCH, DEPTH, 2, D_RNN), F32)
    st_ret = jnp.zeros((BATCH, DEPTH, 2, RET_HEADS, RET_DK, RET_DV), F32)
    y_lru = jnp.zeros((N_TOK, D_RNN), BF16)
    y_mla = jnp.zeros((N_TOK, MLA_HEADS * MLA_V), BF16)
    y_ret = jnp.zeros((N_TOK, RET_HEADS * RET_DV), BF16)

    kv_cache = _mm(cache_ckv_rows, mla_wukv, lambda i: i % DEPTH, out_dtype=BF16, tm=PAST_LEN)

    h, = _resnorm(x_ctx, x_lat, mod, g_all, nxt=(0, 0, 0))
    for l in range(DEPTH):
        proj_x = _mm(h, w_in_t, l, col0=0, ncols=2 * D_RNN, out_dtype=BF16, w_t=True)
        proj_m = _mm(h, w_in_t, l, col0=2 * D_RNN, ncols=MLA_SLAB, w_t=True)
        kscale = (wq, RET_DK ** -0.5)
        qk_ctx = _mm(h, w_in_t, l, col0=COL_TAIL, ncols=2 * wq, out_dtype=BF16, m=N_CTX, kscale=kscale,
                     w_t=True)
        qk_lat = _mm(h, w_in_t, l, col0=COL_TAIL, ncols=2 * wq, out_dtype=BF16, m=N_LAT, tm=DEC_SEQ,
                     row_map=lambda i: N_CTX // DEC_SEQ + i, kscale=kscale,
                     rope=(tab_ret, RET_DK // 4), w_t=True)
        vg = _mm(h, w_in_t, l, col0=COL_TAIL + 2 * wq, ncols=2 * wq + 3 * D_MODEL, out_dtype=BF16,
                 act=("sigmoid_from", 2 * wq), w_t=True, tn=1024)

        y_lru, st_lru = _lru(proj_x, lw, l, T=SEQ, nb=BATCH, blk0=0, prev=y_lru, st_prev=st_lru)
        y_lru, = _lru(proj_x, lw, l, T=DEC_SEQ, nb=DEC_BATCH, blk0=N_CTX // DEC_SEQ, h0=h0_lat, prev=y_lru)

        q, kv, kr, *caches = _mla_front(proj_m, gq, gkv, wuq, mla_wukv, l, tab_mla, caches)
        y_mla = _attn_ctx(q, kv, kr, y_mla)
        y_mla = _attn_lat(q, kv, kr, kv_cache, cache_mla_krope, l, tab_mla, y_mla)

        y_ret, st_ret = _ret_ctx(qk_ctx, vg, ret_decay, l, y_ret, st_ret)
        y_ret = _ret_lat(qk_lat, vg, ret_decay, state_ret, l, y_ret)

        merged = _branch_merge((y_lru, y_mla, y_ret), (w_br_lru, w_br_mla, w_br_ret), vg, l, gate_col0=2 * wq)
        u = _mm(merged, w_out, l)
        x_ctx, x_lat, h2 = _resnorm(x_ctx, x_lat, mod, g_all, res=(u, l * 4 + 1, l, 2), nxt=(l * 4 + 2, l, 3))
        ff = _mm(h2, w_ff1, l, out_dtype=BF16, act="relu2", tn=1024)
        y = _mm(ff, w_ff2, l, tn=1024, tk=1024)
        if l + 1 < DEPTH:
            x_ctx, x_lat, h = _resnorm(x_ctx, x_lat, mod, g_all, res=(y, l * 4 + 3, l, 5),
                                       nxt=((l + 1) * 4, l + 1, 0))
        else:
            x_ctx, x_lat = _resnorm(x_ctx, x_lat, mod, g_all, res=(y, l * 4 + 3, l, 5))

    new_ckv, new_krope = caches
    return (x_ctx.reshape(BATCH, SEQ, D_MODEL), x_lat.reshape(DEC_BATCH, DEC_SEQ, D_MODEL),
            new_ckv, new_krope, st_lru, st_ret)
```

```python
import functools

import jax
import jax.numpy as jnp
from jax import lax
from jax.experimental import pallas as pl
from jax.experimental.pallas import tpu as pltpu

F32 = jnp.float32
BF16 = jnp.bfloat16

D_MODEL = 2048
BATCH = 16
SEQ = 256
DEPTH = 4
DEC_BATCH = 4
DEC_SEQ = 1024
PAST_LEN = 256
GRID_W = 64
EPS = 1e-6
ROPE_BASE = 10000.0
D_RNN = D_MODEL // 2
LRU_BLOCKS = 8
LRU_BS = D_RNN // LRU_BLOCKS
CONV_W = 4
LRU_C = 8.0
MLA_HEADS = 8
MLA_NOPE = 128
MLA_ROPE = 64
MLA_V = 128
MLA_Q_RANK = D_MODEL // 4
MLA_KV_RANK = D_MODEL // 8
RET_HEADS = 8
RET_DK = 128
RET_DV = 128
D_FF = 4 * D_MODEL

N_CTX = BATCH * SEQ
N_LAT = DEC_BATCH * DEC_SEQ
N_TOK = N_CTX + N_LAT
COL_TAIL = 2 * D_RNN + MLA_Q_RANK + MLA_KV_RANK + MLA_ROPE
MLA_QH = 256
MLA_SLAB = 1024
GELU_TANH_SCALE = 0.7978845608028654
GELU_TANH_CUBIC = 0.044715

LANES = 128
SUBLANES = 8
VMEM_LIMIT = 56 * 1024 * 1024
ROW_TILE = 256
RES_TILE = 512
LAT_Q_TILE = 512


def _params(*sem):
    return pltpu.CompilerParams(dimension_semantics=sem, vmem_limit_bytes=VMEM_LIMIT)


def _sigmoid(x):
    return 0.5 * (1.0 + jnp.tanh(0.5 * x))


def _softplus(x):
    return jnp.maximum(x, 0.0) + jnp.log(1.0 + jnp.exp(-jnp.abs(x)))


def _rms(x, g):
    return x * lax.rsqrt(jnp.mean(x * x, axis=-1, keepdims=True) + EPS) * g


def _dot(a, b):
    return jnp.dot(a, b, preferred_element_type=F32)


def _dot_nt(a, b):
    return lax.dot_general(a, b, (((1,), (1,)), ((), ())), preferred_element_type=F32)


def _dot_tn(a, b):
    return lax.dot_general(a, b, (((0,), (0,)), ((), ())), preferred_element_type=F32)


def _rope(x, cos, sa, sb, quarter):
    w = x.shape[-1]
    reps = w // cos.shape[-1]
    if reps > 1:
        cos, sa, sb = (jnp.tile(t, (1, reps)) for t in (cos, sa, sb))
    return x * cos + pltpu.roll(x, quarter, 1) * sa + pltpu.roll(x, w - quarter, 1) * sb


def _mod_row(row0):
    return jnp.where(row0 < N_CTX, 0, 1 + (row0 - N_CTX) // DEC_SEQ)


def _ada_kernel(c_ref, w_ref, b_ref, o_ref):
    c = c_ref[...]
    s = c * _sigmoid(c)
    o_ref[...] = _dot(s.astype(BF16), w_ref[...].astype(BF16)) + b_ref[...]


def _ada(cond8, w_mod, b_mod):
    tn = 1024
    n = w_mod.shape[-1]
    return pl.pallas_call(
        _ada_kernel,
        grid=(DEPTH, n // tn),
        in_specs=[pl.BlockSpec((8, D_MODEL), lambda l, j: (0, 0)),
                  pl.BlockSpec((None, D_MODEL, tn), lambda l, j: (l, 0, j)),
                  pl.BlockSpec((None, 1, tn), lambda l, j: (l, 0, j))],
        out_specs=pl.BlockSpec((None, 8, tn), lambda l, j: (l, 0, j)),
        out_shape=jax.ShapeDtypeStruct((DEPTH, 8, n), F32),
        compiler_params=_params("arbitrary", "arbitrary"),
        name="ada",
    )(cond8, w_mod, b_mod.reshape(DEPTH, 1, n))


def _resnorm_kernel(*refs, has_res, has_next):
    refs = list(refs)
    xc_ref, xl_ref = refs[:2]
    refs = refs[2:]
    is_ctx = pl.program_id(0) < N_CTX // RES_TILE
    x = jnp.where(is_ctx, xc_ref[...], xl_ref[...])
    if has_res:
        u_ref, gpost_ref, gate_ref = refs[:3]
        refs = refs[3:]
        x = x + gate_ref[...] * _rms(u_ref[...], gpost_ref[...])
    if has_next:
        gpre_ref, shift_ref, scale_ref = refs[:3]
        refs = refs[3:]
    if has_res:
        xco_ref, xlo_ref = refs[:2]
        refs = refs[2:]

        @pl.when(is_ctx)
        def _():
            xco_ref[...] = x

        @pl.when(jnp.logical_not(is_ctx))
        def _():
            xlo_ref[...] = x
    if has_next:
        ho_ref = refs.pop(0)
        h = _rms(x, gpre_ref[...]) * (1.0 + scale_ref[...]) + shift_ref[...]
        ho_ref[...] = h.astype(BF16)


def _resnorm(x_ctx, x_lat, mod, g_norm, *, res=None, nxt=None):
    tr = RES_TILE
    nctx = N_CTX // tr
    row = pl.BlockSpec((tr, D_MODEL), lambda i: (i, 0))
    crow = pl.BlockSpec((tr, D_MODEL), lambda i: (jnp.minimum(i, nctx - 1), 0))
    lrow = pl.BlockSpec((tr, D_MODEL), lambda i: (jnp.maximum(i - nctx, 0), 0))

    def gspec(k):
        return pl.BlockSpec((None, 1, D_MODEL), lambda i: (k, 0, 0))

    def mspec(l, chunk):
        return pl.BlockSpec((None, 1, D_MODEL), lambda i: (l * 8 + _mod_row(i * tr), 0, chunk))

    args, specs, outs, ospecs = [x_ctx, x_lat], [crow, lrow], [], []
    if res is not None:
        u, gk, l, chunk = res
        args += [u, g_norm, mod]
        specs += [row, gspec(gk), mspec(l, chunk)]
        outs += [jax.ShapeDtypeStruct((N_CTX, D_MODEL), F32), jax.ShapeDtypeStruct((N_LAT, D_MODEL), F32)]
        ospecs += [crow, lrow]
    if nxt is not None:
        gk, l, chunk = nxt
        args += [g_norm, mod, mod]
        specs += [gspec(gk), mspec(l, chunk), mspec(l, chunk + 1)]
        outs.append(jax.ShapeDtypeStruct((N_TOK, D_MODEL), BF16))
        ospecs.append(row)
    return pl.pallas_call(
        functools.partial(_resnorm_kernel, has_res=res is not None, has_next=nxt is not None),
        grid=(N_TOK // tr,),
        in_specs=specs, out_specs=ospecs, out_shape=outs,
        compiler_params=_params("arbitrary"),
        name="resnorm",
    )(*args)


def _mm_kernel(x_ref, w_ref, *rest, nk, act, kscale, rope_quarter, w_t):
    o_ref = rest[-1]
    if nk == 1:
        if w_t:
            part = _dot_nt(x_ref[...].astype(BF16), w_ref[0].astype(BF16))
        else:
            part = _dot(x_ref[...].astype(BF16), w_ref[...].astype(BF16))
        if act == "relu2":
            part = jnp.square(jnp.maximum(part, 0.0))
        elif act == "sigmoid":
            part = _sigmoid(part)
        elif act is not None:
            part = jnp.where(pl.program_id(1) >= act[1], _sigmoid(part), part)
        if kscale is not None:
            part = part * jnp.where(pl.program_id(1) >= kscale[0], kscale[1], 1.0)
        if rope_quarter is not None:
            cos_ref, sa_ref, sb_ref = rest[:3]
            part = _rope(part, cos_ref[...], sa_ref[...], sb_ref[...], rope_quarter)
        o_ref[...] = part.astype(o_ref.dtype)
        return

    @pl.when(pl.program_id(2) == 0)
    def _():
        o_ref[...] = jnp.zeros(o_ref.shape, o_ref.dtype)

    o_ref[...] += _dot(x_ref[...].astype(BF16), w_ref[...].astype(BF16))


def _mm(x, w, l, *, col0=0, ncols=None, out_dtype=F32, act=None, tm=2048, tn=512, tk=None,
        m=None, row_map=None, kscale=None, rope=None, w_t=False):
    kdim = w.shape[2] if w_t else w.shape[1]
    ncols = (w.shape[1] if w_t else w.shape[2]) - col0 if ncols is None else ncols
    m = x.shape[0] if m is None else m
    tk = kdim if tk is None else tk
    tm, tn = min(tm, m), min(tn, ncols)
    assert m % tm == 0 and ncols % tn == 0 and kdim % tk == 0
    nk = kdim // tk
    assert nk == 1 or (act is None and out_dtype == F32 and kscale is None and rope is None and not w_t)
    row_map = (lambda i: i) if row_map is None else row_map
    args = [x, w]
    if w_t:
        assert col0 % SUBLANES == 0 and tn % SUBLANES == 0
        wspec = pl.BlockSpec((pl.Element(1), pl.Element(tn), pl.Element(tk)),
                             lambda i, j, k: (l, pl.multiple_of(col0 + j * tn, SUBLANES), 0))
    else:
        assert col0 % tn == 0
        jb = col0 // tn
        layer = l if callable(l) else (lambda i: l)
        wspec = pl.BlockSpec((None, tk, tn), lambda i, j, k: (layer(i), k, jb + j))
    in_specs = [pl.BlockSpec((tm, tk), lambda i, j, k: (row_map(i), k)), wspec]
    if kscale is not None:
        assert kscale[0] % tn == 0
        kscale = (kscale[0] // tn, kscale[1])
    if isinstance(act, tuple):
        assert act[1] % tn == 0
        act = (act[0], act[1] // tn)
    if rope is not None:
        tabs, quarter = rope
        assert tm == DEC_SEQ and tabs[0].shape == (DEC_SEQ, LANES)
        args += list(tabs)
        in_specs += [pl.BlockSpec((tm, LANES), lambda i, j, k: (0, 0))] * 3
    return pl.pallas_call(
        functools.partial(_mm_kernel, nk=nk, act=act, kscale=kscale,
                          rope_quarter=None if rope is None else rope[1], w_t=w_t),
        grid=(m // tm, ncols // tn, nk),
        in_specs=in_specs,
        out_specs=pl.BlockSpec((tm, tn), lambda i, j, k: (i, j)),
        out_shape=jax.ShapeDtypeStruct((m, ncols), out_dtype),
        compiler_params=_params("arbitrary", "arbitrary", "arbitrary"),
        name="mm",
    )(*args)


def _bm_kernel(a0_ref, a1_ref, a2_ref, w0_ref, w1_ref, w2_ref, g0_ref, g1_ref, g2_ref, o_ref):
    acc = g0_ref[...].astype(F32) * _dot(a0_ref[...], w0_ref[...].astype(BF16))
    acc = acc + g1_ref[...].astype(F32) * _dot(a1_ref[...], w1_ref[...].astype(BF16))
    acc = acc + g2_ref[...].astype(F32) * _dot(a2_ref[...], w2_ref[...].astype(BF16))
    o_ref[...] = acc.astype(BF16)


def _branch_merge(acts, ws, gates, l, *, gate_col0=0, tm=2048, tn=256):
    nj = D_MODEL // tn
    kdim = acts[0].shape[1]
    assert gate_col0 % tn == 0
    g0 = gate_col0 // tn
    aspec = pl.BlockSpec((tm, kdim), lambda i, j: (i, 0))
    wspec = pl.BlockSpec((None, kdim, tn), lambda i, j: (l, 0, j))
    gspecs = [pl.BlockSpec((tm, tn), lambda i, j, b=b: (i, g0 + b * nj + j)) for b in range(3)]
    return pl.pallas_call(
        _bm_kernel,
        grid=(N_TOK // tm, nj),
        in_specs=[aspec] * 3 + [wspec] * 3 + gspecs,
        out_specs=pl.BlockSpec((tm, tn), lambda i, j: (i, j)),
        out_shape=jax.ShapeDtypeStruct((N_TOK, D_MODEL), BF16),
        compiler_params=_params("arbitrary", "arbitrary"),
        name="branch_merge",
    )(*acts, *ws, gates, gates, gates)


LRU_CT = 512


def _lru_kernel(*refs, T, has_h0, has_state):
    xr_ref, xg_ref, cw_ref, cb_ref, wa_ref, ba_ref, wx_ref, bx_ref, lam_ref = refs[:9]
    refs = refs[9:]
    if has_h0:
        h0_ref = refs[0]
        refs = refs[1:]
    if has_state:
        y_ref, st_ref = refs[2:4]
        refs = refs[4:]
    else:
        y_ref = refs[1]
        refs = refs[2:]
    x3_ref, p_ref, a_ref, b_ref, hl_ref, pc_ref, hn_ref = refs
    nblk = xr_ref.shape[1] // LRU_BS
    S = SUBLANES
    G = T // S
    H = CONV_W - 1

    pitch = G + S
    for n in range(nblk):
        for k in range(S):
            x3_ref[n, k * pitch:k * pitch + G, :] = xr_ref[k * G:(k + 1) * G, n * LRU_BS:(n + 1) * LRU_BS]

    def gather(p, _):
        r0 = pl.multiple_of((p + H) * S, S)
        for n in range(nblk):
            p_ref.at[n][pl.ds(r0, S), :] = x3_ref.at[n][pl.ds(p, S, stride=pitch), :]
        return 0

    lax.fori_loop(0, G, gather, 0, unroll=4)
    row = lax.broadcasted_iota(jnp.int32, (S, LRU_BS), 0)
    for n in range(nblk):
        for i in range(H):
            tail = p_ref[n, (G + i) * S:(G + i + 1) * S, :]
            p_ref[n, i * S:(i + 1) * S, :] = jnp.where(row == 0, 0.0, pltpu.roll(tail, 1, 0))
            head = p_ref[n, (H + i) * S:(H + i + 1) * S, :]
            p_ref[n, (H + G + i) * S:(H + G + i + 1) * S, :] = jnp.where(row == S - 1, 0.0,
                                                                        pltpu.roll(head, S - 1, 0))

    cmats = []
    for d in range(2):
        for n in range(nblk):
            cs = slice(n * LRU_BS, (n + 1) * LRU_BS)
            xc = jnp.zeros((T, LRU_BS), F32) + cb_ref[d, :, cs]
            for j in range(CONV_W):
                off = j if d == 0 else 2 * H - j
                xc = xc + cw_ref[d, j:j + 1, cs] * p_ref[n, off * S:off * S + T, :]
            xcb = xc.astype(BF16)
            t_r = jnp.tanh(_dot(xcb, (0.5 * wa_ref[d, n]).astype(BF16)) + 0.5 * ba_ref[d, :, cs])
            t_i = jnp.tanh(_dot(xcb, (0.5 * wx_ref[d, n]).astype(BF16)) + 0.5 * bx_ref[d, :, cs])
            half_c = (-0.5 * LRU_C) * _softplus(-lam_ref[d, :, cs])
            a = jnp.exp(half_c + half_c * t_r)
            a_ref[n] = a
            b_ref[n] = (0.5 * jnp.sqrt(1.0 - a * a)) * (xc + t_i * xc)

        def body(it, carry, d=d):
            p = it if d == 0 else G - 1 - it
            r0 = pl.multiple_of(p * S, S)
            out = []
            for n in range(nblk):
                h, pc = carry[n]
                av = a_ref.at[n][pl.ds(r0, S), :]
                h = av * h + b_ref.at[n][pl.ds(r0, S), :]
                pc = av * pc
                hl_ref.at[d, n][pl.ds(r0, S), :] = h
                pc_ref.at[d, n][pl.ds(r0, S), :] = pc
                out.append((h, pc))
            return tuple(out)

        init = tuple((jnp.zeros((S, LRU_BS), F32), jnp.ones((S, LRU_BS), F32)) for _ in range(nblk))
        ends = lax.fori_loop(0, G, body, init, unroll=4)

        last_d, cmat_d = [], []
        for n in range(nblk):
            cs = slice(n * LRU_BS, (n + 1) * LRU_BS)
            h_end, pc_end = ends[n]
            c = h0_ref[d, :, cs] if has_h0 else jnp.zeros((1, LRU_BS), F32)
            rows = [None] * S
            for k in (range(S) if d == 0 else range(S - 1, -1, -1)):
                rows[k] = c
                c = h_end[k:k + 1, :] + pc_end[k:k + 1, :] * c
            last_d.append(c)
            cmat_d.append(jnp.concatenate(rows, axis=0))
        cmats.append(cmat_d)
        if has_state:
            for n in range(nblk):
                st_ref[d:d + 1, n * LRU_BS:(n + 1) * LRU_BS] = last_d[n]

    def scatter(p, _):
        r0 = pl.multiple_of(p * S, S)
        for n in range(nblk):
            h = (hl_ref.at[0, n][pl.ds(r0, S), :] + pc_ref.at[0, n][pl.ds(r0, S), :] * cmats[0][n]
                 + hl_ref.at[1, n][pl.ds(r0, S), :] + pc_ref.at[1, n][pl.ds(r0, S), :] * cmats[1][n])
            hn_ref.at[n][pl.ds(p, S, stride=pitch), :] = h
        return 0

    lax.fori_loop(0, G, scatter, 0, unroll=4)

    for n in range(nblk):
        cs = slice(n * LRU_BS, (n + 1) * LRU_BS)
        for k in range(S):
            xg = xg_ref[k * G:(k + 1) * G, cs]
            gelu = 0.5 * xg * (1.0 + jnp.tanh(GELU_TANH_SCALE * (xg + GELU_TANH_CUBIC * (xg * xg * xg))))
            y_ref[k * G:(k + 1) * G, cs] = (hn_ref[n, k * pitch:k * pitch + G, :] * gelu).astype(BF16)


def _lru(proj_x, lw, l, *, T, nb, blk0, prev, h0=None, st_prev=None):
    ct = LRU_CT
    ncb = D_RNN // ct
    bpc = ct // LRU_BS
    in_specs = [
        pl.BlockSpec((T, ct), lambda b, c: (blk0 + b, c)),
        pl.BlockSpec((T, ct), lambda b, c: (blk0 + b, ncb + c)),
        pl.BlockSpec((None, 2, CONV_W, ct), lambda b, c: (l, 0, 0, c)),
        pl.BlockSpec((None, 2, 1, ct), lambda b, c: (l, 0, 0, c)),
        pl.BlockSpec((None, 2, bpc, LRU_BS, LRU_BS), lambda b, c: (l, 0, c, 0, 0)),
        pl.BlockSpec((None, 2, 1, ct), lambda b, c: (l, 0, 0, c)),
        pl.BlockSpec((None, 2, bpc, LRU_BS, LRU_BS), lambda b, c: (l, 0, c, 0, 0)),
        pl.BlockSpec((None, 2, 1, ct), lambda b, c: (l, 0, 0, c)),
        pl.BlockSpec((None, 2, 1, ct), lambda b, c: (l, 0, 0, c)),
    ]
    args = [proj_x, proj_x, lw["conv_w"], lw["conv_b"], lw["wa"], lw["ba"], lw["wx"], lw["bx"], lw["lam"]]
    if h0 is not None:
        in_specs.append(pl.BlockSpec((None, None, 2, 1, ct), lambda b, c: (b, l, 0, 0, c)))
        args.append(h0)
    has_state = st_prev is not None
    aliases = {len(args): 0}
    in_specs.append(pl.BlockSpec(memory_space=pl.ANY))
    args.append(prev)
    out_specs = [pl.BlockSpec((T, ct), lambda b, c: (blk0 + b, c))]
    out_shape = [jax.ShapeDtypeStruct(prev.shape, prev.dtype)]
    if has_state:
        aliases[len(args)] = 1
        in_specs.append(pl.BlockSpec(memory_space=pl.ANY))
        args.append(st_prev)
        out_specs.append(pl.BlockSpec((None, None, 2, ct), lambda b, c: (b, l, 0, c)))
        out_shape.append(jax.ShapeDtypeStruct(st_prev.shape, st_prev.dtype))
    return pl.pallas_call(
        functools.partial(_lru_kernel, T=T, has_h0=h0 is not None, has_state=has_state),
        grid=(nb, ncb),
        in_specs=in_specs,
        out_specs=out_specs,
        out_shape=out_shape,
        input_output_aliases=aliases,
        scratch_shapes=[pltpu.VMEM((bpc, T + SUBLANES * SUBLANES, LRU_BS), F32),
                        pltpu.VMEM((bpc, T + 2 * (CONV_W - 1) * SUBLANES, LRU_BS), F32),
                        pltpu.VMEM((bpc, T, LRU_BS), F32), pltpu.VMEM((bpc, T, LRU_BS), F32),
                        pltpu.VMEM((2, bpc, T, LRU_BS), F32), pltpu.VMEM((2, bpc, T, LRU_BS), F32),
                        pltpu.VMEM((bpc, T + SUBLANES * SUBLANES, LRU_BS), F32)],
        compiler_params=_params("arbitrary", "arbitrary"),
        name="lru",
    )(*args)


MLA_TILE = DEC_SEQ
MLA_TILE_SEQS = MLA_TILE // SEQ


def _mla_front_kernel(p_ref, gq_ref, gkv_ref, wuq_ref, wukv_ref, cos_ref, sa_ref, sb_ref, *rest):
    q_ref, kv_ref, kr_ref, ckv_out_ref, kr_out_ref = rest[-5:]
    i = pl.program_id(0)
    cq = _rms(p_ref[:, 0:MLA_Q_RANK], gq_ref[...]).astype(BF16)
    q_ref[...] = _dot(cq, wuq_ref[...].astype(BF16)).astype(BF16)
    ckv = _rms(p_ref[:, MLA_Q_RANK:MLA_Q_RANK + MLA_KV_RANK], gkv_ref[...])
    kv_ref[...] = _dot(ckv.astype(BF16), wukv_ref[...].astype(BF16)).astype(BF16)
    k0 = MLA_Q_RANK + MLA_KV_RANK
    kr = p_ref[:, k0:k0 + LANES]

    @pl.when(i < N_CTX // MLA_TILE)
    def _():
        kr_ref[...] = kr
        for s in range(MLA_TILE_SEQS):
            ckv_out_ref[s] = ckv[s * SEQ:(s + 1) * SEQ, :]
            kr_out_ref[s] = kr[s * SEQ:(s + 1) * SEQ, 0:MLA_ROPE]

    @pl.when(i >= N_CTX // MLA_TILE)
    def _():
        kr_ref[...] = _rope(kr, cos_ref[...], sa_ref[...], sb_ref[...], MLA_ROPE // 4)


def _mla_front(proj_m, gq, gkv, wuq, wukv, l, tab, caches):
    tr = MLA_TILE
    last_ctx = N_CTX // tr - 1
    tspec = pl.BlockSpec((tr, LANES), lambda i: (0, 0))
    kvw = MLA_HEADS * MLA_QH
    args = [proj_m, gq, gkv, wuq, wukv, *tab]
    in_specs = [pl.BlockSpec((tr, MLA_SLAB), lambda i: (i, 2 * D_RNN // MLA_SLAB)),
                pl.BlockSpec((None, 1, MLA_Q_RANK), lambda i: (l, 0, 0)),
                pl.BlockSpec((None, 1, MLA_KV_RANK), lambda i: (l, 0, 0)),
                pl.BlockSpec((None, MLA_Q_RANK, kvw), lambda i: (l, 0, 0)),
                pl.BlockSpec((None, MLA_KV_RANK, kvw), lambda i: (l, 0, 0)),
                tspec, tspec, tspec]
    aliases = {len(args): 3, len(args) + 1: 4}
    args += list(caches)
    in_specs += [pl.BlockSpec(memory_space=pl.ANY)] * 2
    return pl.pallas_call(
        _mla_front_kernel,
        grid=(N_TOK // tr,),
        in_specs=in_specs,
        out_specs=[pl.BlockSpec((tr, kvw), lambda i: (i, 0)),
                   pl.BlockSpec((tr, kvw), lambda i: (i, 0)),
                   pl.BlockSpec((tr, LANES), lambda i: (i, 0)),
                   pl.BlockSpec((MLA_TILE_SEQS, None, SEQ, MLA_KV_RANK),
                                lambda i: (jnp.minimum(i, last_ctx), l, 0, 0)),
                   pl.BlockSpec((MLA_TILE_SEQS, None, SEQ, MLA_ROPE),
                                lambda i: (jnp.minimum(i, last_ctx), l, 0, 0))],
        out_shape=[jax.ShapeDtypeStruct((N_TOK, kvw), BF16),
                   jax.ShapeDtypeStruct((N_TOK, kvw), BF16),
                   jax.ShapeDtypeStruct((N_TOK, LANES), F32),
                   jax.ShapeDtypeStruct((BATCH, DEPTH, SEQ, MLA_KV_RANK), F32),
                   jax.ShapeDtypeStruct((BATCH, DEPTH, SEQ, MLA_ROPE), F32)],
        input_output_aliases=aliases,
        compiler_params=_params("arbitrary"),
        name="mla_front",
    )(*args)


def _attn_kernel(*refs, lat):
    if lat:
        (q_ref, kvn_ref, krn_ref, kvc_ref, krc_ref, cos_ref, sa_ref, sb_ref, _,
         o_ref, kcat_ref, vcat_ref) = refs
    else:
        q_ref, kvn_ref, krn_ref, _, o_ref, kcat_ref = refs
    scale = (MLA_NOPE + MLA_ROPE) ** -0.5
    s_new = kvn_ref.shape[0]
    off = PAST_LEN if lat else 0

    def build():
        lane = lax.broadcasted_iota(jnp.int32, (s_new, LANES), 1)
        kr_new = jnp.where(lane < MLA_ROPE, krn_ref[...], 0.0).astype(BF16)
        for h in range(MLA_HEADS):
            c0 = h * MLA_QH
            kcat_ref[h, off:off + s_new, 0:MLA_NOPE] = kvn_ref[:, c0:c0 + MLA_NOPE]
            kcat_ref[h, off:off + s_new, MLA_NOPE:MLA_QH] = kr_new
            if lat:
                kcat_ref[h, 0:off, 0:MLA_NOPE] = kvc_ref[:, c0:c0 + MLA_NOPE]
                kcat_ref[h, 0:off, MLA_NOPE:MLA_NOPE + MLA_ROPE] = krc_ref[...].astype(BF16)
                kcat_ref[h, 0:off, MLA_NOPE + MLA_ROPE:MLA_QH] = jnp.zeros((off, MLA_QH - MLA_NOPE - MLA_ROPE), BF16)
                vcat_ref[0:off, h * MLA_V:(h + 1) * MLA_V] = kvc_ref[:, c0 + MLA_NOPE:c0 + MLA_QH]
                vcat_ref[off:off + s_new, h * MLA_V:(h + 1) * MLA_V] = kvn_ref[:, c0 + MLA_NOPE:c0 + MLA_QH]

    if lat:
        pl.when(pl.program_id(1) == 0)(build)
    else:
        build()

    def scores(h):
        c0 = h * MLA_QH
        qn = q_ref[:, c0:c0 + MLA_NOPE].astype(F32)
        qr = q_ref[:, c0 + MLA_NOPE:c0 + MLA_QH].astype(F32)
        if lat:
            qr = _rope(qr, cos_ref[...], sa_ref[...], sb_ref[...], MLA_ROPE // 4)
        qh = (jnp.concatenate([qn, qr], axis=1) * scale).astype(BF16)
        return _dot_nt(qh, kcat_ref[h])

    def attend(h, s):
        p = jnp.exp(s - jnp.max(s, axis=-1, keepdims=True))
        den = jnp.sum(p, axis=-1, keepdims=True)
        v = (vcat_ref[:, h * MLA_V:(h + 1) * MLA_V] if lat
             else kvn_ref[:, h * MLA_QH + MLA_NOPE:(h + 1) * MLA_QH])
        o_ref[:, h * MLA_V:(h + 1) * MLA_V] = (_dot(p.astype(BF16), v) / den).astype(BF16)

    if lat:
        s_next = scores(0)
        for h in range(MLA_HEADS):
            s_cur = s_next
            if h + 1 < MLA_HEADS:
                s_next = scores(h + 1)
            attend(h, s_cur)
    else:
        all_scores = [scores(h) for h in range(MLA_HEADS)]
        for h in range(MLA_HEADS):
            attend(h, all_scores[h])


def _attn_ctx(q, kv, kr, prev):
    t = SEQ
    return pl.pallas_call(
        functools.partial(_attn_kernel, lat=False),
        grid=(BATCH,),
        in_specs=[pl.BlockSpec((t, MLA_HEADS * MLA_QH), lambda b: (b, 0)),
                  pl.BlockSpec((t, MLA_HEADS * MLA_QH), lambda b: (b, 0)),
                  pl.BlockSpec((t, LANES), lambda b: (b, 0)),
                  pl.BlockSpec(memory_space=pl.ANY)],
        out_specs=pl.BlockSpec((t, MLA_HEADS * MLA_V), lambda b: (b, 0)),
        out_shape=jax.ShapeDtypeStruct(prev.shape, prev.dtype),
        input_output_aliases={3: 0},
        scratch_shapes=[pltpu.VMEM((MLA_HEADS, t, MLA_QH), BF16)],
        compiler_params=_params("arbitrary"),
        name="attn_ctx",
    )(q, kv, kr, prev)


def _attn_lat(q, kv, kr, kv_cache, cache_krope, l, tab, prev):
    tq = ROW_TILE
    nq = DEC_SEQ // tq
    blk0 = N_CTX // DEC_SEQ
    qblk0 = N_CTX // tq
    tspec = pl.BlockSpec((tq, LANES), lambda b, i: (i, 0))
    return pl.pallas_call(
        functools.partial(_attn_kernel, lat=True),
        grid=(DEC_BATCH, nq),
        in_specs=[pl.BlockSpec((tq, MLA_HEADS * MLA_QH), lambda b, i: (qblk0 + b * nq + i, 0)),
                  pl.BlockSpec((DEC_SEQ, MLA_HEADS * MLA_QH), lambda b, i: (blk0 + b, 0)),
                  pl.BlockSpec((DEC_SEQ, LANES), lambda b, i: (blk0 + b, 0)),
                  pl.BlockSpec((PAST_LEN, MLA_HEADS * MLA_QH), lambda b, i: (b * DEPTH + l, 0)),
                  pl.BlockSpec((None, None, PAST_LEN, MLA_ROPE), lambda b, i: (b, l, 0, 0)),
                  tspec, tspec, tspec, pl.BlockSpec(memory_space=pl.ANY)],
        out_specs=pl.BlockSpec((tq, MLA_HEADS * MLA_V), lambda b, i: (qblk0 + b * nq + i, 0)),
        out_shape=jax.ShapeDtypeStruct((N_TOK, MLA_HEADS * MLA_V), BF16),
        input_output_aliases={8: 0},
        scratch_shapes=[pltpu.VMEM((MLA_HEADS, PAST_LEN + DEC_SEQ, MLA_QH), BF16),
                        pltpu.VMEM((PAST_LEN + DEC_SEQ, MLA_HEADS * MLA_V), BF16)],
        compiler_params=_params("arbitrary", "arbitrary"),
        name="attn_lat",
    )(q, kv, kr, kv_cache, cache_krope, *tab, prev)


def _log_sigmoid(x):
    return -_softplus(-x)


def _ret_kernel(*refs, lat, T):
    if lat:
        q_ref, k_ref, v_ref, rg_ref, dec_ref, s0_ref, _, o_ref, decay_ref = refs
    else:
        q_ref, k_ref, v_ref, rg_ref, dec_ref = refs[:5]
        o_ref, st_ref, decay_ref = refs[-3:]
    tq = q_ref.shape[0]
    t0 = pl.program_id(0) * tq if lat else 0
    batch_id = pl.program_id(1) if lat else pl.program_id(0)
    lg = _log_sigmoid(dec_ref[...])
    tcol = (t0 + lax.broadcasted_iota(jnp.int32, (tq, 1), 0)).astype(F32)

    @pl.when(batch_id == 0)
    def _():
        rows = (t0 + lax.broadcasted_iota(jnp.int32, (tq, T), 0)).astype(F32)
        cols = lax.broadcasted_iota(jnp.int32, (tq, T), 1).astype(F32)
        diff = rows - cols
        fwd = diff >= 0.0
        dist = jnp.abs(diff)
        for h in range(RET_HEADS):
            one_sided = jnp.exp(jnp.where(fwd, lg[0:1, h:h + 1], lg[1:2, h:h + 1]) * dist)
            decay_ref[h] = jnp.where(dist == 0.0, 2.0, one_sided)

    heads = [slice(h * RET_DK, (h + 1) * RET_DK) for h in range(RET_HEADS)]
    if lat:
        outs = []
        s_next = _dot_nt(q_ref[:, heads[0]], k_ref[:, heads[0]])
        for h, cs in enumerate(heads):
            s_cur = s_next
            if h + 1 < RET_HEADS:
                s_next = _dot_nt(q_ref[:, heads[h + 1]], k_ref[:, heads[h + 1]])
            outs.append(_dot((s_cur * decay_ref[h]).astype(BF16), v_ref[:, cs]))
    else:
        scores = [_dot_nt(q_ref[:, cs], k_ref[:, cs]) for cs in heads]
        probs = [(scores[h] * decay_ref[h]).astype(BF16) for h in range(RET_HEADS)]
        outs = [_dot(probs[h], v_ref[:, cs]) for h, cs in enumerate(heads)]
    for h, cs in enumerate(heads):
        lgf = lg[0:1, h:h + 1]
        lgb = lg[1:2, h:h + 1]
        if lat:
            qf = q_ref[:, cs].astype(F32)
            outs[h] = outs[h] + _dot((qf * jnp.exp(lgf * (tcol + 1.0))).astype(BF16), s0_ref[0, h].astype(BF16))
            outs[h] = outs[h] + _dot((qf * jnp.exp(lgb * (T - tcol))).astype(BF16), s0_ref[1, h].astype(BF16))
        else:
            vf = v_ref[:, cs].astype(F32)
            k = k_ref[:, cs]
            st_ref[0, h] = _dot_tn(k, (vf * jnp.exp(lgf * (T - 1.0 - tcol))).astype(BF16))
            st_ref[1, h] = _dot_tn(k, (vf * jnp.exp(lgb * tcol)).astype(BF16))
    for h, cs in enumerate(heads):
        o = outs[h]
        mu = jnp.mean(o, axis=-1, keepdims=True)
        oc = o - mu
        var = jnp.mean(oc * oc, axis=-1, keepdims=True)
        rg = rg_ref[:, cs].astype(F32)
        o_ref[:, cs] = (oc * lax.rsqrt(var + EPS) * (rg * _sigmoid(rg))).astype(BF16)


def _ret_ctx(qk, vg, decay, l, prev, st_prev):
    t = SEQ
    w = RET_HEADS * RET_DK
    return pl.pallas_call(
        functools.partial(_ret_kernel, lat=False, T=t),
        grid=(BATCH,),
        in_specs=[pl.BlockSpec((t, w), lambda b: (b, 0)),
                  pl.BlockSpec((t, w), lambda b: (b, 1)),
                  pl.BlockSpec((t, w), lambda b: (b, 0)),
                  pl.BlockSpec((t, w), lambda b: (b, 1)),
                  pl.BlockSpec((None, 2, RET_HEADS), lambda b: (l, 0, 0)),
                  pl.BlockSpec(memory_space=pl.ANY), pl.BlockSpec(memory_space=pl.ANY)],
        out_specs=[pl.BlockSpec((t, w), lambda b: (b, 0)),
                   pl.BlockSpec((None, None, 2, RET_HEADS, RET_DK, RET_DV), lambda b: (b, l, 0, 0, 0, 0))],
        out_shape=[jax.ShapeDtypeStruct(prev.shape, prev.dtype),
                   jax.ShapeDtypeStruct(st_prev.shape, st_prev.dtype)],
        input_output_aliases={5: 0, 6: 1},
        scratch_shapes=[pltpu.VMEM((RET_HEADS, t, t), F32)],
        compiler_params=_params("arbitrary"),
        name="ret_ctx",
    )(qk, qk, vg, vg, decay, prev, st_prev)


def _ret_lat(qk, vg, decay, state_ret, l, prev):
    tq = LAT_Q_TILE
    t = DEC_SEQ
    nq = t // tq
    w = RET_HEADS * RET_DK
    blk0 = N_CTX // t
    qblk0 = N_CTX // tq
    return pl.pallas_call(
        functools.partial(_ret_kernel, lat=True, T=t),
        grid=(nq, DEC_BATCH),
        in_specs=[pl.BlockSpec((tq, w), lambda i, b: (b * nq + i, 0)),
                  pl.BlockSpec((t, w), lambda i, b: (b, 1)),
                  pl.BlockSpec((t, w), lambda i, b: (blk0 + b, 0)),
                  pl.BlockSpec((tq, w), lambda i, b: (qblk0 + b * nq + i, 1)),
                  pl.BlockSpec((None, 2, RET_HEADS), lambda i, b: (l, 0, 0)),
                  pl.BlockSpec((None, None, 2, RET_HEADS, RET_DK, RET_DV), lambda i, b: (b, l, 0, 0, 0, 0)),
                  pl.BlockSpec(memory_space=pl.ANY)],
        out_specs=pl.BlockSpec((tq, w), lambda i, b: (qblk0 + b * nq + i, 0)),
        out_shape=jax.ShapeDtypeStruct((N_TOK, w), BF16),
        scratch_shapes=[pltpu.VMEM((RET_HEADS, tq, t), F32)],
        input_output_aliases={6: 0},
        compiler_params=_params("arbitrary", "arbitrary"),
        name="ret_lat",
    )(qk, qk, vg, vg, decay, state_ret, prev)


def _rope_tables(dim):
    half = dim // 2
    quarter = half // 2
    t = jnp.arange(DEC_SEQ)
    row = (t // GRID_W).astype(F32)
    col = (t % GRID_W).astype(F32)
    inv = ROPE_BASE ** (-jnp.arange(0, half, 2, dtype=F32) / half)
    zeros = jnp.zeros((DEC_SEQ, quarter), F32)
    cos_parts, sa_parts, sb_parts = [], [], []
    for pos in (row, col):
        ang = pos[:, None] * inv[None, :]
        c, s = jnp.cos(ang), jnp.sin(ang)
        cos_parts += [c, c]
        sa_parts += [zeros, s]
        sb_parts += [-s, zeros]
    padw = LANES - dim

    def build(parts):
        return jnp.concatenate(parts + ([jnp.zeros((DEC_SEQ, padw), F32)] if padw else []), axis=-1)

    return build(cos_parts), build(sa_parts), build(sb_parts)


def kernel(x_prompt, x_sample, c, cache_mla_ckv, cache_mla_krope, state_lru, state_ret, c_ctx, w_mod, b_mod,
           g_norm, w_in, lru_conv_w, lru_conv_b, lru_wa, lru_ba, lru_wx, lru_bx, lru_lam, mla_gq, mla_gkv,
           mla_wuq, mla_wukv, ret_decay, w_br_lru, w_br_mla, w_br_ret, w_out, w_ff1, w_ff2):
    x_ctx = x_prompt.reshape(N_CTX, D_MODEL)
    x_lat = x_sample.reshape(N_LAT, D_MODEL)
    cond8 = jnp.concatenate([c_ctx[None, :], c, jnp.zeros((8 - 1 - DEC_BATCH, D_MODEL), F32)], axis=0)
    mod = _ada(cond8, w_mod, b_mod).reshape(DEPTH * 8, 1, 6 * D_MODEL)
    g_all = g_norm.reshape(DEPTH * 4, 1, D_MODEL)

    w_in_t = jnp.swapaxes(w_in, 1, 2)
    wuq = jnp.pad(mla_wuq.reshape(DEPTH, MLA_Q_RANK, MLA_HEADS, MLA_NOPE + MLA_ROPE),
                  ((0, 0), (0, 0), (0, 0), (0, MLA_QH - MLA_NOPE - MLA_ROPE)))
    wuq = wuq.reshape(DEPTH, MLA_Q_RANK, MLA_HEADS * MLA_QH)
    lw = {"conv_w": lru_conv_w, "conv_b": lru_conv_b.reshape(DEPTH, 2, 1, D_RNN), "wa": lru_wa,
          "ba": lru_ba.reshape(DEPTH, 2, 1, D_RNN), "wx": lru_wx, "bx": lru_bx.reshape(DEPTH, 2, 1, D_RNN),
          "lam": lru_lam.reshape(DEPTH, 2, 1, D_RNN)}
    gq = mla_gq.reshape(DEPTH, 1, MLA_Q_RANK)
    gkv = mla_gkv.reshape(DEPTH, 1, MLA_KV_RANK)
    h0_lat = state_lru.reshape(DEC_BATCH, DEPTH, 2, 1, D_RNN)
    cache_ckv_rows = cache_mla_ckv.reshape(DEC_BATCH * DEPTH * PAST_LEN, MLA_KV_RANK)
    tab_mla = _rope_tables(MLA_ROPE)
    tab_ret = _rope_tables(RET_DK)
    wq = RET_HEADS * RET_DK

    caches = (jnp.zeros((BATCH, DEPTH, SEQ, MLA_KV_RANK), F32), jnp.zeros((BATCH, DEPTH, SEQ, MLA_ROPE), F32))
    st_lru = jnp.zeros((BATCH, DEPTH, 2, D_RNN), F32)
    st_ret = jnp.zeros((BATCH, DEPTH, 2, RET_HEADS, RET_DK, RET_DV), F32)
    y_lru = jnp.zeros((N_TOK, D_RNN), BF16)
    y_mla = jnp.zeros((N_TOK, MLA_HEADS * MLA_V), BF16)
    y_ret = jnp.zeros((N_TOK, RET_HEADS * RET_DV), BF16)

    kv_cache = _mm(cache_ckv_rows, mla_wukv, lambda i: i % DEPTH, out_dtype=BF16, tm=PAST_LEN,
                   tn=MLA_HEADS * MLA_QH)

    h, = _resnorm(x_ctx, x_lat, mod, g_all, nxt=(0, 0, 0))
    for l in range(DEPTH):
        proj_a = _mm(h, w_in_t, l, col0=0, ncols=2 * D_RNN + MLA_SLAB, w_t=True)
        kscale = (wq, RET_DK ** -0.5)
        qk_ctx = _mm(h, w_in_t, l, col0=COL_TAIL, ncols=2 * wq, out_dtype=BF16, m=N_CTX, kscale=kscale,
                     w_t=True)
        qk_lat = _mm(h, w_in_t, l, col0=COL_TAIL, ncols=2 * wq, out_dtype=BF16, m=N_LAT, tm=DEC_SEQ,
                     row_map=lambda i: N_CTX // DEC_SEQ + i, kscale=kscale,
                     rope=(tab_ret, RET_DK // 4), w_t=True)
        vg = _mm(h, w_in_t, l, col0=COL_TAIL + 2 * wq, ncols=2 * wq + 3 * D_MODEL, out_dtype=BF16,
                 act=("sigmoid_from", 2 * wq), w_t=True, tn=1024)

        y_lru, st_lru = _lru(proj_a, lw, l, T=SEQ, nb=BATCH, blk0=0, prev=y_lru, st_prev=st_lru)
        y_lru, = _lru(proj_a, lw, l, T=DEC_SEQ, nb=DEC_BATCH, blk0=N_CTX // DEC_SEQ, h0=h0_lat, prev=y_lru)

        q, kv, kr, *caches = _mla_front(proj_a, gq, gkv, wuq, mla_wukv, l, tab_mla, caches)
        y_mla = _attn_ctx(q, kv, kr, y_mla)
        y_mla = _attn_lat(q, kv, kr, kv_cache, cache_mla_krope, l, tab_mla, y_mla)

        y_ret, st_ret = _ret_ctx(qk_ctx, vg, ret_decay, l, y_ret, st_ret)
        y_ret = _ret_lat(qk_lat, vg, ret_decay, state_ret, l, y_ret)

        merged = _branch_merge((y_lru, y_mla, y_ret), (w_br_lru, w_br_mla, w_br_ret), vg, l, gate_col0=2 * wq)
        u = _mm(merged, w_out, l)
        x_ctx, x_lat, h2 = _resnorm(x_ctx, x_lat, mod, g_all, res=(u, l * 4 + 1, l, 2), nxt=(l * 4 + 2, l, 3))
        ff = _mm(h2, w_ff1, l, out_dtype=BF16, act="relu2", tn=1024)
        y = _mm(ff, w_ff2, l, tn=1024, tk=1024)
        if l + 1 < DEPTH:
            x_ctx, x_lat, h = _resnorm(x_ctx, x_lat, mod, g_all, res=(y, l * 4 + 3, l, 5),
                                       nxt=((l + 1) * 4, l + 1, 0))
        else:
            x_ctx, x_lat = _resnorm(x_ctx, x_lat, mod, g_all, res=(y, l * 4 + 3, l, 5))

    new_ckv, new_krope = caches
    return (x_ctx.reshape(BATCH, SEQ, D_MODEL), x_lat.reshape(DEC_BATCH, DEC_SEQ, D_MODEL),
            new_ckv, new_krope, st_lru, st_ret)
```
